```python
import math
import jax, jax.numpy as jnp
from jax import lax
import numpy as np

D_MODEL = 1024
BATCH = 8
SEQ = 2048
DEPTH = 4
DEC_BATCH = 128
DEC_SEQ = 4
PAST_LEN = 16384
PAGE_SIZE = 128

N_MIXERS = 3
N_RWKV = (DEPTH + 2) // 3
N_GDN = (DEPTH + 1) // 3
N_HGRN = DEPTH // 3
N_VRES = N_RWKV - 1

RWKV_N = 64
RWKV_H = D_MODEL // RWKV_N
DECAY_LORA = 64
AAA_LORA = 64
MV_LORA = 32
GATE_LORA = 128
RWKV_LNX_EPS = 64e-5

GDN_H = 8
GDN_DK = 128
GDN_DV = 128
GDN_CONV = 4
GDN_CHUNK = 64
GDN_QK = GDN_H * GDN_DK
GDN_VD = GDN_H * GDN_DV
GDN_CONV_DIM = 2 * GDN_QK + GDN_VD
GDN_IN = GDN_CONV_DIM + GDN_VD + 2 * GDN_H

HGRN_DK = 128
HGRN_H = D_MODEL // HGRN_DK
HGRN_DV = D_MODEL // HGRN_H
HGRN_CHUNK = 32
HGRN_FK = HGRN_H * HGRN_DK
HGRN_VD = HGRN_H * HGRN_DV
HGRN_IN = 2 * HGRN_FK + 2 * HGRN_VD

D_FF = 2816
FFN_CONV = 3

NORM_EPS = 1e-6
L2_EPS = 1e-6

kernel_name = 'hybrid_rwkv7_gdn_hgrn2_convffn_step'


def _rmsnorm(x, w):
    xf = x.astype(jnp.float32)
    y = xf * lax.rsqrt(jnp.mean(xf * xf, axis=-1, keepdims=True) + NORM_EPS)
    return (y * w.astype(jnp.float32)).astype(x.dtype)


def _l2norm(x):
    xf = x.astype(jnp.float32)
    return xf * lax.rsqrt(jnp.sum(xf * xf, axis=-1, keepdims=True) + L2_EPS)


def _causal_conv(x, hist, w):
    width, T = w.shape[0], x.shape[1]
    xp = jnp.concatenate([hist.astype(x.dtype), x], axis=1)
    y = sum(xp[:, j:j + T] * w[j] for j in range(width))
    return y, xp[:, xp.shape[1] - (width - 1):]


def _to_chunks(t, C):
    B, T = t.shape[0], t.shape[1]
    t = t.reshape((B, T // C, C) + t.shape[2:])
    return jnp.moveaxis(jnp.moveaxis(t, 1, 0), 2, 3)


def _from_chunks(o):
    n, B, H, C = o.shape[:4]
    o = jnp.moveaxis(jnp.moveaxis(o, 3, 2), 0, 1)
    return o.reshape((B, n * C, H) + o.shape[4:])


def _rwkv7_scan(r, w, k, v, kk, a, S0):
    def step(S, inp):
        r_t, w_t, k_t, v_t, kk_t, a_t = inp
        sa = jnp.einsum('bhvk,bhk->bhv', S, kk_t)
        S = (S * w_t[:, :, None, :] - sa[..., None] * (kk_t * a_t)[:, :, None, :]
             + v_t[..., None] * k_t[:, :, None, :])
        return S, jnp.einsum('bhvk,bhk->bhv', S, r_t)
    xs = tuple(jnp.moveaxis(t, 1, 0) for t in (r, w, k, v, kk, a))
    S, o = lax.scan(step, S0.astype(jnp.float32), xs)
    return jnp.moveaxis(o, 0, 1), S


def _gated_delta_chunked(q, k, v, g, beta, S0):
    T = q.shape[1]
    C = math.gcd(T, GDN_CHUNK)
    q, k, v, g, beta = (_to_chunks(t.astype(jnp.float32), C) for t in (q, k, v, g, beta))
    gc = jnp.cumsum(g, axis=-1)
    idx = jnp.arange(C)
    causal = idx[:, None] >= idx[None, :]
    strict = idx[:, None] > idx[None, :]
    decay = jnp.exp(jnp.where(causal, gc[..., :, None] - gc[..., None, :], -jnp.inf))
    kb = k * beta[..., None]
    L = jnp.where(strict, jnp.einsum('nbhik,nbhjk->nbhij', kb, k) * decay, 0.0)
    eye = jnp.eye(C, dtype=jnp.float32)
    Tinv = lax.linalg.triangular_solve(eye + L, jnp.broadcast_to(eye, L.shape),
                                       left_side=True, lower=True, unit_diagonal=True)
    u = Tinv @ (v * beta[..., None])
    wk = Tinv @ (kb * jnp.exp(gc)[..., None])
    qk = jnp.einsum('nbhik,nbhjk->nbhij', q, k) * decay
    qg = q * jnp.exp(gc)[..., None]
    g_last = gc[..., -1]
    kd = k * jnp.exp(g_last[..., None] - gc)[..., None]

    def step(S, inp):
        u_c, w_c, qk_c, qg_c, kd_c, gl_c = inp
        v_new = u_c - w_c @ S
        o = qg_c @ S + qk_c @ v_new
        S = S * jnp.exp(gl_c)[..., None, None] + jnp.einsum('bhck,bhcv->bhkv', kd_c, v_new)
        return S, o
    S, o = lax.scan(step, S0.astype(jnp.float32), (u, wk, qk, qg, kd, g_last))
    return _from_chunks(o), S


def _gla_chunked(q, k, v, logf, S0):
    T = q.shape[1]
    C = math.gcd(T, HGRN_CHUNK)
    q, k, v, logf = (_to_chunks(t.astype(jnp.float32), C) for t in (q, k, v, logf))
    gc = jnp.cumsum(logf, axis=3)
    idx = jnp.arange(C)
    causal = (idx[:, None] >= idx[None, :])[:, :, None]

    def step(S, inp):
        q_c, k_c, v_c, g_c = inp
        dec = jnp.exp(jnp.where(causal, g_c[..., :, None, :] - g_c[..., None, :, :], -jnp.inf))
        A = jnp.einsum('bhik,bhjk,bhijk->bhij', q_c, k_c, dec)
        o = (q_c * jnp.exp(g_c)) @ S + A @ v_c
        gl = g_c[..., -1, :]
        S = S * jnp.exp(gl)[..., None] + jnp.einsum('bhck,bhcv->bhkv', k_c * jnp.exp(gl[..., None, :] - g_c), v_c)
        return S, o
    S, o = lax.scan(step, S0.astype(jnp.float32), (q, k, v, gc))
    return _from_chunks(o), S


def _rwkv7_mixer(x, shift0, S0, v_first, P, j):
    B, T, D = x.shape
    hd = lambda t: t.reshape(B, T, RWKV_H, RWKV_N).astype(jnp.float32)
    prev = jnp.concatenate([shift0[:, None].astype(x.dtype), x[:, :-1]], axis=1)
    mixed = x[None] + (prev - x)[None] * P['rwkv_mix'][j][:, None, None, :]
    xr, xw, xk, xv, xa, xg = (mixed[s] for s in range(6))
    r, k, v = jnp.einsum('sbtd,sde->sbte', jnp.stack([xr, xk, xv]), P['rwkv_w_rkv'][j])
    w_log = -jax.nn.softplus(-(P['rwkv_w0'][j] + jnp.tanh(xw @ P['rwkv_w1'][j]) @ P['rwkv_w2'][j]).astype(jnp.float32)) - 0.5
    a = jax.nn.sigmoid(P['rwkv_a0'][j] + (xa @ P['rwkv_a1'][j]) @ P['rwkv_a2'][j])
    if v_first is None:
        v_first = v
    else:
        gate_v = jax.nn.sigmoid(P['rwkv_v0'][j - 1] + (xv @ P['rwkv_v1'][j - 1]) @ P['rwkv_v2'][j - 1])
        v = v + (v_first - v) * gate_v
    g = jax.nn.sigmoid(xg @ P['rwkv_g1'][j]) @ P['rwkv_g2'][j]
    kk = _l2norm(hd(k * P['rwkv_k_k'][j]))
    k = k * (1 + (a - 1) * P['rwkv_k_a'][j])
    decay = jnp.exp(-jnp.exp(w_log))
    o, S = _rwkv7_scan(hd(r), hd(decay), hd(k), hd(v), kk, hd(a), S0)
    mu = jnp.mean(o, -1, keepdims=True)
    var = jnp.mean(jnp.square(o - mu), -1, keepdims=True)
    o = ((o - mu) * lax.rsqrt(var + RWKV_LNX_EPS)).reshape(B, T, D) * P['rwkv_lnx_w'][j] + P['rwkv_lnx_b'][j]
    bonus = jnp.sum(hd(r) * hd(k) * P['rwkv_r_k'][j], -1, keepdims=True) * hd(v)
    out = ((o + bonus.reshape(B, T, D)) * g).astype(x.dtype) @ P['rwkv_w_o'][j]
    return out, x[:, -1], S.astype(S0.dtype), v_first


def _gdn_mixer(x, conv0, S0, P, j):
    B, T, _ = x.shape
    proj = x @ P['gdn_w_in'][j]
    qkv, conv_new = _causal_conv(proj[..., :GDN_CONV_DIM], conv0, P['gdn_conv_w'][j])
    qkv = jax.nn.silu(qkv)
    q = _l2norm(qkv[..., :GDN_QK].reshape(B, T, GDN_H, GDN_DK)) * GDN_DK ** -0.5
    k = _l2norm(qkv[..., GDN_QK:2 * GDN_QK].reshape(B, T, GDN_H, GDN_DK))
    v = qkv[..., 2 * GDN_QK:].reshape(B, T, GDN_H, GDN_DV).astype(jnp.float32)
    gate = proj[..., GDN_CONV_DIM:GDN_CONV_DIM + GDN_VD].reshape(B, T, GDN_H, GDN_DV).astype(jnp.float32)
    rest = proj[..., GDN_CONV_DIM + GDN_VD:].astype(jnp.float32)
    beta = jax.nn.sigmoid(rest[..., :GDN_H])
    g = -jnp.exp(P['gdn_a_log'][j].astype(jnp.float32)) * jax.nn.softplus(rest[..., GDN_H:] + P['gdn_dt_bias'][j])
    o, S = _gated_delta_chunked(q, k, v, g, beta, S0)
    o = _rmsnorm(o, P['gdn_norm_w'][j]) * jax.nn.silu(gate)
    out = o.reshape(B, T, GDN_VD).astype(x.dtype) @ P['gdn_w_o'][j]
    return out, conv_new, S.astype(S0.dtype)


def _hgrn2_mixer(x, S0, lb, P, j):
    B, T, _ = x.shape
    proj = (x @ P['hgrn_w_in'][j]).astype(jnp.float32)
    q = jax.nn.silu(proj[..., :HGRN_FK])
    f = lb + (1.0 - lb) * jax.nn.sigmoid(proj[..., HGRN_FK:2 * HGRN_FK])
    i_in = proj[..., 2 * HGRN_FK:2 * HGRN_FK + HGRN_VD]
    gate = proj[..., 2 * HGRN_FK + HGRN_VD:]
    hk = lambda t: t.reshape(B, T, HGRN_H, HGRN_DK)
    o, S = _gla_chunked(hk(q), hk(1.0 - f), i_in.reshape(B, T, HGRN_H, HGRN_DV), hk(jnp.log(f)), S0)
    o = _rmsnorm(o, P['hgrn_norm_w'][j]).reshape(B, T, HGRN_VD) * jax.nn.silu(gate)
    return o.astype(x.dtype) @ P['hgrn_w_o'][j], S.astype(S0.dtype)


def _conv_ffn(x, conv0, P, i):
    h, conv_new = _causal_conv(x @ P['ffn_w_up'][i], conv0, P['ffn_conv_w'][i])
    y = jax.nn.silu(h[..., D_FF:]) * h[..., :D_FF]
    return y @ P['ffn_w_down'][i], conv_new


def _trunk(x, shift0, wkv0, gconv0, gS0, hS0, fconv0, P):
    lb_soft = jax.nn.softmax(P['hgrn_lb'].astype(jnp.float32), axis=0)
    lb_layer = jnp.cumsum(lb_soft, axis=0) - lb_soft[0]
    v_first = None
    shift, wkv, gconv, gS, hS, fconv = [], [], [], [], [], []
    for i in range(DEPTH):
        kind, j = i % N_MIXERS, i // N_MIXERS
        h = _rmsnorm(x, P['norm_mix_pre'][i])
        if kind == 0:
            out, s_shift, s_wkv, v_first = _rwkv7_mixer(h, shift0[j], wkv0[j], v_first, P, j)
            shift.append(s_shift)
            wkv.append(s_wkv)
        elif kind == 1:
            out, c_new, s_new = _gdn_mixer(h, gconv0[j], gS0[j], P, j)
            gconv.append(c_new)
            gS.append(s_new)
        else:
            out, s_new = _hgrn2_mixer(h, hS0[j], lb_layer[i], P, j)
            hS.append(s_new)
        x = x + _rmsnorm(out, P['norm_mix_post'][i])
        out, c_new = _conv_ffn(_rmsnorm(x, P['norm_ffn_pre'][i]), fconv0[i], P, i)
        fconv.append(c_new)
        x = x + _rmsnorm(out, P['norm_ffn_post'][i])
    return x, (jnp.stack(shift), jnp.stack(wkv), jnp.stack(gconv), jnp.stack(gS), jnp.stack(hS), jnp.stack(fconv))


def setup_inputs(seed: int = 0) -> dict:
    keys = iter(jax.random.split(jax.random.key(seed), 64))

    def nrm(shape, scale):
        return jax.random.normal(next(keys), shape, jnp.float32) * scale

    def uni(shape, lo, hi):
        return jax.random.uniform(next(keys), shape, jnp.float32, lo, hi)

    def gain(shape):
        return 1.0 + nrm(shape, 0.05)

    D = D_MODEL
    dt = jnp.exp(uni((N_GDN, GDN_H), math.log(1e-3), math.log(1e-1)))
    return {
        'x_prompt': nrm((BATCH, SEQ, D), 1.0),
        'x_sample': nrm((DEC_BATCH, DEC_SEQ, D), 1.0),
        'state_rwkv_shift': nrm((N_RWKV, DEC_BATCH, D), 1.0),
        'state_rwkv_wkv': nrm((N_RWKV, DEC_BATCH, RWKV_H, RWKV_N, RWKV_N), 0.3),
        'state_gdn_conv': nrm((N_GDN, DEC_BATCH, GDN_CONV - 1, GDN_CONV_DIM), 1.0),
        'state_gdn_S': nrm((N_GDN, DEC_BATCH, GDN_H, GDN_DK, GDN_DV), 0.3),
        'state_hgrn_S': nrm((N_HGRN, DEC_BATCH, HGRN_H, HGRN_DK, HGRN_DV), 1.0),
        'state_ffn_conv': nrm((DEPTH, DEC_BATCH, FFN_CONV - 1, 2 * D_FF), 1.0),
        'norm_mix_pre': gain((DEPTH, D)),
        'norm_mix_post': gain((DEPTH, D)),
        'norm_ffn_pre': gain((DEPTH, D)),
        'norm_ffn_post': gain((DEPTH, D)),
        'rwkv_mix': uni((N_RWKV, 6, D), 0.0, 1.0),
        'rwkv_w_rkv': nrm((N_RWKV, 3, D, D), D ** -0.5),
        'rwkv_w0': -2.5 + nrm((N_RWKV, D), 0.5),
        'rwkv_w1': nrm((N_RWKV, D, DECAY_LORA), D ** -0.5),
        'rwkv_w2': nrm((N_RWKV, DECAY_LORA, D), 0.1 * DECAY_LORA ** -0.5),
        'rwkv_a0': nrm((N_RWKV, D), 0.1),
        'rwkv_a1': nrm((N_RWKV, D, AAA_LORA), D ** -0.5),
        'rwkv_a2': nrm((N_RWKV, AAA_LORA, D), 0.1 * AAA_LORA ** -0.5),
        'rwkv_v0': nrm((N_VRES, D), 0.1),
        'rwkv_v1': nrm((N_VRES, D, MV_LORA), D ** -0.5),
        'rwkv_v2': nrm((N_VRES, MV_LORA, D), 0.1 * MV_LORA ** -0.5),
        'rwkv_g1': nrm((N_RWKV, D, GATE_LORA), D ** -0.5),
        'rwkv_g2': nrm((N_RWKV, GATE_LORA, D), GATE_LORA ** -0.5),
        'rwkv_k_k': 0.85 + nrm((N_RWKV, D), 0.05),
        'rwkv_k_a': 1.0 + nrm((N_RWKV, D), 0.05),
        'rwkv_r_k': nrm((N_RWKV, RWKV_H, RWKV_N), 0.1),
        'rwkv_lnx_w': gain((N_RWKV, D)),
        'rwkv_lnx_b': nrm((N_RWKV, D), 0.02),
        'rwkv_w_o': nrm((N_RWKV, D, D), D ** -0.5),
        'gdn_w_in': nrm((N_GDN, D, GDN_IN), D ** -0.5),
        'gdn_conv_w': nrm((N_GDN, GDN_CONV, GDN_CONV_DIM), GDN_CONV ** -0.5),
        'gdn_a_log': jnp.log(uni((N_GDN, GDN_H), 1.0, 16.0)),
        'gdn_dt_bias': dt + jnp.log(-jnp.expm1(-dt)),
        'gdn_norm_w': gain((N_GDN, GDN_DV)),
        'gdn_w_o': nrm((N_GDN, GDN_VD, D), GDN_VD ** -0.5),
        'hgrn_w_in': nrm((N_HGRN, D, HGRN_IN), D ** -0.5),
        'hgrn_lb': nrm((DEPTH, HGRN_FK), 0.1),
        'hgrn_norm_w': gain((N_HGRN, HGRN_DV)),
        'hgrn_w_o': nrm((N_HGRN, HGRN_VD, D), HGRN_VD ** -0.5),
        'ffn_w_up': nrm((DEPTH, D, 2 * D_FF), D ** -0.5),
        'ffn_conv_w': nrm((DEPTH, FFN_CONV, 2 * D_FF), FFN_CONV ** -0.5),
        'ffn_w_down': nrm((DEPTH, D_FF, D), D_FF ** -0.5),
    }


def reference(x_prompt, x_sample, state_rwkv_shift, state_rwkv_wkv, state_gdn_conv, state_gdn_S,
              state_hgrn_S, state_ffn_conv, norm_mix_pre, norm_mix_post, norm_ffn_pre, norm_ffn_post,
              rwkv_mix, rwkv_w_rkv, rwkv_w0, rwkv_w1, rwkv_w2, rwkv_a0, rwkv_a1, rwkv_a2,
              rwkv_v0, rwkv_v1, rwkv_v2, rwkv_g1, rwkv_g2, rwkv_k_k, rwkv_k_a, rwkv_r_k,
              rwkv_lnx_w, rwkv_lnx_b, rwkv_w_o, gdn_w_in, gdn_conv_w, gdn_a_log, gdn_dt_bias,
              gdn_norm_w, gdn_w_o, hgrn_w_in, hgrn_lb, hgrn_norm_w, hgrn_w_o,
              ffn_w_up, ffn_conv_w, ffn_w_down):
    P = dict(norm_mix_pre=norm_mix_pre, norm_mix_post=norm_mix_post, norm_ffn_pre=norm_ffn_pre,
             norm_ffn_post=norm_ffn_post, rwkv_mix=rwkv_mix, rwkv_w_rkv=rwkv_w_rkv, rwkv_w0=rwkv_w0,
             rwkv_w1=rwkv_w1, rwkv_w2=rwkv_w2, rwkv_a0=rwkv_a0, rwkv_a1=rwkv_a1, rwkv_a2=rwkv_a2,
             rwkv_v0=rwkv_v0, rwkv_v1=rwkv_v1, rwkv_v2=rwkv_v2, rwkv_g1=rwkv_g1, rwkv_g2=rwkv_g2,
             rwkv_k_k=rwkv_k_k, rwkv_k_a=rwkv_k_a, rwkv_r_k=rwkv_r_k, rwkv_lnx_w=rwkv_lnx_w,
             rwkv_lnx_b=rwkv_lnx_b, rwkv_w_o=rwkv_w_o, gdn_w_in=gdn_w_in, gdn_conv_w=gdn_conv_w,
             gdn_a_log=gdn_a_log, gdn_dt_bias=gdn_dt_bias, gdn_norm_w=gdn_norm_w, gdn_w_o=gdn_w_o,
             hgrn_w_in=hgrn_w_in, hgrn_lb=hgrn_lb, hgrn_norm_w=hgrn_norm_w, hgrn_w_o=hgrn_w_o,
             ffn_w_up=ffn_w_up, ffn_conv_w=ffn_conv_w, ffn_w_down=ffn_w_down)
    Bp = x_prompt.shape[0]
    zero_like = lambda s: jnp.zeros((s.shape[0], Bp) + s.shape[2:], s.dtype)
    y_prompt, (p_shift, p_wkv, p_gconv, p_gS, p_hS, p_fconv) = _trunk(
        x_prompt, zero_like(state_rwkv_shift), zero_like(state_rwkv_wkv), zero_like(state_gdn_conv),
        zero_like(state_gdn_S), zero_like(state_hgrn_S), zero_like(state_ffn_conv), P)
    y_sample, (s_shift, s_wkv, s_gconv, s_gS, s_hS, s_fconv) = _trunk(
        x_sample, state_rwkv_shift, state_rwkv_wkv, state_gdn_conv, state_gdn_S, state_hgrn_S,
        state_ffn_conv, P)
    return (y_prompt, y_sample, p_shift, s_shift, p_wkv, s_wkv, p_gconv, s_gconv,
            p_gS, s_gS, p_hS, s_hS, p_fconv, s_fconv)
```

```python
import functools
import math

import jax
import jax.numpy as jnp
from jax import lax
from jax.experimental import pallas as pl
from jax.experimental.pallas import tpu as pltpu

F32 = jnp.float32
BF16 = jnp.bfloat16

NORM_EPS = 1e-6
L2_EPS = 1e-6
RWKV_LNX_EPS = 64e-5
RWKV_N = 64
LANES = 128
SUBLANES = 8
BF16_ROWS = 16
VMEM_LIMIT = 56 * 1024 * 1024
ROW_TILE = 256
RWKV_CHUNK = 64
GDN_CHUNK = 64
HGRN_CHUNK = 32


def _dg(a, b, ca, cb):
    return lax.dot_general(a, b, (((ca,), (cb,)), ((), ())), preferred_element_type=F32)


def _mm(a, b):
    return _dg(a.astype(BF16), b.astype(BF16), 1, 0)


def _mm_nt(a, b):
    return _dg(a.astype(BF16), b.astype(BF16), 1, 1)


def _mm_tn(a, b):
    return _dg(a.astype(BF16), b.astype(BF16), 0, 0)


def _split2(x):
    hi = x.astype(BF16)
    lo = (x - hi.astype(F32)).astype(BF16)
    return hi, lo


def _split3(x):
    hi = x.astype(BF16)
    r1 = x - hi.astype(F32)
    mid = r1.astype(BF16)
    lo = (r1 - mid.astype(F32)).astype(BF16)
    return hi, mid, lo


def _mm_hp(a, b):
    ah, al = _split2(a)
    bh, bl = _split2(b)
    return _dg(ah, bh, 1, 0) + (_dg(ah, bl, 1, 0) + _dg(al, bh, 1, 0))


def _mm_xl(m, x):
    h, mi, lo = _split3(x)
    m = m.astype(BF16)
    return _dg(m, h, 1, 0) + (_dg(m, mi, 1, 0) + _dg(m, lo, 1, 0))


def _mm_xr(x, m):
    h, mi, lo = _split3(x)
    m = m.astype(BF16)
    return _dg(h, m, 1, 0) + (_dg(mi, m, 1, 0) + _dg(lo, m, 1, 0))


def _iota2(n, m):
    return (lax.broadcasted_iota(jnp.int32, (n, m), 0), lax.broadcasted_iota(jnp.int32, (n, m), 1))


def _sigmoid(x):
    return 1.0 / (1.0 + jnp.exp(-x))


def _silu(x):
    return x * _sigmoid(x)


def _softplus(x):
    return jnp.maximum(x, 0.0) + jnp.log(1.0 + jnp.exp(-jnp.abs(x)))


def _rmsnorm(x, w):
    return x * lax.rsqrt(jnp.mean(x * x, axis=-1, keepdims=True) + NORM_EPS) * w


def _unit_lower_inv(L):
    n = L.shape[0]
    ri, ci = _iota2(n, n)
    X = (ri == ci).astype(F32) - L
    P = L
    m = 2
    while m < n:
        P = _mm_hp(P, P)
        X = X + _mm_hp(X, P)
        m *= 2
    return X


def _cumsum_rows(x):
    C = x.shape[0]
    ri, ci = _iota2(C, C)
    return _mm_xl((ri >= ci).astype(F32), x)


def _head_sum(x, e):
    return _mm_xr(x, e)


def _rwkv_chunk(r, lw, k, v, kk, bb, S):
    C = r.shape[0]
    ri, ci = _iota2(C, C)
    tril = ri >= ci
    stril = ri > ci
    lane = lax.broadcasted_iota(jnp.int32, (1, LANES), 1)
    gc = _cumsum_rows(lw)
    e_pos = jnp.exp(gc)
    e_neg = jnp.exp(-gc)
    at = -kk * jnp.exp(gc - lw)
    rt = r * e_pos
    bt = bb * e_neg
    kt = k * e_neg
    gl = gc[C - 1:C, :]
    e_last = jnp.exp(gl - gc)
    aS = _mm_nt(at, S)
    o = _mm_nt(rt, S)
    U = jnp.zeros_like(r)
    for hd in range(2):
        m = ((lane >= RWKV_N * hd) & (lane < RWKV_N * (hd + 1))).astype(F32)
        a_h = at * m
        r_h = rt * m
        A_ab = jnp.where(stril, _mm_nt(a_h, bt), 0.0)
        A_ak = jnp.where(stril, _mm_nt(a_h, kt), 0.0)
        A_rb = jnp.where(tril, _mm_nt(r_h, bt), 0.0)
        A_rk = jnp.where(tril, _mm_nt(r_h, kt), 0.0)
        Tinv = _unit_lower_inv(-A_ab)
        U_h = _mm_hp(Tinv, aS + _mm(A_ak, v))
        U = U + U_h * m
        o = o + (_mm(A_rb, U_h) + _mm(A_rk, v)) * m
    r2, c2 = _iota2(LANES, LANES)
    blk = ((r2 >= RWKV_N) == (c2 >= RWKV_N)).astype(F32)
    S_new = S * jnp.exp(gl) + blk * (_mm_tn(U, bb * e_last) + _mm_tn(v, k * e_last))
    return o, S_new


def _gdn_chunk(q, k, v, g_row, beta_row, S):
    C = q.shape[0]
    ri, ci = _iota2(C, C)
    tril = ri >= ci
    stril = ri > ci
    eye = ri == ci
    g_b = jnp.broadcast_to(g_row, (C, C))
    g_col = jnp.sum(jnp.where(eye, g_b, 0.0), axis=1, keepdims=True)
    beta_col = jnp.sum(jnp.where(eye, jnp.broadcast_to(beta_row, (C, C)), 0.0), axis=1, keepdims=True)
    gc_col = jnp.sum(jnp.where(tril, g_b, 0.0), axis=1, keepdims=True)
    gc_row = jnp.sum(jnp.where(ri <= ci, jnp.broadcast_to(g_col, (C, C)), 0.0), axis=0, keepdims=True)
    decay = jnp.where(tril, jnp.exp(jnp.minimum(gc_col - gc_row, 0.0)), 0.0)
    kb = k * beta_col
    L = jnp.where(stril, _mm_nt(kb, k) * decay, 0.0)
    Tinv = _unit_lower_inv(L)
    u = _mm_hp(Tinv, v * beta_col)
    wk = _mm_hp(Tinv, kb * jnp.exp(gc_col))
    qk = _mm_nt(q, k) * decay
    gl = gc_col[C - 1:C, :]
    v_new = u - _mm(wk, S)
    o = _mm(q * jnp.exp(gc_col), S) + _mm(qk, v_new)
    S_new = S * jnp.exp(gl) + _mm_tn(k * jnp.exp(gl - gc_col), v_new)
    return o, S_new


def _gla_chunk(q, k, v, lf, ST):
    C = q.shape[0]
    ri, ci = _iota2(C, C)
    gc = _cumsum_rows(lf)
    row = lax.broadcasted_iota(jnp.int32, (C, 1), 0)

    def body(i, AT):
        sel = (row == i).astype(F32)
        gi = jnp.sum(gc * sel, axis=0, keepdims=True)
        qi = jnp.sum(q * sel, axis=0, keepdims=True)
        E = jnp.exp(jnp.minimum(gi - gc, 0.0))
        s = jnp.sum(qi * k * E, axis=1, keepdims=True)
        return jnp.where(ci == i, s, AT)

    AT = lax.fori_loop(0, C, body, jnp.zeros((C, C), F32))
    AT = jnp.where(ri <= ci, AT, 0.0)
    gl = gc[C - 1:C, :]
    o = _mm_nt(q * jnp.exp(gc), ST) + _mm_tn(AT, v)
    ST_new = ST * jnp.exp(gl) + _mm_tn(v, k * jnp.exp(gl - gc))
    return o, ST_new


def _shifted(ext, hp, s, tm, back):
    return ext[hp - back * s:hp - back * s + tm, :]


def _rwkv_pre_kernel(has_vres, s, *refs):
    if has_vres:
        (x_ref, vf_ref, sh0_ref, nw_ref, mix_ref, wrkv_ref, w0_ref, w1_ref, w2_ref, a0_ref, a1_ref, a2_ref,
         g1_ref, g2_ref, kkw_ref, ka_ref, e_ref, v0_ref, v1_ref, v2_ref,
         r_out, lw_out, k_out, v_out, kk_out, bb_out, g_out, sh_out, ext) = refs
    else:
        (x_ref, sh0_ref, nw_ref, mix_ref, wrkv_ref, w0_ref, w1_ref, w2_ref, a0_ref, a1_ref, a2_ref,
         g1_ref, g2_ref, kkw_ref, ka_ref, e_ref,
         r_out, lw_out, k_out, v_out, kk_out, bb_out, g_out, sh_out, ext) = refs
    t = pl.program_id(1)
    tm, D = x_ref.shape
    hp = ext.shape[0] - tm
    h = _rmsnorm(x_ref[...], nw_ref[...])

    @pl.when(t == 0)
    def _():
        ext[hp - s:hp, :] = sh0_ref[...]

    ext[hp:, :] = h
    d = _shifted(ext, hp, s, tm, 1) - h
    ext[hp - s:hp, :] = h[tm - s:, :]

    @pl.when(t == pl.num_programs(1) - 1)
    def _():
        sh_out[...] = h[tm - s:, :]

    mixed = lambda i: h + d * mix_ref[i:i + 1, :]
    r = _mm(mixed(0), wrkv_ref[0])
    z = w0_ref[...] + _mm(jnp.tanh(_mm(mixed(1), w1_ref[...])), w2_ref[...])
    lw_out[...] = -math.exp(-0.5) * _sigmoid(z)
    k = _mm(mixed(2), wrkv_ref[1])
    xv = mixed(3)
    v = _mm(xv, wrkv_ref[2])
    a = _sigmoid(a0_ref[...] + _mm(_mm(mixed(4), a1_ref[...]), a2_ref[...]))
    if has_vres:
        gate_v = _sigmoid(v0_ref[...] + _mm(_mm(xv, v1_ref[...]), v2_ref[...]))
        v = v + (vf_ref[...] - v) * gate_v
    g_out[...] = _mm(_sigmoid(_mm(mixed(5), g1_ref[...])), g2_ref[...])
    r_out[...] = r
    v_out[...] = v
    kkraw = k * kkw_ref[...]
    e = e_ref[...]
    for c in range(D // LANES):
        sl = slice(c * LANES, (c + 1) * LANES)
        kc = kkraw[:, sl]
        kkn = kc * lax.rsqrt(_head_sum(kc * kc, e) + L2_EPS)
        kk_out[:, sl] = kkn
        bb_out[:, sl] = kkn * a[:, sl]
    k_out[...] = k * (1.0 + (a - 1.0) * ka_ref[...])


def _gdn_pre_kernel(s, nheads, x_ref, c0_ref, nw_ref, wqkv_ref, wgate_ref, wbg_ref, cw_ref, alog_ref, dtb_ref,
                    q_out, k_out, v_out, gate_out, bg_out, cnew_out, ext):
    t = pl.program_id(1)
    tm, D = x_ref.shape
    hp = ext.shape[0] - tm
    nh = cw_ref.shape[0] - 1
    h = _rmsnorm(x_ref[...], nw_ref[...])

    @pl.when(t == 0)
    def _():
        ext[hp - nh * s:hp, :] = c0_ref[...]

    ext[hp:, :] = _mm(h, wqkv_ref[...])
    y = _shifted(ext, hp, s, tm, 0) * cw_ref[nh:nh + 1, :]
    for j in range(nh):
        y = y + _shifted(ext, hp, s, tm, nh - j) * cw_ref[j:j + 1, :]
    hist = ext[hp + tm - nh * s:hp + tm, :]

    @pl.when(t == pl.num_programs(1) - 1)
    def _():
        cnew_out[...] = hist

    ext[hp - nh * s:hp, :] = hist
    y = _silu(y)
    qk_w = D
    for c in range(qk_w // LANES):
        sl = slice(c * LANES, (c + 1) * LANES)
        qc = y[:, sl]
        q_out[:, sl] = qc * (lax.rsqrt(jnp.sum(qc * qc, axis=-1, keepdims=True) + L2_EPS) * LANES ** -0.5)
        sl2 = slice(qk_w + c * LANES, qk_w + (c + 1) * LANES)
        kc = y[:, sl2]
        k_out[:, sl] = kc * lax.rsqrt(jnp.sum(kc * kc, axis=-1, keepdims=True) + L2_EPS)
    v_out[...] = y[:, 2 * qk_w:]
    gate_out[...] = _mm(h, wgate_ref[...])
    rest = _mm(h, wbg_ref[...])
    lane = lax.broadcasted_iota(jnp.int32, rest.shape, 1)
    bg_out[...] = jnp.where(lane < nheads, _sigmoid(rest),
                            -jnp.exp(alog_ref[...]) * _softplus(rest + dtb_ref[...]))


def _hgrn_pre_kernel(layer, x_ref, nw_ref, win_ref, lb_ref, q_out, k_out, lf_out, v_out, gate_out):
    D = x_ref.shape[1]
    h = _rmsnorm(x_ref[...], nw_ref[...])
    lbp = lb_ref[...]
    ex = jnp.exp(lbp - jnp.max(lbp, axis=0, keepdims=True))
    soft = ex / jnp.sum(ex, axis=0, keepdims=True)
    row = lax.broadcasted_iota(jnp.int32, soft.shape, 0)
    lb = jnp.sum(jnp.where((row >= 1) & (row <= layer), soft, 0.0), axis=0, keepdims=True)
    p = _mm(h, win_ref[...])
    q_out[...] = _silu(p[:, :D])
    f = lb + (1.0 - lb) * _sigmoid(p[:, D:2 * D])
    k_out[...] = 1.0 - f
    lf_out[...] = jnp.log(f)
    v_out[...] = p[:, 2 * D:3 * D]
    gate_out[...] = p[:, 3 * D:]


def _out_proj_kernel(y_ref, x_ref, wo_ref, pw_ref, x_out):
    x_out[...] = x_ref[...] + _rmsnorm(_mm(y_ref[...], wo_ref[...]), pw_ref[...])


def _ffn_kernel(s, x_ref, c0_ref, nw_ref, wup_ref, cw_ref, wdn_ref, pw_ref, x_out, cnew_out, ext):
    t = pl.program_id(1)
    tm, D = x_ref.shape
    hp = ext.shape[0] - tm
    nh = cw_ref.shape[0] - 1
    dff = wdn_ref.shape[0]
    x = x_ref[...]
    h = _rmsnorm(x, nw_ref[...])

    @pl.when(t == 0)
    def _():
        ext[hp - nh * s:hp, :] = c0_ref[...]

    ext[hp:, :] = _mm(h, wup_ref[...])
    y = _shifted(ext, hp, s, tm, 0) * cw_ref[nh:nh + 1, :]
    for j in range(nh):
        y = y + _shifted(ext, hp, s, tm, nh - j) * cw_ref[j:j + 1, :]
    hist = ext[hp + tm - nh * s:hp + tm, :]

    @pl.when(t == pl.num_programs(1) - 1)
    def _():
        cnew_out[...] = hist

    ext[hp - nh * s:hp, :] = hist
    act = _silu(y[:, dff:]) * y[:, :dff]
    x_out[...] = x + _rmsnorm(_mm(act, wdn_ref[...]), pw_ref[...])


def _const_spec(a):
    nd = a.ndim
    return pl.BlockSpec(a.shape, lambda g, t: (0,) * nd, pipeline_mode=pl.Buffered(1))


def _row_call(body, name, tm, tiled_ins, group_ins, const_ins, tiled_out_widths, group_out_shapes, scratch):
    G, R, _ = tiled_ins[0].shape
    assert R % tm == 0
    tile_spec = lambda c: pl.BlockSpec((None, tm, c), lambda g, t: (g, t, 0))
    group_spec = lambda n, c: pl.BlockSpec((None, n, c), lambda g, t: (g, 0, 0))
    group_in_spec = lambda n, c: pl.BlockSpec((None, n, c), lambda g, t: (g, 0, 0), pipeline_mode=pl.Buffered(1))
    in_specs = ([tile_spec(a.shape[2]) for a in tiled_ins] + [group_in_spec(*a.shape[1:]) for a in group_ins]
                + [_const_spec(a) for a in const_ins])
    out_specs = [tile_spec(c) for c in tiled_out_widths] + [group_spec(n, c) for n, c in group_out_shapes]
    out_shape = ([jax.ShapeDtypeStruct((G, R, c), F32) for c in tiled_out_widths]
                 + [jax.ShapeDtypeStruct((G, n, c), F32) for n, c in group_out_shapes])
    return pl.pallas_call(
        body, name=name, grid=(G, R // tm), in_specs=in_specs, out_specs=out_specs, out_shape=out_shape,
        scratch_shapes=scratch,
        compiler_params=pltpu.CompilerParams(dimension_semantics=("parallel", "arbitrary"),
                                             vmem_limit_bytes=VMEM_LIMIT),
    )(*tiled_ins, *group_ins, *const_ins)


def _hist_pad(n):
    return -(-n // SUBLANES) * SUBLANES


def _rwkv_scan_kernel(r_ref, lw_ref, k_ref, v_ref, kk_ref, bb_ref, g_ref, rk_ref, lnw_ref, lnb_ref, e_ref,
                      s0_ref, y_out, s_out, S):
    c = pl.program_id(2)
    N = RWKV_N
    r2, c2 = _iota2(LANES, LANES)
    blk = ((r2 >= N) == (c2 >= N)).astype(F32)

    @pl.when(c == 0)
    def _():
        ri, ci = _iota2(N, LANES)
        dup = ((ci == ri) | (ci == ri + N)).astype(F32)
        S[...] = _mm_xr(s0_ref[...].reshape(2 * N, N), dup) * blk

    r = r_ref[...]
    k = k_ref[...]
    v = v_ref[...]
    o, S_new = _rwkv_chunk(r, lw_ref[...], k, v, kk_ref[...], bb_ref[...], S[...])
    S[...] = S_new
    e = e_ref[...]
    mu = _head_sum(o, e) * (1.0 / N)
    d = o - mu
    var = _head_sum(d * d, e) * (1.0 / N)
    on = d * lax.rsqrt(var + RWKV_LNX_EPS) * lnw_ref[...] + lnb_ref[...]
    bonus = _head_sum(r * k * rk_ref[...], e) * v
    y_out[...] = (on + bonus) * g_ref[...]

    @pl.when(c == pl.num_programs(2) - 1)
    def _():
        ri, ci = _iota2(LANES, N)
        fold = ((ri == ci) | (ri == ci + N)).astype(F32)
        s_out[...] = _mm_xr(S_new, fold).reshape(2, N, N)


def _gdn_scan_kernel(q_ref, k_ref, v_ref, gate_ref, g_ref, beta_ref, nw_ref, s0_ref, y_out, s_out, S):
    c = pl.program_id(2)

    @pl.when(c == 0)
    def _():
        S[...] = s0_ref[...]

    o, S_new = _gdn_chunk(q_ref[...], k_ref[...], v_ref[...], g_ref[pl.ds(c, 1), :], beta_ref[pl.ds(c, 1), :],
                          S[...])
    S[...] = S_new
    y_out[...] = _rmsnorm(o, nw_ref[...]) * _silu(gate_ref[...])

    @pl.when(c == pl.num_programs(2) - 1)
    def _():
        s_out[...] = S_new


def _gla_scan_kernel(q_ref, k_ref, lf_ref, v_ref, gate_ref, nw_ref, s0_ref, y_out, s_out, ST):
    c = pl.program_id(2)

    @pl.when(c == 0)
    def _():
        ST[...] = s0_ref[...].T

    o, ST_new = _gla_chunk(q_ref[...], k_ref[...], v_ref[...], lf_ref[...], ST[...])
    ST[...] = ST_new
    y_out[...] = _rmsnorm(o, nw_ref[...]) * _silu(gate_ref[...])

    @pl.when(c == pl.num_programs(2) - 1)
    def _():
        s_out[...] = ST_new.T


def _scan_call(body, name, C, tiles, extra_ins, extra_specs, s0, s0_block, nheads):
    B, T, D = tiles[0].shape
    assert T % C == 0
    tile_spec = pl.BlockSpec((None, C, LANES), lambda b, h, c: (b, c, h))
    s_spec = pl.BlockSpec((None,) + tuple(s0_block), lambda b, h, c: (b, h, 0, 0))
    return pl.pallas_call(
        body, name=name, grid=(B, nheads, T // C),
        in_specs=[tile_spec] * len(tiles) + list(extra_specs) + [s_spec],
        out_specs=[tile_spec, s_spec],
        out_shape=[jax.ShapeDtypeStruct((B, T, D), F32), jax.ShapeDtypeStruct(s0.shape, F32)],
        scratch_shapes=[pltpu.VMEM((LANES, LANES), F32)],
        compiler_params=pltpu.CompilerParams(dimension_semantics=("parallel", "parallel", "arbitrary"),
                                             vmem_limit_bytes=VMEM_LIMIT),
    )(*tiles, *extra_ins, s0)


class _Group:
    def __init__(self, B, T, time_major):
        self.B, self.T, self.tm_major = B, T, time_major
        if time_major:
            self.G, self.R, self.s = 1, B * T, B
        else:
            self.G, self.R, self.s = B, T, 1
        self.tile = min(ROW_TILE, self.R)
        self.ffn_tile = self.s if time_major else self.tile

    def to_rows(self, x):
        if self.tm_major:
            return jnp.swapaxes(x, 0, 1).reshape(1, self.R, x.shape[-1])
        return x

    def hist_to_rows(self, h):
        if self.tm_major:
            return jnp.swapaxes(h, 0, 1).reshape(1, -1, h.shape[-1])
        return h

    def hist_from_rows(self, h, n):
        if self.tm_major:
            return jnp.swapaxes(h.reshape(n, self.B, h.shape[-1]), 0, 1)
        return h

    def to_scan(self, a, tpad):
        if self.tm_major:
            a = jnp.swapaxes(a.reshape(self.T, self.B, a.shape[-1]), 0, 1)
        if tpad != self.T:
            a = jnp.pad(a, ((0, 0), (0, tpad - self.T), (0, 0)))
        return a

    def from_scan(self, a):
        a = a[:, :self.T]
        return self.to_rows(a)


def _scan_len(T, chunk):
    if T % chunk == 0:
        return chunk, T
    tpad = -(-T // BF16_ROWS) * BF16_ROWS
    assert tpad <= chunk
    return tpad, tpad


def _rwkv_layer(grp, x, shift0, S0, v_first, P, j):
    D = x.shape[-1]
    s, tm = grp.s, grp.tile
    has_vres = v_first is not None
    r2, c2 = _iota2(LANES, LANES)
    e = ((r2 // RWKV_N) == (c2 // RWKV_N)).astype(BF16)
    row = lambda a: a.reshape(1, -1)
    tiled = [x] + ([v_first] if has_vres else [])
    consts = [row(P['norm_mix_pre_i']), P['rwkv_mix'][j], P['rwkv_w_rkv'][j].astype(BF16),
              row(P['rwkv_w0'][j]), P['rwkv_w1'][j].astype(BF16), P['rwkv_w2'][j].astype(BF16),
              row(P['rwkv_a0'][j]), P['rwkv_a1'][j].astype(BF16), P['rwkv_a2'][j].astype(BF16),
              P['rwkv_g1'][j].astype(BF16), P['rwkv_g2'][j].astype(BF16),
              row(P['rwkv_k_k'][j]), row(P['rwkv_k_a'][j]), e]
    if has_vres:
        consts += [row(P['rwkv_v0'][j - 1]), P['rwkv_v1'][j - 1].astype(BF16), P['rwkv_v2'][j - 1].astype(BF16)]
    r, lw, k, v, kk, bb, g, shift = _row_call(
        functools.partial(_rwkv_pre_kernel, has_vres, s), f"rwkv_pre_{j}", tm, tiled, [shift0], consts,
        [D] * 7, [(s, D)], [pltpu.VMEM((_hist_pad(s) + tm, D), F32)])
    if not has_vres:
        v_first = v
    C, tpad = _scan_len(grp.T, RWKV_CHUNK)
    tiles = [grp.to_scan(a, tpad) for a in (r, lw, k, v, kk, bb, g)]
    vec_spec = pl.BlockSpec((1, LANES), lambda b, h, c: (0, h))
    e_spec = pl.BlockSpec((LANES, LANES), lambda b, h, c: (0, 0))
    y, S = _scan_call(_rwkv_scan_kernel, f"rwkv_scan_{j}", C, tiles,
                      [row(P['rwkv_r_k'][j]), row(P['rwkv_lnx_w'][j]), row(P['rwkv_lnx_b'][j]), e],
                      [vec_spec] * 3 + [e_spec], S0, (2, RWKV_N, RWKV_N), D // LANES)
    return grp.from_scan(y), P['rwkv_w_o'][j], shift, S, v_first


def _gdn_layer(grp, x, conv0, S0, P, j):
    D = x.shape[-1]
    s, tm = grp.s, grp.tile
    w_in = P['gdn_w_in'][j]
    cw = P['gdn_conv_w'][j]
    nh, cdim = cw.shape[0] - 1, cw.shape[1]
    H = S0.shape[1]
    assert grp.T >= nh
    row = lambda a: a.reshape(1, -1)
    lane_pad = lambda a: jnp.pad(a, ((0, 0), (0, LANES - a.shape[1])))
    zeros = jnp.zeros((1, H), F32)
    consts = [row(P['norm_mix_pre_i']), w_in[:, :cdim].astype(BF16), w_in[:, cdim:cdim + D].astype(BF16),
              lane_pad(w_in[:, cdim + D:]).astype(BF16), cw,
              lane_pad(jnp.concatenate([zeros, row(P['gdn_a_log'][j])], axis=1)),
              lane_pad(jnp.concatenate([zeros, row(P['gdn_dt_bias'][j])], axis=1))]
    q, k, v, gate, bg, conv_new = _row_call(
        functools.partial(_gdn_pre_kernel, s, H), f"gdn_pre_{j}", tm, [x], [conv0], consts,
        [D, D, D, D, LANES], [(nh * s, cdim)], [pltpu.VMEM((_hist_pad(nh * s) + tm, cdim), F32)])
    C, tpad = _scan_len(grp.T, GDN_CHUNK)
    tiles = [grp.to_scan(a, tpad) for a in (q, k, v, gate)]
    bg = jnp.swapaxes(grp.to_scan(bg[:, :, :2 * H], tpad), 1, 2).reshape(grp.B, 2 * H, tpad // C, C)
    row_spec = pl.BlockSpec((None, None, tpad // C, C), lambda b, h, c: (b, h, 0, 0))
    vec_spec = pl.BlockSpec((1, LANES), lambda b, h, c: (0, 0))
    y, S = _scan_call(_gdn_scan_kernel, f"gdn_scan_{j}", C, tiles,
                      [bg[:, H:], bg[:, :H], row(P['gdn_norm_w'][j])], [row_spec, row_spec, vec_spec],
                      S0, (None, LANES, LANES), H)
    return grp.from_scan(y), P['gdn_w_o'][j], conv_new, S


def _hgrn_layer(grp, x, S0, P, i, j):
    D = x.shape[-1]
    tm = grp.tile
    H = S0.shape[1]
    row = lambda a: a.reshape(1, -1)
    consts = [row(P['norm_mix_pre_i']), P['hgrn_w_in'][j].astype(BF16), P['hgrn_lb']]
    q, k, lf, v, gate = _row_call(functools.partial(_hgrn_pre_kernel, i), f"hgrn_pre_{j}", tm, [x], [], consts,
                                  [D] * 5, [], [])
    C, tpad = _scan_len(grp.T, HGRN_CHUNK)
    tiles = [grp.to_scan(a, tpad) for a in (q, k, lf, v, gate)]
    vec_spec = pl.BlockSpec((1, LANES), lambda b, h, c: (0, 0))
    y, S = _scan_call(_gla_scan_kernel, f"hgrn_scan_{j}", C, tiles, [row(P['hgrn_norm_w'][j])], [vec_spec],
                      S0, (None, LANES, LANES), H)
    return grp.from_scan(y), P['hgrn_w_o'][j], S


def _trunk(grp, x, shift0, wkv0, gconv0, gS0, hS0, fconv0, P):
    D = x.shape[-1]
    depth = P['norm_mix_pre'].shape[0]
    row = lambda a: a.reshape(1, -1)
    x = grp.to_rows(x)
    v_first = None
    shift, wkv, gconv, gS, hS, fconv = [], [], [], [], [], []
    for i in range(depth):
        kind, j = i % 3, i // 3
        P = dict(P, norm_mix_pre_i=P['norm_mix_pre'][i])
        if kind == 0:
            y, w_o, s_shift, s_wkv, v_first = _rwkv_layer(grp, x, grp.hist_to_rows(shift0[j][:, None]), wkv0[j],
                                                         v_first, P, j)
            shift.append(grp.hist_from_rows(s_shift, 1)[:, 0])
            wkv.append(s_wkv)
        elif kind == 1:
            y, w_o, c_new, s_new = _gdn_layer(grp, x, grp.hist_to_rows(gconv0[j]), gS0[j], P, j)
            gconv.append(grp.hist_from_rows(c_new, gconv0.shape[2]))
            gS.append(s_new)
        else:
            y, w_o, s_new = _hgrn_layer(grp, x, hS0[j], P, i, j)
            hS.append(s_new)
        (x,) = _row_call(_out_proj_kernel, f"out_proj_{i}", grp.tile, [y, x], [],
                         [w_o.astype(BF16), row(P['norm_mix_post'][i])], [D], [], [])
        nh = P['ffn_conv_w'].shape[1] - 1
        dff2 = P['ffn_w_up'].shape[2]
        x, c_new = _row_call(
            functools.partial(_ffn_kernel, grp.s), f"ffn_{i}", grp.ffn_tile, [x], [grp.hist_to_rows(fconv0[i])],
            [row(P['norm_ffn_pre'][i]), P['ffn_w_up'][i].astype(BF16), P['ffn_conv_w'][i],
             P['ffn_w_down'][i].astype(BF16), row(P['norm_ffn_post'][i])],
            [D], [(nh * grp.s, dff2)], [pltpu.VMEM((_hist_pad(nh * grp.s) + grp.ffn_tile, dff2), F32)])
        fconv.append(grp.hist_from_rows(c_new, nh))
    y = x.reshape(grp.T, grp.B, D).swapaxes(0, 1) if grp.tm_major else x
    return y, (jnp.stack(shift), jnp.stack(wkv), jnp.stack(gconv), jnp.stack(gS), jnp.stack(hS), jnp.stack(fconv))


def kernel(x_prompt, x_sample, state_rwkv_shift, state_rwkv_wkv, state_gdn_conv, state_gdn_S, state_hgrn_S, state_ffn_conv, norm_mix_pre, norm_mix_post, norm_ffn_pre, norm_ffn_post, rwkv_mix, rwkv_w_rkv, rwkv_w0, rwkv_w1, rwkv_w2, rwkv_a0, rwkv_a1, rwkv_a2, rwkv_v0, rwkv_v1, rwkv_v2, rwkv_g1, rwkv_g2, rwkv_k_k, rwkv_k_a, rwkv_r_k, rwkv_lnx_w, rwkv_lnx_b, rwkv_w_o, gdn_w_in, gdn_conv_w, gdn_a_log, gdn_dt_bias, gdn_norm_w, gdn_w_o, hgrn_w_in, hgrn_lb, hgrn_norm_w, hgrn_w_o, ffn_w_up, ffn_conv_w, ffn_w_down):
    P = dict(norm_mix_pre=norm_mix_pre, norm_mix_post=norm_mix_post, norm_ffn_pre=norm_ffn_pre,
             norm_ffn_post=norm_ffn_post, rwkv_mix=rwkv_mix, rwkv_w_rkv=rwkv_w_rkv, rwkv_w0=rwkv_w0,
             rwkv_w1=rwkv_w1, rwkv_w2=rwkv_w2, rwkv_a0=rwkv_a0, rwkv_a1=rwkv_a1, rwkv_a2=rwkv_a2,
             rwkv_v0=rwkv_v0, rwkv_v1=rwkv_v1, rwkv_v2=rwkv_v2, rwkv_g1=rwkv_g1, rwkv_g2=rwkv_g2,
             rwkv_k_k=rwkv_k_k, rwkv_k_a=rwkv_k_a, rwkv_r_k=rwkv_r_k, rwkv_lnx_w=rwkv_lnx_w,
             rwkv_lnx_b=rwkv_lnx_b, rwkv_w_o=rwkv_w_o, gdn_w_in=gdn_w_in, gdn_conv_w=gdn_conv_w,
             gdn_a_log=gdn_a_log, gdn_dt_bias=gdn_dt_bias, gdn_norm_w=gdn_norm_w, gdn_w_o=gdn_w_o,
             hgrn_w_in=hgrn_w_in, hgrn_lb=hgrn_lb, hgrn_norm_w=hgrn_norm_w, hgrn_w_o=hgrn_w_o,
             ffn_w_up=ffn_w_up, ffn_conv_w=ffn_conv_w, ffn_w_down=ffn_w_down)
    Bp, Tp, _ = x_prompt.shape
    Bs, Ts, _ = x_sample.shape
    zero_like = lambda st: jnp.zeros((st.shape[0], Bp) + st.shape[2:], st.dtype)
    y_p, (p_shift, p_wkv, p_gconv, p_gS, p_hS, p_fconv) = _trunk(
        _Group(Bp, Tp, False), x_prompt, zero_like(state_rwkv_shift), zero_like(state_rwkv_wkv),
        zero_like(state_gdn_conv), zero_like(state_gdn_S), zero_like(state_hgrn_S), zero_like(state_ffn_conv), P)
    y_s, (s_shift, s_wkv, s_gconv, s_gS, s_hS, s_fconv) = _trunk(
        _Group(Bs, Ts, True), x_sample, state_rwkv_shift, state_rwkv_wkv, state_gdn_conv, state_gdn_S,
        state_hgrn_S, state_ffn_conv, P)
    return (y_p, y_s, p_shift, s_shift, p_wkv, s_wkv, p_gconv, s_gconv,
            p_gS, s_gS, p_hS, s_hS, p_fconv, s_fconv)
```

```python
import functools
import math

import jax
import jax.numpy as jnp
from jax import lax
from jax.experimental import pallas as pl
from jax.experimental.pallas import tpu as pltpu

F32 = jnp.float32
BF16 = jnp.bfloat16

NORM_EPS = 1e-6
L2_EPS = 1e-6
RWKV_LNX_EPS = 64e-5
RWKV_N = 64
LANES = 128
SUBLANES = 8
BF16_ROWS = 16
VMEM_LIMIT = 56 * 1024 * 1024
ROW_TILE = 256
SCAN_ROWS = 256
SCAN_HEADS = 2
SCAN_CHUNK = 64
GLA_SUB = 16


def _dg(a, b, ca, cb):
    return lax.dot_general(a, b, (((ca,), (cb,)), ((), ())), preferred_element_type=F32)


def _mm(a, b):
    return _dg(a.astype(BF16), b.astype(BF16), 1, 0)


def _mm_nt(a, b):
    return _dg(a.astype(BF16), b.astype(BF16), 1, 1)


def _mm_tn(a, b):
    return _dg(a.astype(BF16), b.astype(BF16), 0, 0)


def _split3(x):
    hi = x.astype(BF16)
    r1 = x - hi.astype(F32)
    mid = r1.astype(BF16)
    lo = (r1 - mid.astype(F32)).astype(BF16)
    return hi, mid, lo


def _mm_xl(m, x):
    h, mi, lo = _split3(x)
    m = m.astype(BF16)
    return _dg(m, h, 1, 0) + (_dg(m, mi, 1, 0) + _dg(m, lo, 1, 0))


def _mm_xr(x, m):
    h, mi, lo = _split3(x)
    m = m.astype(BF16)
    return _dg(h, m, 1, 0) + (_dg(mi, m, 1, 0) + _dg(lo, m, 1, 0))


def _iota2(n, m):
    return (lax.broadcasted_iota(jnp.int32, (n, m), 0), lax.broadcasted_iota(jnp.int32, (n, m), 1))


def _sigmoid(x):
    return 1.0 / (1.0 + jnp.exp(-x))


def _silu(x):
    return x * _sigmoid(x)


def _softplus(x):
    return jnp.maximum(x, 0.0) + jnp.log(1.0 + jnp.exp(-jnp.abs(x)))


def _rmsnorm(x, w):
    return x * lax.rsqrt(jnp.mean(x * x, axis=-1, keepdims=True) + NORM_EPS) * w


def _head_sum(x, e):
    return _mm_xr(x, e)


def _unit_lower_inv(Ls, C):
    n = Ls[0].shape[0]
    ri, ci = _iota2(n, n)
    eye = (ri == ci).astype(F32)
    Xs = [eye - L for L in Ls]
    Ps = list(Ls)
    m = 2
    while m < C:
        Ps = [_mm(P, P) for P in Ps]
        Xs = [X + _mm(X, P) for X, P in zip(Xs, Ps)]
        m *= 2
    return Xs


def _tile_masks(C):
    sh = C.bit_length() - 1
    ri, ci = _iota2(LANES, LANES)
    same = jnp.right_shift(ri, sh) == jnp.right_shift(ci, sh)
    return ri, ci, same, same & (ri >= ci), same & (ri > ci)


def _place_rows(x, r0, n):
    parts = []
    if r0:
        parts.append(jnp.zeros((r0, x.shape[1]), x.dtype))
    parts.append(x)
    if n - r0 - x.shape[0]:
        parts.append(jnp.zeros((n - r0 - x.shape[0], x.shape[1]), x.dtype))
    return jnp.concatenate(parts, axis=0) if len(parts) > 1 else x


def _rwkv_tile(r, lw, k, v, kk, bb, C):
    ri, ci, same, tril, stril = _tile_masks(C)
    lane = lax.broadcasted_iota(jnp.int32, (1, LANES), 1)
    m = [(lane < RWKV_N).astype(F32), (lane >= RWKV_N).astype(F32)]
    trilf = tril.astype(F32)
    n = len(r)
    gc = [_mm_xl(trilf, x) for x in lw]
    at = [-kk[g] * jnp.exp(gc[g] - lw[g]) for g in range(n)]
    rt = [r[g] * jnp.exp(gc[g]) for g in range(n)]
    sc = []
    for g in range(n):
        e_neg = jnp.exp(-gc[g])
        lhs = jnp.concatenate([at[g] * m[0], at[g] * m[1], rt[g] * m[0], rt[g] * m[1]], axis=0)
        sc.append(_mm_nt(lhs, jnp.concatenate([bb[g] * e_neg, k[g] * e_neg], axis=0)))
    gh = [(g, hd) for g in range(n) for hd in range(2)]
    a_blk = [sc[g][LANES * hd:LANES * (hd + 1)] for g, hd in gh]
    r_blk = [sc[g][LANES * (2 + hd):LANES * (3 + hd)] for g, hd in gh]
    A_rb = [jnp.where(tril, x[:, :LANES], 0.0) for x in r_blk]
    A_rk = [jnp.where(tril, x[:, LANES:], 0.0) for x in r_blk]
    Tinv = _unit_lower_inv([jnp.where(stril, -x[:, :LANES], 0.0) for x in a_blk], C)
    akv = [_mm(jnp.where(stril, a_blk[p][:, LANES:], 0.0), v[g]) for p, (g, hd) in enumerate(gh)]
    y = [_mm(Tinv[p], jnp.concatenate([at[g] * m[hd], akv[p]], axis=1)) for p, (g, hd) in enumerate(gh)]
    o0 = [_mm(jnp.concatenate([A_rk[p], A_rb[p]], axis=1), jnp.concatenate([v[g], y[p][:, LANES:]], axis=0))
          for p, (g, hd) in enumerate(gh)]
    out = []
    for g in range(n):
        p0, p1 = 2 * g, 2 * g + 1
        out.append(dict(gc=gc[g], rt=rt[g], m=m, Wa=y[p0][:, :LANES] + y[p1][:, :LANES],
                        U0=y[p0][:, LANES:] * m[0] + y[p1][:, LANES:] * m[1],
                        O0=o0[p0] * m[0] + o0[p1] * m[1],
                        A_rb=jnp.concatenate([A_rb[p0], A_rb[p1]], axis=0), k=k[g], v=v[g], bb=bb[g]))
    return out


def _rwkv_block(i, C, ts, Ss):
    sl = slice(C * i, C * (i + 1))
    sl1 = slice(LANES + C * i, LANES + C * (i + 1))
    r2, c2 = _iota2(LANES, LANES)
    blk = ((r2 >= RWKV_N) == (c2 >= RWKV_N)).astype(F32)
    xs = [_mm_nt(jnp.concatenate([t['Wa'][sl], t['rt'][sl]], axis=0), S) for t, S in zip(ts, Ss)]
    S_new, ax = [], []
    for t, S, x in zip(ts, Ss, xs):
        X = x[:C]
        U = t['U0'][sl] + X
        gl = t['gc'][C * (i + 1) - 1:C * (i + 1), :]
        e_last = jnp.exp(gl - t['gc'][sl])
        S_new.append(S * jnp.exp(gl) + blk * _mm_tn(
            jnp.concatenate([U, t['v'][sl]], axis=0),
            jnp.concatenate([t['bb'][sl] * e_last, t['k'][sl] * e_last], axis=0)))
        ax.append(_mm(jnp.concatenate([t['A_rb'][sl], t['A_rb'][sl1]], axis=0), _place_rows(X, C * i, LANES)))
    o = [t['O0'][sl] + x[C:] + a[:C] * t['m'][0] + a[C:] * t['m'][1] for t, x, a in zip(ts, xs, ax)]
    return o, S_new


def _gdn_tile(q, k, v, g_row, beta_row, C):
    ri, ci, same, tril, stril = _tile_masks(C)
    eye = ri == ci
    lsum = lambda msk, x: jnp.sum(jnp.where(msk, x, 0.0), axis=1, keepdims=True)
    n = len(q)
    pre = []
    for g in range(n):
        g_b = jnp.broadcast_to(g_row[g], (LANES, LANES))
        beta_col = lsum(eye, jnp.broadcast_to(beta_row[g], (LANES, LANES)))
        g_col = lsum(eye, g_b)
        gc_col = lsum(tril, g_b)
        gl_col = lsum(same, g_b)
        gc_row = jnp.sum(jnp.where(same & (ri <= ci), jnp.broadcast_to(g_col, (LANES, LANES)), 0.0),
                         axis=0, keepdims=True)
        decay = jnp.where(tril, jnp.exp(jnp.minimum(gc_col - gc_row, 0.0)), 0.0)
        pre.append(dict(beta=beta_col, gc=gc_col, gl=gl_col, decay=decay, kb=k[g] * beta_col))
    sc = [_mm_nt(jnp.concatenate([pre[g]['kb'], q[g]], axis=0), k[g]) for g in range(n)]
    Tinv = _unit_lower_inv([jnp.where(stril, sc[g][:LANES] * pre[g]['decay'], 0.0) for g in range(n)], C)
    out = []
    for g in range(n):
        p = pre[g]
        e_gc = jnp.exp(p['gc'])
        uw = _mm(Tinv[g], jnp.concatenate([v[g] * p['beta'], p['kb'] * e_gc], axis=1))
        out.append(dict(u=uw[:, :LANES], wk=uw[:, LANES:], qk=sc[g][LANES:] * p['decay'], qg=q[g] * e_gc,
                        kd=k[g] * jnp.exp(p['gl'] - p['gc']), egl=jnp.exp(p['gl'])))
    return out


def _gdn_block(i, C, ts, Ss):
    sl = slice(C * i, C * (i + 1))
    ws = [_mm(jnp.concatenate([t['wk'][sl], t['qg'][sl]], axis=0), S) for t, S in zip(ts, Ss)]
    v_new = [t['u'][sl] - w[:C] for t, w in zip(ts, ws)]
    S_new = [S * t['egl'][C * i:C * i + 1, :] + _mm_tn(t['kd'][sl], vn) for t, S, vn in zip(ts, Ss, v_new)]
    o = [w[C:] + _mm(t['qk'][sl], _place_rows(vn, C * i, LANES)) for t, w, vn in zip(ts, ws, v_new)]
    return o, S_new


def _gla_tile(q_ref, k_ref, lf_ref, v_ref, r0, c0, C, gc_scr):
    rows = slice(r0, r0 + LANES)
    cols = slice(c0, c0 + LANES)
    q = q_ref[rows, cols]
    k = k_ref[rows, cols]
    ri, ci, same, tril, _ = _tile_masks(C)
    sub = GLA_SUB.bit_length() - 1
    rsub, csub = jnp.right_shift(ri, sub), jnp.right_shift(ci, sub)
    gc = _mm_xl(tril.astype(F32), lf_ref[rows, cols])
    gc_scr[...] = gc
    lane = lax.broadcasted_iota(jnp.int32, (1, LANES), 1)
    strips = []
    for I in range(LANES // GLA_SUB):
        s0 = GLA_SUB * I
        ksub = k[s0:s0 + GLA_SUB]
        gsub = gc[s0:s0 + GLA_SUB]
        strip = jnp.zeros((GLA_SUB, LANES), F32)
        for ii in range(GLA_SUB):
            i = s0 + ii
            gi = gc_scr[i:i + 1, :]
            qi = q_ref[r0 + i:r0 + i + 1, cols]
            e = jnp.exp(jnp.minimum(gi - gsub, 0.0))
            strip = jnp.where(lane == i, jnp.sum(qi * ksub * e, axis=1, keepdims=True), strip)
        strips.append(strip)
    AT = jnp.where((rsub == csub) & (ri <= ci), jnp.concatenate(strips, axis=0), 0.0)
    nsub = C // GLA_SUB
    if nsub > 1:
        off = []
        for b in range(LANES // C):
            b0 = C * b
            kblk = k[b0:b0 + C]
            gblk = gc[b0:b0 + C]
            acc = jnp.zeros((C, LANES), F32)
            for J in range(1, nsub):
                s0 = b0 + GLA_SUB * J
                ref = gc_scr[s0 - 1:s0, :]
                khat = kblk * jnp.exp(jnp.minimum(ref - gblk, 0.0))
                qhat = q[s0:s0 + GLA_SUB] * jnp.exp(gc[s0:s0 + GLA_SUB] - ref)
                acc = acc + _mm_nt(khat, _place_rows(qhat, s0, LANES))
            off.append(acc)
        AT = AT + jnp.where(same & (rsub < csub), jnp.concatenate(off, axis=0), 0.0)
    v = v_ref[rows, cols]
    return dict(gc=gc, k=k, v=v, qg=q * jnp.exp(gc), o_intra=_mm_tn(AT, v))


def _gla_block(i, C, ts, STs):
    sl = slice(C * i, C * (i + 1))
    o, ST_new = [], []
    for t, ST in zip(ts, STs):
        gl = t['gc'][C * (i + 1) - 1:C * (i + 1), :]
        o.append(t['o_intra'][sl] + _mm_nt(t['qg'][sl], ST))
        ST_new.append(ST * jnp.exp(gl) + _mm_tn(t['v'][sl], t['k'][sl] * jnp.exp(gl - t['gc'][sl])))
    return o, ST_new


def _shifted(ext, hp, s, tm, back):
    return ext[hp - back * s:hp - back * s + tm, :]


def _rwkv_pre_kernel(has_vres, s, *refs):
    if has_vres:
        (x_ref, vf_ref, sh0_ref, nw_ref, mix_ref, wrkv_ref, w0_ref, w1_ref, w2_ref, a0_ref, a1_ref, a2_ref,
         g1_ref, g2_ref, kkw_ref, ka_ref, e_ref, v0_ref, v1_ref, v2_ref,
         r_out, lw_out, k_out, v_out, kk_out, bb_out, g_out, sh_out, ext) = refs
    else:
        (x_ref, sh0_ref, nw_ref, mix_ref, wrkv_ref, w0_ref, w1_ref, w2_ref, a0_ref, a1_ref, a2_ref,
         g1_ref, g2_ref, kkw_ref, ka_ref, e_ref,
         r_out, lw_out, k_out, v_out, kk_out, bb_out, g_out, sh_out, ext) = refs
    t = pl.program_id(1)
    tm, D = x_ref.shape
    hp = ext.shape[0] - tm
    h = _rmsnorm(x_ref[...], nw_ref[...])

    @pl.when(t == 0)
    def _():
        ext[hp - s:hp, :] = sh0_ref[...]

    ext[hp:, :] = h
    d = _shifted(ext, hp, s, tm, 1) - h
    ext[hp - s:hp, :] = h[tm - s:, :]

    @pl.when(t == pl.num_programs(1) - 1)
    def _():
        sh_out[...] = h[tm - s:, :]

    mixed = lambda i: h + d * mix_ref[i:i + 1, :]
    r = _mm(mixed(0), wrkv_ref[0])
    z = w0_ref[...] + _mm(jnp.tanh(_mm(mixed(1), w1_ref[...])), w2_ref[...])
    lw_out[...] = -math.exp(-0.5) * _sigmoid(z)
    k = _mm(mixed(2), wrkv_ref[1])
    xv = mixed(3)
    v = _mm(xv, wrkv_ref[2])
    a = _sigmoid(a0_ref[...] + _mm(_mm(mixed(4), a1_ref[...]), a2_ref[...]))
    if has_vres:
        gate_v = _sigmoid(v0_ref[...] + _mm(_mm(xv, v1_ref[...]), v2_ref[...]))
        v = v + (vf_ref[...] - v) * gate_v
    g_out[...] = _mm(_sigmoid(_mm(mixed(5), g1_ref[...])), g2_ref[...])
    r_out[...] = r
    v_out[...] = v
    kkraw = k * kkw_ref[...]
    e = e_ref[...]
    for c in range(D // LANES):
        sl = slice(c * LANES, (c + 1) * LANES)
        kc = kkraw[:, sl]
        kkn = kc * lax.rsqrt(_head_sum(kc * kc, e) + L2_EPS)
        kk_out[:, sl] = kkn
        bb_out[:, sl] = kkn * a[:, sl]
    k_out[...] = k * (1.0 + (a - 1.0) * ka_ref[...])


def _gdn_pre_kernel(s, nheads, x_ref, c0_ref, nw_ref, wqkv_ref, wgate_ref, wbg_ref, cw_ref, alog_ref, dtb_ref,
                    q_out, k_out, v_out, gate_out, bg_out, cnew_out, ext):
    t = pl.program_id(1)
    tm, D = x_ref.shape
    hp = ext.shape[0] - tm
    nh = cw_ref.shape[0] - 1
    h = _rmsnorm(x_ref[...], nw_ref[...])

    @pl.when(t == 0)
    def _():
        ext[hp - nh * s:hp, :] = c0_ref[...]

    ext[hp:, :] = _mm(h, wqkv_ref[...])
    y = _shifted(ext, hp, s, tm, 0) * cw_ref[nh:nh + 1, :]
    for j in range(nh):
        y = y + _shifted(ext, hp, s, tm, nh - j) * cw_ref[j:j + 1, :]
    hist = ext[hp + tm - nh * s:hp + tm, :]

    @pl.when(t == pl.num_programs(1) - 1)
    def _():
        cnew_out[...] = hist

    ext[hp - nh * s:hp, :] = hist
    y = _silu(y)
    qk_w = D
    for c in range(qk_w // LANES):
        sl = slice(c * LANES, (c + 1) * LANES)
        qc = y[:, sl]
        q_out[:, sl] = qc * (lax.rsqrt(jnp.sum(qc * qc, axis=-1, keepdims=True) + L2_EPS) * LANES ** -0.5)
        sl2 = slice(qk_w + c * LANES, qk_w + (c + 1) * LANES)
        kc = y[:, sl2]
        k_out[:, sl] = kc * lax.rsqrt(jnp.sum(kc * kc, axis=-1, keepdims=True) + L2_EPS)
    v_out[...] = y[:, 2 * qk_w:]
    gate_out[...] = _mm(h, wgate_ref[...])
    rest = _mm(h, wbg_ref[...])
    lane = lax.broadcasted_iota(jnp.int32, rest.shape, 1)
    bg_out[...] = jnp.where(lane < nheads, _sigmoid(rest),
                            -jnp.exp(alog_ref[...]) * _softplus(rest + dtb_ref[...]))


def _hgrn_pre_kernel(layer, x_ref, nw_ref, win_ref, lb_ref, q_out, k_out, lf_out, v_out, gate_out):
    D = x_ref.shape[1]
    h = _rmsnorm(x_ref[...], nw_ref[...])
    lbp = lb_ref[...]
    ex = jnp.exp(lbp - jnp.max(lbp, axis=0, keepdims=True))
    soft = ex / jnp.sum(ex, axis=0, keepdims=True)
    row = lax.broadcasted_iota(jnp.int32, soft.shape, 0)
    lb = jnp.sum(jnp.where((row >= 1) & (row <= layer), soft, 0.0), axis=0, keepdims=True)
    p = _mm(h, win_ref[...])
    q_out[...] = _silu(p[:, :D])
    f = lb + (1.0 - lb) * _sigmoid(p[:, D:2 * D])
    k_out[...] = 1.0 - f
    lf_out[...] = jnp.log(f)
    v_out[...] = p[:, 2 * D:3 * D]
    gate_out[...] = p[:, 3 * D:]


def _out_proj_kernel(y_ref, x_ref, wo_ref, pw_ref, x_out):
    x_out[...] = x_ref[...] + _rmsnorm(_mm(y_ref[...], wo_ref[...]), pw_ref[...])


def _ffn_kernel(s, x_ref, c0_ref, nw_ref, wup_ref, cw_ref, wdn_ref, pw_ref, x_out, cnew_out, ext):
    t = pl.program_id(1)
    tm, D = x_ref.shape
    hp = ext.shape[0] - tm
    nh = cw_ref.shape[0] - 1
    dff = wdn_ref.shape[0]
    x = x_ref[...]
    h = _rmsnorm(x, nw_ref[...])

    @pl.when(t == 0)
    def _():
        ext[hp - nh * s:hp, :] = c0_ref[...]

    ext[hp:, :] = _mm(h, wup_ref[...])
    y = _shifted(ext, hp, s, tm, 0) * cw_ref[nh:nh + 1, :]
    for j in range(nh):
        y = y + _shifted(ext, hp, s, tm, nh - j) * cw_ref[j:j + 1, :]
    hist = ext[hp + tm - nh * s:hp + tm, :]

    @pl.when(t == pl.num_programs(1) - 1)
    def _():
        cnew_out[...] = hist

    ext[hp - nh * s:hp, :] = hist
    act = _silu(y[:, dff:]) * y[:, :dff]
    x_out[...] = x + _rmsnorm(_mm(act, wdn_ref[...]), pw_ref[...])


def _const_spec(a):
    nd = a.ndim
    return pl.BlockSpec(a.shape, lambda g, t: (0,) * nd, pipeline_mode=pl.Buffered(1))


def _row_call(body, name, tm, tiled_ins, group_ins, const_ins, tiled_out_widths, group_out_shapes, scratch):
    G, R, _ = tiled_ins[0].shape
    assert R % tm == 0
    tile_spec = lambda c: pl.BlockSpec((None, tm, c), lambda g, t: (g, t, 0))
    group_spec = lambda n, c: pl.BlockSpec((None, n, c), lambda g, t: (g, 0, 0))
    group_in_spec = lambda n, c: pl.BlockSpec((None, n, c), lambda g, t: (g, 0, 0), pipeline_mode=pl.Buffered(1))
    in_specs = ([tile_spec(a.shape[2]) for a in tiled_ins] + [group_in_spec(*a.shape[1:]) for a in group_ins]
                + [_const_spec(a) for a in const_ins])
    out_specs = [tile_spec(c) for c in tiled_out_widths] + [group_spec(n, c) for n, c in group_out_shapes]
    out_shape = ([jax.ShapeDtypeStruct((G, R, c), F32) for c in tiled_out_widths]
                 + [jax.ShapeDtypeStruct((G, n, c), F32) for n, c in group_out_shapes])
    return pl.pallas_call(
        body, name=name, grid=(G, R // tm), in_specs=in_specs, out_specs=out_specs, out_shape=out_shape,
        scratch_shapes=scratch,
        compiler_params=pltpu.CompilerParams(dimension_semantics=("parallel", "arbitrary"),
                                             vmem_limit_bytes=VMEM_LIMIT),
    )(*tiled_ins, *group_ins, *const_ins)


def _hist_pad(n):
    return -(-n // SUBLANES) * SUBLANES


def _tile(ref, p):
    h, j = p
    return ref[LANES * j:LANES * (j + 1), LANES * h:LANES * (h + 1)]


def _scan_states(chained, C, probs, tiles, s0_ref, s_out, scr, load, store, block_fn):
    nb = LANES // C
    tt = pl.program_id(2)
    heads = sorted({h for h, _ in probs})
    nt = len(probs) // len(heads)
    outs = {}
    if chained:
        @pl.when(tt == 0)
        def _():
            for h in heads:
                scr[h] = load(s0_ref[h])

        Ss = [scr[h] for h in heads]
        for j in range(nt):
            sel = [tiles[probs.index((h, j))] for h in heads]
            for i in range(nb):
                os, Ss = block_fn(i, sel, Ss)
                for h, o in zip(heads, os):
                    outs[(h, j, i)] = o
        for h, S in zip(heads, Ss):
            scr[h] = S

        @pl.when(tt == pl.num_programs(2) - 1)
        def _():
            for h, S in zip(heads, Ss):
                s_out[h] = store(S)
    else:
        for i in range(nb):
            Ss = [load(s0_ref[j * nb + i, h]) for h, j in probs]
            os, Ss = block_fn(i, tiles, Ss)
            for (h, j), o, S in zip(probs, os, Ss):
                outs[(h, j, i)] = o
                s_out[j * nb + i, h] = store(S)
    return [jnp.concatenate([outs[(h, j, i)] for i in range(nb)], axis=0) if nb > 1 else outs[(h, j, 0)]
            for h, j in probs]


def _scan_probs(ref):
    return [(h, j) for h in range(ref.shape[1] // LANES) for j in range(ref.shape[0] // LANES)]


def _rwkv_scan_kernel(C, chained, r_ref, lw_ref, k_ref, v_ref, kk_ref, bb_ref, g_ref, rk_ref, lnw_ref, lnb_ref,
                      e_ref, s0_ref, y_out, s_out, scr):
    N = RWKV_N
    e = e_ref[...]
    ident = lambda a: a
    probs = _scan_probs(r_ref)
    get = lambda ref: [_tile(ref, p) for p in probs]
    r, k, v = get(r_ref), get(k_ref), get(v_ref)
    tiles = _rwkv_tile(r, get(lw_ref), k, v, get(kk_ref), get(bb_ref), C)
    O = _scan_states(chained, C, probs, tiles, s0_ref, s_out, scr, ident, ident,
                     lambda i, ts, Ss: _rwkv_block(i, C, ts, Ss))
    for n, (h, j) in enumerate(probs):
        lanes = slice(LANES * h, LANES * (h + 1))
        o = O[n]
        mu = _head_sum(o, e) * (1.0 / N)
        d = o - mu
        var = _head_sum(d * d, e) * (1.0 / N)
        on = d * lax.rsqrt(var + RWKV_LNX_EPS) * lnw_ref[:, lanes] + lnb_ref[:, lanes]
        bonus = _head_sum(r[n] * k[n] * rk_ref[:, lanes], e) * v[n]
        y_out[LANES * j:LANES * (j + 1), lanes] = (on + bonus) * _tile(g_ref, (h, j))


def _gdn_scan_kernel(C, chained, q_ref, k_ref, v_ref, gate_ref, g_ref, beta_ref, nw_ref, s0_ref, y_out, s_out,
                     scr):
    ident = lambda a: a
    probs = _scan_probs(q_ref)
    get = lambda ref: [_tile(ref, p) for p in probs]
    tiles = _gdn_tile(get(q_ref), get(k_ref), get(v_ref), [g_ref[h, j:j + 1, :] for h, j in probs],
                      [beta_ref[h, j:j + 1, :] for h, j in probs], C)
    O = _scan_states(chained, C, probs, tiles, s0_ref, s_out, scr, ident, ident,
                     lambda i, ts, Ss: _gdn_block(i, C, ts, Ss))
    for n, (h, j) in enumerate(probs):
        y_out[LANES * j:LANES * (j + 1), LANES * h:LANES * (h + 1)] = (
            _rmsnorm(O[n], nw_ref[...]) * _silu(_tile(gate_ref, (h, j))))


def _gla_scan_kernel(C, chained, q_ref, k_ref, lf_ref, v_ref, gate_ref, nw_ref, s0_ref, y_out, s_out, scr, gc_scr):
    tr = lambda a: a.T
    probs = _scan_probs(q_ref)
    tiles = [_gla_tile(q_ref, k_ref, lf_ref, v_ref, LANES * j, LANES * h, C, gc_scr) for h, j in probs]
    O = _scan_states(chained, C, probs, tiles, s0_ref, s_out, scr, tr, tr,
                     lambda i, ts, Ss: _gla_block(i, C, ts, Ss))
    for n, (h, j) in enumerate(probs):
        y_out[LANES * j:LANES * (j + 1), LANES * h:LANES * (h + 1)] = (
            _rmsnorm(O[n], nw_ref[...]) * _silu(_tile(gate_ref, (h, j))))


def _scan_call(body, name, C, chained, tiles, extra_ins, extra_specs, s0, nheads, extra_scratch=()):
    NB, RB, D = tiles[0].shape
    rows = min(SCAN_ROWS, RB)
    hg = SCAN_HEADS
    assert RB % rows == 0 and rows % LANES == 0 and nheads % hg == 0
    tile_spec = pl.BlockSpec((None, rows, hg * LANES), lambda b, h, t: (b, t, h))
    if chained:
        s_spec = pl.BlockSpec((None, hg, LANES, LANES), lambda b, h, t: (b, h, 0, 0))
    else:
        nt = RB // rows
        s_spec = pl.BlockSpec((rows // C, hg, LANES, LANES), lambda b, h, t: (b * nt + t, h, 0, 0))
    return pl.pallas_call(
        functools.partial(body, C, chained), name=name, grid=(NB, nheads // hg, RB // rows),
        in_specs=[tile_spec] * len(tiles) + list(extra_specs) + [s_spec],
        out_specs=[tile_spec, s_spec],
        out_shape=[jax.ShapeDtypeStruct((NB, RB, D), F32), jax.ShapeDtypeStruct(s0.shape, F32)],
        scratch_shapes=[pltpu.VMEM((hg, LANES, LANES), F32)] + list(extra_scratch),
        compiler_params=pltpu.CompilerParams(dimension_semantics=("parallel", "parallel", "arbitrary"),
                                             vmem_limit_bytes=VMEM_LIMIT),
    )(*tiles, *extra_ins, s0)


class _Group:
    def __init__(self, B, T, time_major):
        self.B, self.T, self.tm_major = B, T, time_major
        if time_major:
            self.G, self.R, self.s = 1, B * T, B
            self.C = -(-T // BF16_ROWS) * BF16_ROWS
            assert LANES % self.C == 0 and (B * self.C) % LANES == 0
        else:
            self.G, self.R, self.s = B, T, 1
            self.C = SCAN_CHUNK
            assert T % LANES == 0
        self.chained = not time_major
        self.tile = min(ROW_TILE, self.R)
        self.ffn_tile = self.s if time_major else self.tile

    def to_rows(self, x):
        if self.tm_major:
            return jnp.swapaxes(x, 0, 1).reshape(1, self.R, x.shape[-1])
        return x

    def hist_to_rows(self, h):
        if self.tm_major:
            return jnp.swapaxes(h, 0, 1).reshape(1, -1, h.shape[-1])
        return h

    def hist_from_rows(self, h, n):
        if self.tm_major:
            return jnp.swapaxes(h.reshape(n, self.B, h.shape[-1]), 0, 1)
        return h

    def to_scan(self, a):
        if not self.tm_major:
            return a
        a = jnp.swapaxes(a.reshape(self.T, self.B, a.shape[-1]), 0, 1)
        a = jnp.pad(a, ((0, 0), (0, self.C - self.T), (0, 0)))
        return a.reshape(1, self.B * self.C, a.shape[-1])

    def from_scan(self, a):
        if not self.tm_major:
            return a
        return self.to_rows(a.reshape(self.B, self.C, a.shape[-1])[:, :self.T])


def _rwkv_layer(grp, x, shift0, S0, v_first, P, j):
    D = x.shape[-1]
    s, tm = grp.s, grp.tile
    N = RWKV_N
    has_vres = v_first is not None
    r2, c2 = _iota2(LANES, LANES)
    e = ((r2 // N) == (c2 // N)).astype(BF16)
    row = lambda a: a.reshape(1, -1)
    tiled = [x] + ([v_first] if has_vres else [])
    consts = [row(P['norm_mix_pre_i']), P['rwkv_mix'][j], P['rwkv_w_rkv'][j].astype(BF16),
              row(P['rwkv_w0'][j]), P['rwkv_w1'][j].astype(BF16), P['rwkv_w2'][j].astype(BF16),
              row(P['rwkv_a0'][j]), P['rwkv_a1'][j].astype(BF16), P['rwkv_a2'][j].astype(BF16),
              P['rwkv_g1'][j].astype(BF16), P['rwkv_g2'][j].astype(BF16),
              row(P['rwkv_k_k'][j]), row(P['rwkv_k_a'][j]), e]
    if has_vres:
        consts += [row(P['rwkv_v0'][j - 1]), P['rwkv_v1'][j - 1].astype(BF16), P['rwkv_v2'][j - 1].astype(BF16)]
    r, lw, k, v, kk, bb, g, shift = _row_call(
        functools.partial(_rwkv_pre_kernel, has_vres, s), f"rwkv_pre_{j}", tm, tiled, [shift0], consts,
        [D] * 7, [(s, D)], [pltpu.VMEM((_hist_pad(s) + tm, D), F32)])
    if not has_vres:
        v_first = v
    tiles = [grp.to_scan(a) for a in (r, lw, k, v, kk, bb, g)]
    vec_spec = pl.BlockSpec((1, SCAN_HEADS * LANES), lambda b, h, t: (0, h))
    e_spec = pl.BlockSpec((LANES, LANES), lambda b, h, t: (0, 0))
    Bq, H = S0.shape[:2]
    S0p = S0.reshape(Bq, H // 2, 2, N, N)
    zero = jnp.zeros_like(S0p[:, :, 0])
    S0bd = jnp.concatenate([jnp.concatenate([S0p[:, :, 0], zero], axis=-1),
                            jnp.concatenate([zero, S0p[:, :, 1]], axis=-1)], axis=-2)
    y, Sbd = _scan_call(_rwkv_scan_kernel, f"rwkv_scan_{j}", grp.C, grp.chained, tiles,
                        [row(P['rwkv_r_k'][j]), row(P['rwkv_lnx_w'][j]), row(P['rwkv_lnx_b'][j]), e],
                        [vec_spec] * 3 + [e_spec], S0bd, D // LANES)
    S = jnp.stack([Sbd[:, :, :N, :N], Sbd[:, :, N:, N:]], axis=2).reshape(S0.shape)
    return grp.from_scan(y), P['rwkv_w_o'][j], shift, S, v_first


def _gdn_layer(grp, x, conv0, S0, P, j):
    D = x.shape[-1]
    s, tm = grp.s, grp.tile
    w_in = P['gdn_w_in'][j]
    cw = P['gdn_conv_w'][j]
    nh, cdim = cw.shape[0] - 1, cw.shape[1]
    H = S0.shape[1]
    assert grp.T >= nh
    row = lambda a: a.reshape(1, -1)
    lane_pad = lambda a: jnp.pad(a, ((0, 0), (0, LANES - a.shape[1])))
    zeros = jnp.zeros((1, H), F32)
    consts = [row(P['norm_mix_pre_i']), w_in[:, :cdim].astype(BF16), w_in[:, cdim:cdim + D].astype(BF16),
              lane_pad(w_in[:, cdim + D:]).astype(BF16), cw,
              lane_pad(jnp.concatenate([zeros, row(P['gdn_a_log'][j])], axis=1)),
              lane_pad(jnp.concatenate([zeros, row(P['gdn_dt_bias'][j])], axis=1))]
    q, k, v, gate, bg, conv_new = _row_call(
        functools.partial(_gdn_pre_kernel, s, H), f"gdn_pre_{j}", tm, [x], [conv0], consts,
        [D, D, D, D, LANES], [(nh * s, cdim)], [pltpu.VMEM((_hist_pad(nh * s) + tm, cdim), F32)])
    tiles = [grp.to_scan(a) for a in (q, k, v, gate)]
    NB, RB, _ = tiles[0].shape
    rows = min(SCAN_ROWS, RB)
    nt = RB // rows
    bg = grp.to_scan(bg[:, :, :2 * H]).reshape(NB * RB, 2 * H).T.reshape(2 * H, NB * nt, rows // LANES, LANES)
    hg = SCAN_HEADS
    g_spec = pl.BlockSpec((hg, None, rows // LANES, LANES), lambda b, h, t: (H // hg + h, b * nt + t, 0, 0))
    beta_spec = pl.BlockSpec((hg, None, rows // LANES, LANES), lambda b, h, t: (h, b * nt + t, 0, 0))
    vec_spec = pl.BlockSpec((1, LANES), lambda b, h, t: (0, 0))
    y, S = _scan_call(_gdn_scan_kernel, f"gdn_scan_{j}", grp.C, grp.chained, tiles,
                      [bg, bg, row(P['gdn_norm_w'][j])], [g_spec, beta_spec, vec_spec], S0, H)
    return grp.from_scan(y), P['gdn_w_o'][j], conv_new, S


def _hgrn_layer(grp, x, S0, P, i, j):
    D = x.shape[-1]
    tm = grp.tile
    H = S0.shape[1]
    row = lambda a: a.reshape(1, -1)
    consts = [row(P['norm_mix_pre_i']), P['hgrn_w_in'][j].astype(BF16), P['hgrn_lb']]
    q, k, lf, v, gate = _row_call(functools.partial(_hgrn_pre_kernel, i), f"hgrn_pre_{j}", tm, [x], [], consts,
                                  [D] * 5, [], [])
    tiles = [grp.to_scan(a) for a in (q, k, lf, v, gate)]
    vec_spec = pl.BlockSpec((1, LANES), lambda b, h, t: (0, 0))
    y, S = _scan_call(_gla_scan_kernel, f"hgrn_scan_{j}", grp.C, grp.chained, tiles, [row(P['hgrn_norm_w'][j])],
                      [vec_spec], S0, H, extra_scratch=[pltpu.VMEM((LANES, LANES), F32)])
    return grp.from_scan(y), P['hgrn_w_o'][j], S


def _trunk(grp, x, shift0, wkv0, gconv0, gS0, hS0, fconv0, P):
    D = x.shape[-1]
    depth = P['norm_mix_pre'].shape[0]
    row = lambda a: a.reshape(1, -1)
    x = grp.to_rows(x)
    v_first = None
    shift, wkv, gconv, gS, hS, fconv = [], [], [], [], [], []
    for i in range(depth):
        kind, j = i % 3, i // 3
        P = dict(P, norm_mix_pre_i=P['norm_mix_pre'][i])
        if kind == 0:
            y, w_o, s_shift, s_wkv, v_first = _rwkv_layer(grp, x, grp.hist_to_rows(shift0[j][:, None]), wkv0[j],
                                                         v_first, P, j)
            shift.append(grp.hist_from_rows(s_shift, 1)[:, 0])
            wkv.append(s_wkv)
        elif kind == 1:
            y, w_o, c_new, s_new = _gdn_layer(grp, x, grp.hist_to_rows(gconv0[j]), gS0[j], P, j)
            gconv.append(grp.hist_from_rows(c_new, gconv0.shape[2]))
            gS.append(s_new)
        else:
            y, w_o, s_new = _hgrn_layer(grp, x, hS0[j], P, i, j)
            hS.append(s_new)
        (x,) = _row_call(_out_proj_kernel, f"out_proj_{i}", grp.tile, [y, x], [],
                         [w_o.astype(BF16), row(P['norm_mix_post'][i])], [D], [], [])
        nh = P['ffn_conv_w'].shape[1] - 1
        dff2 = P['ffn_w_up'].shape[2]
        x, c_new = _row_call(
            functools.partial(_ffn_kernel, grp.s), f"ffn_{i}", grp.ffn_tile, [x], [grp.hist_to_rows(fconv0[i])],
            [row(P['norm_ffn_pre'][i]), P['ffn_w_up'][i].astype(BF16), P['ffn_conv_w'][i],
             P['ffn_w_down'][i].astype(BF16), row(P['norm_ffn_post'][i])],
            [D], [(nh * grp.s, dff2)], [pltpu.VMEM((_hist_pad(nh * grp.s) + grp.ffn_tile, dff2), F32)])
        fconv.append(grp.hist_from_rows(c_new, nh))
    y = x.reshape(grp.T, grp.B, D).swapaxes(0, 1) if grp.tm_major else x
    return y, (jnp.stack(shift), jnp.stack(wkv), jnp.stack(gconv), jnp.stack(gS), jnp.stack(hS), jnp.stack(fconv))


def kernel(x_prompt, x_sample, state_rwkv_shift, state_rwkv_wkv, state_gdn_conv, state_gdn_S, state_hgrn_S, state_ffn_conv, norm_mix_pre, norm_mix_post, norm_ffn_pre, norm_ffn_post, rwkv_mix, rwkv_w_rkv, rwkv_w0, rwkv_w1, rwkv_w2, rwkv_a0, rwkv_a1, rwkv_a2, rwkv_v0, rwkv_v1, rwkv_v2, rwkv_g1, rwkv_g2, rwkv_k_k, rwkv_k_a, rwkv_r_k, rwkv_lnx_w, rwkv_lnx_b, rwkv_w_o, gdn_w_in, gdn_conv_w, gdn_a_log, gdn_dt_bias, gdn_norm_w, gdn_w_o, hgrn_w_in, hgrn_lb, hgrn_norm_w, hgrn_w_o, ffn_w_up, ffn_conv_w, ffn_w_down):
    P = dict(norm_mix_pre=norm_mix_pre, norm_mix_post=norm_mix_post, norm_ffn_pre=norm_ffn_pre,
             norm_ffn_post=norm_ffn_post, rwkv_mix=rwkv_mix, rwkv_w_rkv=rwkv_w_rkv, rwkv_w0=rwkv_w0,
             rwkv_w1=rwkv_w1, rwkv_w2=rwkv_w2, rwkv_a0=rwkv_a0, rwkv_a1=rwkv_a1, rwkv_a2=rwkv_a2,
             rwkv_v0=rwkv_v0, rwkv_v1=rwkv_v1, rwkv_v2=rwkv_v2, rwkv_g1=rwkv_g1, rwkv_g2=rwkv_g2,
             rwkv_k_k=rwkv_k_k, rwkv_k_a=rwkv_k_a, rwkv_r_k=rwkv_r_k, rwkv_lnx_w=rwkv_lnx_w,
             rwkv_lnx_b=rwkv_lnx_b, rwkv_w_o=rwkv_w_o, gdn_w_in=gdn_w_in, gdn_conv_w=gdn_conv_w,
             gdn_a_log=gdn_a_log, gdn_dt_bias=gdn_dt_bias, gdn_norm_w=gdn_norm_w, gdn_w_o=gdn_w_o,
             hgrn_w_in=hgrn_w_in, hgrn_lb=hgrn_lb, hgrn_norm_w=hgrn_norm_w, hgrn_w_o=hgrn_w_o,
             ffn_w_up=ffn_w_up, ffn_conv_w=ffn_conv_w, ffn_w_down=ffn_w_down)
    Bp, Tp, _ = x_prompt.shape
    Bs, Ts, _ = x_sample.shape
    zero_like = lambda st: jnp.zeros((st.shape[0], Bp) + st.shape[2:], st.dtype)
    y_p, (p_shift, p_wkv, p_gconv, p_gS, p_hS, p_fconv) = _trunk(
        _Group(Bp, Tp, False), x_prompt, zero_like(state_rwkv_shift), zero_like(state_rwkv_wkv),
        zero_like(state_gdn_conv), zero_like(state_gdn_S), zero_like(state_hgrn_S), zero_like(state_ffn_conv), P)
    y_s, (s_shift, s_wkv, s_gconv, s_gS, s_hS, s_fconv) = _trunk(
        _Group(Bs, Ts, True), x_sample, state_rwkv_shift, state_rwkv_wkv, state_gdn_conv, state_gdn_S,
        state_hgrn_S, state_ffn_conv, P)
    return (y_p, y_s, p_shift, s_shift, p_wkv, s_wkv, p_gconv, s_gconv,
            p_gS, s_gS, p_hS, s_hS, p_fconv, s_fconv)
```

```python
import functools
import math

import jax
import jax.numpy as jnp
from jax import lax
from jax.experimental import pallas as pl
from jax.experimental.pallas import tpu as pltpu

F32 = jnp.float32
BF16 = jnp.bfloat16

NORM_EPS = 1e-6
L2_EPS = 1e-6
RWKV_LNX_EPS = 64e-5
RWKV_N = 64
LANES = 128
SUBLANES = 8
BF16_ROWS = 16
VMEM_LIMIT = 56 * 1024 * 1024
ROW_TILE = 256
SCAN_ROWS = 256
SCAN_HEADS = 4
SCAN_CHUNK = 64
GLA_SUB = 8


def _dg(a, b, ca, cb):
    return lax.dot_general(a, b, (((ca,), (cb,)), ((), ())), preferred_element_type=F32)


def _mm(a, b):
    return _dg(a.astype(BF16), b.astype(BF16), 1, 0)


def _mm_nt(a, b):
    return _dg(a.astype(BF16), b.astype(BF16), 1, 1)


def _mm_tn(a, b):
    return _dg(a.astype(BF16), b.astype(BF16), 0, 0)


def _split3(x):
    hi = x.astype(BF16)
    r1 = x - hi.astype(F32)
    mid = r1.astype(BF16)
    lo = (r1 - mid.astype(F32)).astype(BF16)
    return hi, mid, lo


def _mm_xl(m, x):
    h, mi, lo = _split3(x)
    m = m.astype(BF16)
    return _dg(m, h, 1, 0) + (_dg(m, mi, 1, 0) + _dg(m, lo, 1, 0))


def _mm_xr(x, m):
    h, mi, lo = _split3(x)
    m = m.astype(BF16)
    return _dg(h, m, 1, 0) + (_dg(mi, m, 1, 0) + _dg(lo, m, 1, 0))


def _iota2(n, m):
    return (lax.broadcasted_iota(jnp.int32, (n, m), 0), lax.broadcasted_iota(jnp.int32, (n, m), 1))


def _sigmoid(x):
    return 1.0 / (1.0 + jnp.exp(-x))


def _silu(x):
    return x * _sigmoid(x)


def _softplus(x):
    return jnp.maximum(x, 0.0) + jnp.log(1.0 + jnp.exp(-jnp.abs(x)))


def _rmsnorm(x, w):
    return x * lax.rsqrt(jnp.mean(x * x, axis=-1, keepdims=True) + NORM_EPS) * w


def _head_sum(x, e):
    return _mm_xr(x, e)


def _unit_lower_inv(Ls, C):
    n = Ls[0].shape[0]
    ri, ci = _iota2(n, n)
    eye = (ri == ci).astype(F32)
    Xs = [eye - L for L in Ls]
    Ps = list(Ls)
    m = 2
    while m < C:
        Ps = [_mm(P, P) for P in Ps]
        Xs = [X + _mm(X, P) for X, P in zip(Xs, Ps)]
        m *= 2
    return Xs


def _tile_masks(C):
    sh = C.bit_length() - 1
    ri, ci = _iota2(LANES, LANES)
    same = jnp.right_shift(ri, sh) == jnp.right_shift(ci, sh)
    return ri, ci, same, same & (ri >= ci), same & (ri > ci)


def _place_rows(x, r0, n):
    parts = []
    if r0:
        parts.append(jnp.zeros((r0, x.shape[1]), x.dtype))
    parts.append(x)
    if n - r0 - x.shape[0]:
        parts.append(jnp.zeros((n - r0 - x.shape[0], x.shape[1]), x.dtype))
    return jnp.concatenate(parts, axis=0) if len(parts) > 1 else x


def _rwkv_tile(r, lw, k, v, kk, bb, C):
    ri, ci, same, tril, stril = _tile_masks(C)
    lane = lax.broadcasted_iota(jnp.int32, (1, LANES), 1)
    m = [(lane < RWKV_N).astype(F32), (lane >= RWKV_N).astype(F32)]
    trilf = tril.astype(F32)
    n = len(r)
    gc = [_mm_xl(trilf, x) for x in lw]
    at = [-kk[g] * jnp.exp(gc[g] - lw[g]) for g in range(n)]
    rt = [r[g] * jnp.exp(gc[g]) for g in range(n)]
    sc = []
    for g in range(n):
        e_neg = jnp.exp(-gc[g])
        lhs = jnp.concatenate([at[g] * m[0], at[g] * m[1], rt[g] * m[0], rt[g] * m[1]], axis=0)
        sc.append(_mm_nt(lhs, jnp.concatenate([bb[g] * e_neg, k[g] * e_neg], axis=0)))
    gh = [(g, hd) for g in range(n) for hd in range(2)]
    a_blk = [sc[g][LANES * hd:LANES * (hd + 1)] for g, hd in gh]
    r_blk = [sc[g][LANES * (2 + hd):LANES * (3 + hd)] for g, hd in gh]
    A_rb = [jnp.where(tril, x[:, :LANES], 0.0) for x in r_blk]
    A_rk = [jnp.where(tril, x[:, LANES:], 0.0) for x in r_blk]
    Tinv = _unit_lower_inv([jnp.where(stril, -x[:, :LANES], 0.0) for x in a_blk], C)
    akv = [_mm(jnp.where(stril, a_blk[p][:, LANES:], 0.0), v[g]) for p, (g, hd) in enumerate(gh)]
    y = [_mm(Tinv[p], jnp.concatenate([at[g] * m[hd], akv[p]], axis=1)) for p, (g, hd) in enumerate(gh)]
    o0 = [_mm(jnp.concatenate([A_rk[p], A_rb[p]], axis=1), jnp.concatenate([v[g], y[p][:, LANES:]], axis=0))
          for p, (g, hd) in enumerate(gh)]
    out = []
    for g in range(n):
        p0, p1 = 2 * g, 2 * g + 1
        out.append(dict(gc=gc[g], rt=rt[g], m=m, Wa=y[p0][:, :LANES] + y[p1][:, :LANES],
                        U0=y[p0][:, LANES:] * m[0] + y[p1][:, LANES:] * m[1],
                        O0=o0[p0] * m[0] + o0[p1] * m[1],
                        A_rb=jnp.concatenate([A_rb[p0], A_rb[p1]], axis=0), k=k[g], v=v[g], bb=bb[g]))
    return out


def _rwkv_block(i, C, ts, Ss):
    sl = slice(C * i, C * (i + 1))
    sl1 = slice(LANES + C * i, LANES + C * (i + 1))
    r2, c2 = _iota2(LANES, LANES)
    blk = ((r2 >= RWKV_N) == (c2 >= RWKV_N)).astype(F32)
    xs = [_mm_nt(jnp.concatenate([t['Wa'][sl], t['rt'][sl]], axis=0), S) for t, S in zip(ts, Ss)]
    S_new, ax = [], []
    for t, S, x in zip(ts, Ss, xs):
        X = x[:C]
        U = t['U0'][sl] + X
        gl = t['gc'][C * (i + 1) - 1:C * (i + 1), :]
        e_last = jnp.exp(gl - t['gc'][sl])
        S_new.append(S * jnp.exp(gl) + blk * _mm_tn(
            jnp.concatenate([U, t['v'][sl]], axis=0),
            jnp.concatenate([t['bb'][sl] * e_last, t['k'][sl] * e_last], axis=0)))
        ax.append(_mm(jnp.concatenate([t['A_rb'][sl], t['A_rb'][sl1]], axis=0), _place_rows(X, C * i, LANES)))
    o = [t['O0'][sl] + x[C:] + a[:C] * t['m'][0] + a[C:] * t['m'][1] for t, x, a in zip(ts, xs, ax)]
    return o, S_new


def _gdn_tile(q, k, v, g_row, beta_row, C):
    ri, ci, same, tril, stril = _tile_masks(C)
    eye = ri == ci
    lsum = lambda msk, x: jnp.sum(jnp.where(msk, x, 0.0), axis=1, keepdims=True)
    n = len(q)
    pre = []
    for g in range(n):
        g_b = jnp.broadcast_to(g_row[g], (LANES, LANES))
        beta_col = lsum(eye, jnp.broadcast_to(beta_row[g], (LANES, LANES)))
        g_col = lsum(eye, g_b)
        gc_col = lsum(tril, g_b)
        gl_col = lsum(same, g_b)
        gc_row = jnp.sum(jnp.where(same & (ri <= ci), jnp.broadcast_to(g_col, (LANES, LANES)), 0.0),
                         axis=0, keepdims=True)
        decay = jnp.where(tril, jnp.exp(jnp.minimum(gc_col - gc_row, 0.0)), 0.0)
        pre.append(dict(beta=beta_col, gc=gc_col, gl=gl_col, decay=decay, kb=k[g] * beta_col))
    sc = [_mm_nt(jnp.concatenate([pre[g]['kb'], q[g]], axis=0), k[g]) for g in range(n)]
    Tinv = _unit_lower_inv([jnp.where(stril, sc[g][:LANES] * pre[g]['decay'], 0.0) for g in range(n)], C)
    out = []
    for g in range(n):
        p = pre[g]
        e_gc = jnp.exp(p['gc'])
        uw = _mm(Tinv[g], jnp.concatenate([v[g] * p['beta'], p['kb'] * e_gc], axis=1))
        out.append(dict(u=uw[:, :LANES], wk=uw[:, LANES:], qk=sc[g][LANES:] * p['decay'], qg=q[g] * e_gc,
                        kd=k[g] * jnp.exp(p['gl'] - p['gc']), egl=jnp.exp(p['gl'])))
    return out


def _gdn_block(i, C, ts, Ss):
    sl = slice(C * i, C * (i + 1))
    ws = [_mm(jnp.concatenate([t['wk'][sl], t['qg'][sl]], axis=0), S) for t, S in zip(ts, Ss)]
    v_new = [t['u'][sl] - w[:C] for t, w in zip(ts, ws)]
    S_new = [S * t['egl'][C * i:C * i + 1, :] + _mm_tn(t['kd'][sl], vn) for t, S, vn in zip(ts, Ss, v_new)]
    o = [w[C:] + _mm(t['qk'][sl], _place_rows(vn, C * i, LANES)) for t, w, vn in zip(ts, ws, v_new)]
    return o, S_new


def _gla_tile(q_ref, k_ref, lf_ref, v_ref, probs, C, gc_scr, prod_scr):
    n = len(probs)
    nsb = LANES // GLA_SUB
    ri, ci, same, tril, _ = _tile_masks(C)
    sub = GLA_SUB.bit_length() - 1
    rsub, csub = jnp.right_shift(ri, sub), jnp.right_shift(ci, sub)
    trilf = tril.astype(F32)
    get = lambda ref: [_tile(ref, p) for p in probs]
    q, k, v = get(q_ref), get(k_ref), get(v_ref)
    gc = [_mm_xl(trilf, x) for x in get(lf_ref)]
    rk, ck = _iota2(GLA_SUB * LANES, LANES)
    sel = (jnp.right_shift(rk, LANES.bit_length() - 1) == ck).astype(BF16)
    for g, (h, j) in enumerate(probs):
        gc_scr[g] = gc[g]
        for I in range(nsb):
            s0 = GLA_SUB * I
            ksub = k[g][s0:s0 + GLA_SUB]
            gsub = gc[g][s0:s0 + GLA_SUB]
            for ii in range(GLA_SUB):
                i = s0 + ii
                gi = gc_scr[g, i:i + 1, :]
                qi = q_ref[LANES * j + i:LANES * j + i + 1, LANES * h:LANES * (h + 1)]
                prod_scr[g, s0:s0 + GLA_SUB, LANES * ii:LANES * (ii + 1)] = (
                    qi * ksub * jnp.exp(jnp.minimum(gi - gsub, 0.0)))
    d = [_mm(prod_scr[g], sel) for g in range(n)]
    AT = []
    for g in range(n):
        strips = [d[g][:GLA_SUB]] + [pltpu.roll(d[g][GLA_SUB * I:GLA_SUB * (I + 1)], GLA_SUB * I, axis=1)
                                     for I in range(1, nsb)]
        AT.append(jnp.where((rsub == csub) & (ri <= ci), jnp.concatenate(strips, axis=0), 0.0))
    nsub = C // GLA_SUB
    if nsub > 1:
        for g in range(n):
            off = []
            for b in range(LANES // C):
                b0 = C * b
                kblk = k[g][b0:b0 + C]
                gblk = gc[g][b0:b0 + C]
                acc = jnp.zeros((C, LANES), F32)
                for J in range(1, nsub):
                    s0 = b0 + GLA_SUB * J
                    ref = gc_scr[g, s0 - 1:s0, :]
                    khat = kblk * jnp.exp(jnp.minimum(ref - gblk, 0.0))
                    qhat = q[g][s0:s0 + GLA_SUB] * jnp.exp(gc[g][s0:s0 + GLA_SUB] - ref)
                    acc = acc + _mm_nt(khat, _place_rows(qhat, s0, LANES))
                off.append(acc)
            AT[g] = AT[g] + jnp.where(same & (rsub < csub), jnp.concatenate(off, axis=0), 0.0)
    o_intra = [_mm_tn(AT[g], v[g]) for g in range(n)]
    return [dict(gc=gc[g], k=k[g], v=v[g], qg=q[g] * jnp.exp(gc[g]), o_intra=o_intra[g]) for g in range(n)]


def _gla_block(i, C, ts, STs):
    sl = slice(C * i, C * (i + 1))
    o, ST_new = [], []
    for t, ST in zip(ts, STs):
        gl = t['gc'][C * (i + 1) - 1:C * (i + 1), :]
        o.append(t['o_intra'][sl] + _mm_nt(t['qg'][sl], ST))
        ST_new.append(ST * jnp.exp(gl) + _mm_tn(t['v'][sl], t['k'][sl] * jnp.exp(gl - t['gc'][sl])))
    return o, ST_new


def _shifted(ext, hp, s, tm, back):
    return ext[hp - back * s:hp - back * s + tm, :]


def _rwkv_pre_kernel(has_vres, s, *refs):
    if has_vres:
        (x_ref, vf_ref, sh0_ref, nw_ref, mix_ref, wrkv_ref, w0_ref, w1_ref, w2_ref, a0_ref, a1_ref, a2_ref,
         g1_ref, g2_ref, kkw_ref, ka_ref, e_ref, v0_ref, v1_ref, v2_ref,
         r_out, lw_out, k_out, v_out, kk_out, bb_out, g_out, sh_out, ext) = refs
    else:
        (x_ref, sh0_ref, nw_ref, mix_ref, wrkv_ref, w0_ref, w1_ref, w2_ref, a0_ref, a1_ref, a2_ref,
         g1_ref, g2_ref, kkw_ref, ka_ref, e_ref,
         r_out, lw_out, k_out, v_out, kk_out, bb_out, g_out, sh_out, ext) = refs
    t = pl.program_id(1)
    tm, D = x_ref.shape
    hp = ext.shape[0] - tm
    h = _rmsnorm(x_ref[...], nw_ref[...])

    @pl.when(t == 0)
    def _():
        ext[hp - s:hp, :] = sh0_ref[...]

    ext[hp:, :] = h
    d = _shifted(ext, hp, s, tm, 1) - h
    ext[hp - s:hp, :] = h[tm - s:, :]

    @pl.when(t == pl.num_programs(1) - 1)
    def _():
        sh_out[...] = h[tm - s:, :]

    mixed = lambda i: h + d * mix_ref[i:i + 1, :]
    r = _mm(mixed(0), wrkv_ref[0])
    z = w0_ref[...] + _mm(jnp.tanh(_mm(mixed(1), w1_ref[...])), w2_ref[...])
    lw_out[...] = -math.exp(-0.5) * _sigmoid(z)
    k = _mm(mixed(2), wrkv_ref[1])
    xv = mixed(3)
    v = _mm(xv, wrkv_ref[2])
    a = _sigmoid(a0_ref[...] + _mm(_mm(mixed(4), a1_ref[...]), a2_ref[...]))
    if has_vres:
        gate_v = _sigmoid(v0_ref[...] + _mm(_mm(xv, v1_ref[...]), v2_ref[...]))
        v = v + (vf_ref[...] - v) * gate_v
    g_out[...] = _mm(_sigmoid(_mm(mixed(5), g1_ref[...])), g2_ref[...])
    r_out[...] = r
    v_out[...] = v
    kkraw = k * kkw_ref[...]
    e = e_ref[...]
    for c in range(D // LANES):
        sl = slice(c * LANES, (c + 1) * LANES)
        kc = kkraw[:, sl]
        kkn = kc * lax.rsqrt(_head_sum(kc * kc, e) + L2_EPS)
        kk_out[:, sl] = kkn
        bb_out[:, sl] = kkn * a[:, sl]
    k_out[...] = k * (1.0 + (a - 1.0) * ka_ref[...])


def _gdn_pre_kernel(s, nheads, x_ref, c0_ref, nw_ref, wqkv_ref, wgate_ref, wbg_ref, cw_ref, alog_ref, dtb_ref,
                    q_out, k_out, v_out, gate_out, bg_out, cnew_out, ext):
    t = pl.program_id(1)
    tm, D = x_ref.shape
    hp = ext.shape[0] - tm
    nh = cw_ref.shape[0] - 1
    h = _rmsnorm(x_ref[...], nw_ref[...])

    @pl.when(t == 0)
    def _():
        ext[hp - nh * s:hp, :] = c0_ref[...]

    ext[hp:, :] = _mm(h, wqkv_ref[...])
    y = _shifted(ext, hp, s, tm, 0) * cw_ref[nh:nh + 1, :]
    for j in range(nh):
        y = y + _shifted(ext, hp, s, tm, nh - j) * cw_ref[j:j + 1, :]
    hist = ext[hp + tm - nh * s:hp + tm, :]

    @pl.when(t == pl.num_programs(1) - 1)
    def _():
        cnew_out[...] = hist

    ext[hp - nh * s:hp, :] = hist
    y = _silu(y)
    qk_w = D
    for c in range(qk_w // LANES):
        sl = slice(c * LANES, (c + 1) * LANES)
        qc = y[:, sl]
        q_out[:, sl] = qc * (lax.rsqrt(jnp.sum(qc * qc, axis=-1, keepdims=True) + L2_EPS) * LANES ** -0.5)
        sl2 = slice(qk_w + c * LANES, qk_w + (c + 1) * LANES)
        kc = y[:, sl2]
        k_out[:, sl] = kc * lax.rsqrt(jnp.sum(kc * kc, axis=-1, keepdims=True) + L2_EPS)
    v_out[...] = y[:, 2 * qk_w:]
    gate_out[...] = _mm(h, wgate_ref[...])
    rest = _mm(h, wbg_ref[...])
    lane = lax.broadcasted_iota(jnp.int32, rest.shape, 1)
    bg_out[...] = jnp.where(lane < nheads, _sigmoid(rest),
                            -jnp.exp(alog_ref[...]) * _softplus(rest + dtb_ref[...]))


def _hgrn_pre_kernel(layer, x_ref, nw_ref, win_ref, lb_ref, q_out, k_out, lf_out, v_out, gate_out):
    D = x_ref.shape[1]
    h = _rmsnorm(x_ref[...], nw_ref[...])
    lbp = lb_ref[...]
    ex = jnp.exp(lbp - jnp.max(lbp, axis=0, keepdims=True))
    soft = ex / jnp.sum(ex, axis=0, keepdims=True)
    row = lax.broadcasted_iota(jnp.int32, soft.shape, 0)
    lb = jnp.sum(jnp.where((row >= 1) & (row <= layer), soft, 0.0), axis=0, keepdims=True)
    p = _mm(h, win_ref[...])
    q_out[...] = _silu(p[:, :D])
    f = lb + (1.0 - lb) * _sigmoid(p[:, D:2 * D])
    k_out[...] = 1.0 - f
    lf_out[...] = jnp.log(f)
    v_out[...] = p[:, 2 * D:3 * D]
    gate_out[...] = p[:, 3 * D:]


def _out_proj_kernel(y_ref, x_ref, wo_ref, pw_ref, x_out):
    x_out[...] = x_ref[...] + _rmsnorm(_mm(y_ref[...], wo_ref[...]), pw_ref[...])


def _ffn_kernel(s, x_ref, c0_ref, nw_ref, wup_ref, cw_ref, wdn_ref, pw_ref, x_out, cnew_out, ext):
    t = pl.program_id(1)
    tm, D = x_ref.shape
    hp = ext.shape[0] - tm
    nh = cw_ref.shape[0] - 1
    dff = wdn_ref.shape[0]
    x = x_ref[...]
    h = _rmsnorm(x, nw_ref[...])

    @pl.when(t == 0)
    def _():
        ext[hp - nh * s:hp, :] = c0_ref[...]

    ext[hp:, :] = _mm(h, wup_ref[...])
    y = _shifted(ext, hp, s, tm, 0) * cw_ref[nh:nh + 1, :]
    for j in range(nh):
        y = y + _shifted(ext, hp, s, tm, nh - j) * cw_ref[j:j + 1, :]
    hist = ext[hp + tm - nh * s:hp + tm, :]

    @pl.when(t == pl.num_programs(1) - 1)
    def _():
        cnew_out[...] = hist

    ext[hp - nh * s:hp, :] = hist
    act = _silu(y[:, dff:]) * y[:, :dff]
    x_out[...] = x + _rmsnorm(_mm(act, wdn_ref[...]), pw_ref[...])


def _const_spec(a):
    nd = a.ndim
    return pl.BlockSpec(a.shape, lambda g, t: (0,) * nd, pipeline_mode=pl.Buffered(1))


def _row_call(body, name, tm, tiled_ins, group_ins, const_ins, tiled_out_widths, group_out_shapes, scratch):
    G, R, _ = tiled_ins[0].shape
    assert R % tm == 0
    tile_spec = lambda c: pl.BlockSpec((None, tm, c), lambda g, t: (g, t, 0))
    group_spec = lambda n, c: pl.BlockSpec((None, n, c), lambda g, t: (g, 0, 0))
    group_in_spec = lambda n, c: pl.BlockSpec((None, n, c), lambda g, t: (g, 0, 0), pipeline_mode=pl.Buffered(1))
    in_specs = ([tile_spec(a.shape[2]) for a in tiled_ins] + [group_in_spec(*a.shape[1:]) for a in group_ins]
                + [_const_spec(a) for a in const_ins])
    out_specs = [tile_spec(c) for c in tiled_out_widths] + [group_spec(n, c) for n, c in group_out_shapes]
    out_shape = ([jax.ShapeDtypeStruct((G, R, c), F32) for c in tiled_out_widths]
                 + [jax.ShapeDtypeStruct((G, n, c), F32) for n, c in group_out_shapes])
    return pl.pallas_call(
        body, name=name, grid=(G, R // tm), in_specs=in_specs, out_specs=out_specs, out_shape=out_shape,
        scratch_shapes=scratch,
        compiler_params=pltpu.CompilerParams(dimension_semantics=("parallel", "arbitrary"),
                                             vmem_limit_bytes=VMEM_LIMIT),
    )(*tiled_ins, *group_ins, *const_ins)


def _hist_pad(n):
    return -(-n // SUBLANES) * SUBLANES


def _tile(ref, p):
    h, j = p
    return ref[LANES * j:LANES * (j + 1), LANES * h:LANES * (h + 1)]


def _scan_states(chained, C, probs, tiles, s0_ref, s_out, scr, load, store, block_fn):
    nb = LANES // C
    tt = pl.program_id(2)
    heads = sorted({h for h, _ in probs})
    nt = len(probs) // len(heads)
    outs = {}
    if chained:
        @pl.when(tt == 0)
        def _():
            for h in heads:
                scr[h] = load(s0_ref[h])

        Ss = [scr[h] for h in heads]
        for j in range(nt):
            sel = [tiles[probs.index((h, j))] for h in heads]
            for i in range(nb):
                os, Ss = block_fn(i, sel, Ss)
                for h, o in zip(heads, os):
                    outs[(h, j, i)] = o
        for h, S in zip(heads, Ss):
            scr[h] = S

        @pl.when(tt == pl.num_programs(2) - 1)
        def _():
            for h, S in zip(heads, Ss):
                s_out[h] = store(S)
    else:
        for i in range(nb):
            Ss = [load(s0_ref[j * nb + i, h]) for h, j in probs]
            os, Ss = block_fn(i, tiles, Ss)
            for (h, j), o, S in zip(probs, os, Ss):
                outs[(h, j, i)] = o
                s_out[j * nb + i, h] = store(S)
    return [jnp.concatenate([outs[(h, j, i)] for i in range(nb)], axis=0) if nb > 1 else outs[(h, j, 0)]
            for h, j in probs]


def _scan_probs(ref):
    return [(h, j) for h in range(ref.shape[1] // LANES) for j in range(ref.shape[0] // LANES)]


def _rwkv_scan_kernel(C, chained, r_ref, lw_ref, k_ref, v_ref, kk_ref, bb_ref, g_ref, rk_ref, lnw_ref, lnb_ref,
                      e_ref, s0_ref, y_out, s_out, scr):
    N = RWKV_N
    e = e_ref[...]
    ident = lambda a: a
    probs = _scan_probs(r_ref)
    get = lambda ref: [_tile(ref, p) for p in probs]
    r, k, v = get(r_ref), get(k_ref), get(v_ref)
    tiles = _rwkv_tile(r, get(lw_ref), k, v, get(kk_ref), get(bb_ref), C)
    O = _scan_states(chained, C, probs, tiles, s0_ref, s_out, scr, ident, ident,
                     lambda i, ts, Ss: _rwkv_block(i, C, ts, Ss))
    np_ = len(probs)
    lanes = [slice(LANES * h, LANES * (h + 1)) for h, _ in probs]
    s1 = _head_sum(jnp.concatenate(O + [r[n] * k[n] * rk_ref[:, lanes[n]] for n in range(np_)], axis=0), e)
    d = [O[n] - s1[LANES * n:LANES * (n + 1)] * (1.0 / N) for n in range(np_)]
    s2 = _head_sum(jnp.concatenate([x * x for x in d], axis=0), e)
    for n, (h, j) in enumerate(probs):
        var = s2[LANES * n:LANES * (n + 1)] * (1.0 / N)
        on = d[n] * lax.rsqrt(var + RWKV_LNX_EPS) * lnw_ref[:, lanes[n]] + lnb_ref[:, lanes[n]]
        bonus = s1[LANES * (np_ + n):LANES * (np_ + n + 1)] * v[n]
        y_out[LANES * j:LANES * (j + 1), lanes[n]] = (on + bonus) * _tile(g_ref, (h, j))


def _gdn_scan_kernel(C, chained, q_ref, k_ref, v_ref, gate_ref, g_ref, beta_ref, nw_ref, s0_ref, y_out, s_out,
                     scr):
    ident = lambda a: a
    probs = _scan_probs(q_ref)
    get = lambda ref: [_tile(ref, p) for p in probs]
    tiles = _gdn_tile(get(q_ref), get(k_ref), get(v_ref), [g_ref[h, j:j + 1, :] for h, j in probs],
                      [beta_ref[h, j:j + 1, :] for h, j in probs], C)
    O = _scan_states(chained, C, probs, tiles, s0_ref, s_out, scr, ident, ident,
                     lambda i, ts, Ss: _gdn_block(i, C, ts, Ss))
    for n, (h, j) in enumerate(probs):
        y_out[LANES * j:LANES * (j + 1), LANES * h:LANES * (h + 1)] = (
            _rmsnorm(O[n], nw_ref[...]) * _silu(_tile(gate_ref, (h, j))))


def _gla_scan_kernel(C, chained, q_ref, k_ref, lf_ref, v_ref, gate_ref, nw_ref, s0_ref, y_out, s_out, scr, gc_scr,
                     prod_scr):
    tr = lambda a: a.T
    probs = _scan_probs(q_ref)
    tiles = _gla_tile(q_ref, k_ref, lf_ref, v_ref, probs, C, gc_scr, prod_scr)
    O = _scan_states(chained, C, probs, tiles, s0_ref, s_out, scr, tr, tr,
                     lambda i, ts, Ss: _gla_block(i, C, ts, Ss))
    for n, (h, j) in enumerate(probs):
        y_out[LANES * j:LANES * (j + 1), LANES * h:LANES * (h + 1)] = (
            _rmsnorm(O[n], nw_ref[...]) * _silu(_tile(gate_ref, (h, j))))


def _scan_call(body, name, C, chained, tiles, extra_ins, extra_specs, s0, nheads, extra_scratch=()):
    NB, RB, D = tiles[0].shape
    rows = min(SCAN_ROWS, RB)
    hg = SCAN_HEADS
    assert RB % rows == 0 and rows % LANES == 0 and nheads % hg == 0
    tile_spec = pl.BlockSpec((None, rows, hg * LANES), lambda b, h, t: (b, t, h))
    if chained:
        s_spec = pl.BlockSpec((None, hg, LANES, LANES), lambda b, h, t: (b, h, 0, 0))
    else:
        nt = RB // rows
        s_spec = pl.BlockSpec((rows // C, hg, LANES, LANES), lambda b, h, t: (b * nt + t, h, 0, 0))
    return pl.pallas_call(
        functools.partial(body, C, chained), name=name, grid=(NB, nheads // hg, RB // rows),
        in_specs=[tile_spec] * len(tiles) + list(extra_specs) + [s_spec],
        out_specs=[tile_spec, s_spec],
        out_shape=[jax.ShapeDtypeStruct((NB, RB, D), F32), jax.ShapeDtypeStruct(s0.shape, F32)],
        scratch_shapes=[pltpu.VMEM((hg, LANES, LANES), F32)] + list(extra_scratch),
        compiler_params=pltpu.CompilerParams(dimension_semantics=("parallel", "parallel", "arbitrary"),
                                             vmem_limit_bytes=VMEM_LIMIT),
    )(*tiles, *extra_ins, s0)


class _Group:
    def __init__(self, B, T, time_major):
        self.B, self.T, self.tm_major = B, T, time_major
        if time_major:
            self.G, self.R, self.s = 1, B * T, B
            self.C = -(-T // BF16_ROWS) * BF16_ROWS
            assert LANES % self.C == 0 and (B * self.C) % LANES == 0
        else:
            self.G, self.R, self.s = B, T, 1
            self.C = SCAN_CHUNK
            assert T % LANES == 0
        self.chained = not time_major
        self.tile = min(ROW_TILE, self.R)
        self.ffn_tile = self.s if time_major else self.tile

    def to_rows(self, x):
        if self.tm_major:
            return jnp.swapaxes(x, 0, 1).reshape(1, self.R, x.shape[-1])
        return x

    def hist_to_rows(self, h):
        if self.tm_major:
            return jnp.swapaxes(h, 0, 1).reshape(1, -1, h.shape[-1])
        return h

    def hist_from_rows(self, h, n):
        if self.tm_major:
            return jnp.swapaxes(h.reshape(n, self.B, h.shape[-1]), 0, 1)
        return h

    def to_scan(self, a):
        if not self.tm_major:
            return a
        a = jnp.swapaxes(a.reshape(self.T, self.B, a.shape[-1]), 0, 1)
        a = jnp.pad(a, ((0, 0), (0, self.C - self.T), (0, 0)))
        return a.reshape(1, self.B * self.C, a.shape[-1])

    def from_scan(self, a):
        if not self.tm_major:
            return a
        return self.to_rows(a.reshape(self.B, self.C, a.shape[-1])[:, :self.T])


def _rwkv_layer(grp, x, shift0, S0, v_first, P, j):
    D = x.shape[-1]
    s, tm = grp.s, grp.tile
    N = RWKV_N
    has_vres = v_first is not None
    r2, c2 = _iota2(LANES, LANES)
    e = ((r2 // N) == (c2 // N)).astype(BF16)
    row = lambda a: a.reshape(1, -1)
    tiled = [x] + ([v_first] if has_vres else [])
    consts = [row(P['norm_mix_pre_i']), P['rwkv_mix'][j], P['rwkv_w_rkv'][j].astype(BF16),
              row(P['rwkv_w0'][j]), P['rwkv_w1'][j].astype(BF16), P['rwkv_w2'][j].astype(BF16),
              row(P['rwkv_a0'][j]), P['rwkv_a1'][j].astype(BF16), P['rwkv_a2'][j].astype(BF16),
              P['rwkv_g1'][j].astype(BF16), P['rwkv_g2'][j].astype(BF16),
              row(P['rwkv_k_k'][j]), row(P['rwkv_k_a'][j]), e]
    if has_vres:
        consts += [row(P['rwkv_v0'][j - 1]), P['rwkv_v1'][j - 1].astype(BF16), P['rwkv_v2'][j - 1].astype(BF16)]
    r, lw, k, v, kk, bb, g, shift = _row_call(
        functools.partial(_rwkv_pre_kernel, has_vres, s), f"rwkv_pre_{j}", tm, tiled, [shift0], consts,
        [D] * 7, [(s, D)], [pltpu.VMEM((_hist_pad(s) + tm, D), F32)])
    if not has_vres:
        v_first = v
    tiles = [grp.to_scan(a) for a in (r, lw, k, v, kk, bb, g)]
    vec_spec = pl.BlockSpec((1, SCAN_HEADS * LANES), lambda b, h, t: (0, h))
    e_spec = pl.BlockSpec((LANES, LANES), lambda b, h, t: (0, 0))
    Bq, H = S0.shape[:2]
    S0p = S0.reshape(Bq, H // 2, 2, N, N)
    zero = jnp.zeros_like(S0p[:, :, 0])
    S0bd = jnp.concatenate([jnp.concatenate([S0p[:, :, 0], zero], axis=-1),
                            jnp.concatenate([zero, S0p[:, :, 1]], axis=-1)], axis=-2)
    y, Sbd = _scan_call(_rwkv_scan_kernel, f"rwkv_scan_{j}", grp.C, grp.chained, tiles,
                        [row(P['rwkv_r_k'][j]), row(P['rwkv_lnx_w'][j]), row(P['rwkv_lnx_b'][j]), e],
                        [vec_spec] * 3 + [e_spec], S0bd, D // LANES)
    S = jnp.stack([Sbd[:, :, :N, :N], Sbd[:, :, N:, N:]], axis=2).reshape(S0.shape)
    return grp.from_scan(y), P['rwkv_w_o'][j], shift, S, v_first


def _gdn_layer(grp, x, conv0, S0, P, j):
    D = x.shape[-1]
    s, tm = grp.s, grp.tile
    w_in = P['gdn_w_in'][j]
    cw = P['gdn_conv_w'][j]
    nh, cdim = cw.shape[0] - 1, cw.shape[1]
    H = S0.shape[1]
    assert grp.T >= nh
    row = lambda a: a.reshape(1, -1)
    lane_pad = lambda a: jnp.pad(a, ((0, 0), (0, LANES - a.shape[1])))
    zeros = jnp.zeros((1, H), F32)
    consts = [row(P['norm_mix_pre_i']), w_in[:, :cdim].astype(BF16), w_in[:, cdim:cdim + D].astype(BF16),
              lane_pad(w_in[:, cdim + D:]).astype(BF16), cw,
              lane_pad(jnp.concatenate([zeros, row(P['gdn_a_log'][j])], axis=1)),
              lane_pad(jnp.concatenate([zeros, row(P['gdn_dt_bias'][j])], axis=1))]
    q, k, v, gate, bg, conv_new = _row_call(
        functools.partial(_gdn_pre_kernel, s, H), f"gdn_pre_{j}", tm, [x], [conv0], consts,
        [D, D, D, D, LANES], [(nh * s, cdim)], [pltpu.VMEM((_hist_pad(nh * s) + tm, cdim), F32)])
    tiles = [grp.to_scan(a) for a in (q, k, v, gate)]
    NB, RB, _ = tiles[0].shape
    rows = min(SCAN_ROWS, RB)
    nt = RB // rows
    bg = grp.to_scan(bg[:, :, :2 * H]).reshape(NB * RB, 2 * H).T.reshape(2 * H, NB * nt, rows // LANES, LANES)
    hg = SCAN_HEADS
    g_spec = pl.BlockSpec((hg, None, rows // LANES, LANES), lambda b, h, t: (H // hg + h, b * nt + t, 0, 0))
    beta_spec = pl.BlockSpec((hg, None, rows // LANES, LANES), lambda b, h, t: (h, b * nt + t, 0, 0))
    vec_spec = pl.BlockSpec((1, LANES), lambda b, h, t: (0, 0))
    y, S = _scan_call(_gdn_scan_kernel, f"gdn_scan_{j}", grp.C, grp.chained, tiles,
                      [bg, bg, row(P['gdn_norm_w'][j])], [g_spec, beta_spec, vec_spec], S0, H)
    return grp.from_scan(y), P['gdn_w_o'][j], conv_new, S


def _hgrn_layer(grp, x, S0, P, i, j):
    D = x.shape[-1]
    tm = grp.tile
    H = S0.shape[1]
    row = lambda a: a.reshape(1, -1)
    consts = [row(P['norm_mix_pre_i']), P['hgrn_w_in'][j].astype(BF16), P['hgrn_lb']]
    q, k, lf, v, gate = _row_call(functools.partial(_hgrn_pre_kernel, i), f"hgrn_pre_{j}", tm, [x], [], consts,
                                  [D] * 5, [], [])
    tiles = [grp.to_scan(a) for a in (q, k, lf, v, gate)]
    nprob = SCAN_HEADS * (min(SCAN_ROWS, tiles[0].shape[1]) // LANES)
    vec_spec = pl.BlockSpec((1, LANES), lambda b, h, t: (0, 0))
    y, S = _scan_call(_gla_scan_kernel, f"hgrn_scan_{j}", grp.C, grp.chained, tiles, [row(P['hgrn_norm_w'][j])],
                      [vec_spec], S0, H,
                      extra_scratch=[pltpu.VMEM((nprob, LANES, LANES), F32),
                                     pltpu.VMEM((nprob, LANES, GLA_SUB * LANES), F32)])
    return grp.from_scan(y), P['hgrn_w_o'][j], S


def _trunk(grp, x, shift0, wkv0, gconv0, gS0, hS0, fconv0, P):
    D = x.shape[-1]
    depth = P['norm_mix_pre'].shape[0]
    row = lambda a: a.reshape(1, -1)
    x = grp.to_rows(x)
    v_first = None
    shift, wkv, gconv, gS, hS, fconv = [], [], [], [], [], []
    for i in range(depth):
        kind, j = i % 3, i // 3
        P = dict(P, norm_mix_pre_i=P['norm_mix_pre'][i])
        if kind == 0:
            y, w_o, s_shift, s_wkv, v_first = _rwkv_layer(grp, x, grp.hist_to_rows(shift0[j][:, None]), wkv0[j],
                                                         v_first, P, j)
            shift.append(grp.hist_from_rows(s_shift, 1)[:, 0])
            wkv.append(s_wkv)
        elif kind == 1:
            y, w_o, c_new, s_new = _gdn_layer(grp, x, grp.hist_to_rows(gconv0[j]), gS0[j], P, j)
            gconv.append(grp.hist_from_rows(c_new, gconv0.shape[2]))
            gS.append(s_new)
        else:
            y, w_o, s_new = _hgrn_layer(grp, x, hS0[j], P, i, j)
            hS.append(s_new)
        (x,) = _row_call(_out_proj_kernel, f"out_proj_{i}", grp.tile, [y, x], [],
                         [w_o.astype(BF16), row(P['norm_mix_post'][i])], [D], [], [])
        nh = P['ffn_conv_w'].shape[1] - 1
        dff2 = P['ffn_w_up'].shape[2]
        x, c_new = _row_call(
            functools.partial(_ffn_kernel, grp.s), f"ffn_{i}", grp.ffn_tile, [x], [grp.hist_to_rows(fconv0[i])],
            [row(P['norm_ffn_pre'][i]), P['ffn_w_up'][i].astype(BF16), P['ffn_conv_w'][i],
             P['ffn_w_down'][i].astype(BF16), row(P['norm_ffn_post'][i])],
            [D], [(nh * grp.s, dff2)], [pltpu.VMEM((_hist_pad(nh * grp.s) + grp.ffn_tile, dff2), F32)])
        fconv.append(grp.hist_from_rows(c_new, nh))
    y = x.reshape(grp.T, grp.B, D).swapaxes(0, 1) if grp.tm_major else x
    return y, (jnp.stack(shift), jnp.stack(wkv), jnp.stack(gconv), jnp.stack(gS), jnp.stack(hS), jnp.stack(fconv))


def kernel(x_prompt, x_sample, state_rwkv_shift, state_rwkv_wkv, state_gdn_conv, state_gdn_S, state_hgrn_S, state_ffn_conv, norm_mix_pre, norm_mix_post, norm_ffn_pre, norm_ffn_post, rwkv_mix, rwkv_w_rkv, rwkv_w0, rwkv_w1, rwkv_w2, rwkv_a0, rwkv_a1, rwkv_a2, rwkv_v0, rwkv_v1, rwkv_v2, rwkv_g1, rwkv_g2, rwkv_k_k, rwkv_k_a, rwkv_r_k, rwkv_lnx_w, rwkv_lnx_b, rwkv_w_o, gdn_w_in, gdn_conv_w, gdn_a_log, gdn_dt_bias, gdn_norm_w, gdn_w_o, hgrn_w_in, hgrn_lb, hgrn_norm_w, hgrn_w_o, ffn_w_up, ffn_conv_w, ffn_w_down):
    P = dict(norm_mix_pre=norm_mix_pre, norm_mix_post=norm_mix_post, norm_ffn_pre=norm_ffn_pre,
             norm_ffn_post=norm_ffn_post, rwkv_mix=rwkv_mix, rwkv_w_rkv=rwkv_w_rkv, rwkv_w0=rwkv_w0,
             rwkv_w1=rwkv_w1, rwkv_w2=rwkv_w2, rwkv_a0=rwkv_a0, rwkv_a1=rwkv_a1, rwkv_a2=rwkv_a2,
             rwkv_v0=rwkv_v0, rwkv_v1=rwkv_v1, rwkv_v2=rwkv_v2, rwkv_g1=rwkv_g1, rwkv_g2=rwkv_g2,
             rwkv_k_k=rwkv_k_k, rwkv_k_a=rwkv_k_a, rwkv_r_k=rwkv_r_k, rwkv_lnx_w=rwkv_lnx_w,
             rwkv_lnx_b=rwkv_lnx_b, rwkv_w_o=rwkv_w_o, gdn_w_in=gdn_w_in, gdn_conv_w=gdn_conv_w,
             gdn_a_log=gdn_a_log, gdn_dt_bias=gdn_dt_bias, gdn_norm_w=gdn_norm_w, gdn_w_o=gdn_w_o,
             hgrn_w_in=hgrn_w_in, hgrn_lb=hgrn_lb, hgrn_norm_w=hgrn_norm_w, hgrn_w_o=hgrn_w_o,
             ffn_w_up=ffn_w_up, ffn_conv_w=ffn_conv_w, ffn_w_down=ffn_w_down)
    Bp, Tp, _ = x_prompt.shape
    Bs, Ts, _ = x_sample.shape
    zero_like = lambda st: jnp.zeros((st.shape[0], Bp) + st.shape[2:], st.dtype)
    y_p, (p_shift, p_wkv, p_gconv, p_gS, p_hS, p_fconv) = _trunk(
        _Group(Bp, Tp, False), x_prompt, zero_like(state_rwkv_shift), zero_like(state_rwkv_wkv),
        zero_like(state_gdn_conv), zero_like(state_gdn_S), zero_like(state_hgrn_S), zero_like(state_ffn_conv), P)
    y_s, (s_shift, s_wkv, s_gconv, s_gS, s_hS, s_fconv) = _trunk(
        _Group(Bs, Ts, True), x_sample, state_rwkv_shift, state_rwkv_wkv, state_gdn_conv, state_gdn_S,
        state_hgrn_S, state_ffn_conv, P)
    return (y_p, y_s, p_shift, s_shift, p_wkv, s_wkv, p_gconv, s_gconv,
            p_gS, s_gS, p_hS, s_hS, p_fconv, s_fconv)
```

```python
import functools
import math

import jax
import jax.numpy as jnp
from jax import lax
from jax.experimental import pallas as pl
from jax.experimental.pallas import tpu as pltpu

F32 = jnp.float32
BF16 = jnp.bfloat16

NORM_EPS = 1e-6
L2_EPS = 1e-6
RWKV_LNX_EPS = 64e-5
RWKV_N = 64
LANES = 128
SUBLANES = 8
BF16_ROWS = 16
VMEM_LIMIT = 56 * 1024 * 1024
ROW_TILE = 256
FFN_TILE = 256
SCAN_ROWS = 256
SCAN_HEADS = 4
SCAN_CHUNK = 64
GLA_SUB = 8


def _dg(a, b, ca, cb):
    return lax.dot_general(a, b, (((ca,), (cb,)), ((), ())), preferred_element_type=F32)


def _mm(a, b):
    return _dg(a.astype(BF16), b.astype(BF16), 1, 0)


def _mm_nt(a, b):
    return _dg(a.astype(BF16), b.astype(BF16), 1, 1)


def _mm_tn(a, b):
    return _dg(a.astype(BF16), b.astype(BF16), 0, 0)


def _split3(x):
    hi = x.astype(BF16)
    r1 = x - hi.astype(F32)
    mid = r1.astype(BF16)
    lo = (r1 - mid.astype(F32)).astype(BF16)
    return hi, mid, lo


def _mm_xl(m, x):
    h, mi, lo = _split3(x)
    m = m.astype(BF16)
    return _dg(m, h, 1, 0) + (_dg(m, mi, 1, 0) + _dg(m, lo, 1, 0))


def _mm_xr(x, m):
    h, mi, lo = _split3(x)
    m = m.astype(BF16)
    return _dg(h, m, 1, 0) + (_dg(mi, m, 1, 0) + _dg(lo, m, 1, 0))


def _iota2(n, m):
    return (lax.broadcasted_iota(jnp.int32, (n, m), 0), lax.broadcasted_iota(jnp.int32, (n, m), 1))


def _sigmoid(x):
    return 1.0 / (1.0 + jnp.exp(-x))


def _silu(x):
    return x * _sigmoid(x)


def _softplus(x):
    return jnp.maximum(x, 0.0) + jnp.log(1.0 + jnp.exp(-jnp.abs(x)))


def _rmsnorm(x, w):
    return x * lax.rsqrt(jnp.mean(x * x, axis=-1, keepdims=True) + NORM_EPS) * w


def _head_sum(x, e):
    return _mm_xr(x, e)


def _unit_lower_inv(Ls, C):
    n = Ls[0].shape[0]
    ri, ci = _iota2(n, n)
    eye = (ri == ci).astype(F32)
    Xs = [eye - L for L in Ls]
    Ps = list(Ls)
    m = 2
    while m < C:
        Ps = [_mm(P, P) for P in Ps]
        Xs = [X + _mm(X, P) for X, P in zip(Xs, Ps)]
        m *= 2
    return Xs


def _tile_masks(C):
    sh = C.bit_length() - 1
    ri, ci = _iota2(LANES, LANES)
    same = jnp.right_shift(ri, sh) == jnp.right_shift(ci, sh)
    return ri, ci, same, same & (ri >= ci), same & (ri > ci)


def _place_rows(x, r0, n):
    parts = []
    if r0:
        parts.append(jnp.zeros((r0, x.shape[1]), x.dtype))
    parts.append(x)
    if n - r0 - x.shape[0]:
        parts.append(jnp.zeros((n - r0 - x.shape[0], x.shape[1]), x.dtype))
    return jnp.concatenate(parts, axis=0) if len(parts) > 1 else x


def _rwkv_tile(r, lw, k, v, kk, bb, C):
    ri, ci, same, tril, stril = _tile_masks(C)
    lane = lax.broadcasted_iota(jnp.int32, (1, LANES), 1)
    m = [(lane < RWKV_N).astype(F32), (lane >= RWKV_N).astype(F32)]
    trilf = tril.astype(F32)
    n = len(r)
    gc = [_mm_xl(trilf, x) for x in lw]
    at = [-kk[g] * jnp.exp(gc[g] - lw[g]) for g in range(n)]
    rt = [r[g] * jnp.exp(gc[g]) for g in range(n)]
    sc = []
    for g in range(n):
        e_neg = jnp.exp(-gc[g])
        lhs = jnp.concatenate([at[g] * m[0], at[g] * m[1], rt[g] * m[0], rt[g] * m[1]], axis=0)
        sc.append(_mm_nt(lhs, jnp.concatenate([bb[g] * e_neg, k[g] * e_neg], axis=0)))
    gh = [(g, hd) for g in range(n) for hd in range(2)]
    a_blk = [sc[g][LANES * hd:LANES * (hd + 1)] for g, hd in gh]
    r_blk = [sc[g][LANES * (2 + hd):LANES * (3 + hd)] for g, hd in gh]
    A_rb = [jnp.where(tril, x[:, :LANES], 0.0) for x in r_blk]
    A_rk = [jnp.where(tril, x[:, LANES:], 0.0) for x in r_blk]
    Tinv = _unit_lower_inv([jnp.where(stril, -x[:, :LANES], 0.0) for x in a_blk], C)
    akv = [_mm(jnp.where(stril, a_blk[p][:, LANES:], 0.0), v[g]) for p, (g, hd) in enumerate(gh)]
    y = [_mm(Tinv[p], jnp.concatenate([at[g] * m[hd], akv[p]], axis=1)) for p, (g, hd) in enumerate(gh)]
    o0 = [_mm(jnp.concatenate([A_rk[p], A_rb[p]], axis=1), jnp.concatenate([v[g], y[p][:, LANES:]], axis=0))
          for p, (g, hd) in enumerate(gh)]
    out = []
    for g in range(n):
        p0, p1 = 2 * g, 2 * g + 1
        out.append(dict(gc=gc[g], rt=rt[g], m=m, Wa=y[p0][:, :LANES] + y[p1][:, :LANES],
                        U0=y[p0][:, LANES:] * m[0] + y[p1][:, LANES:] * m[1],
                        O0=o0[p0] * m[0] + o0[p1] * m[1],
                        A_rb=jnp.concatenate([A_rb[p0], A_rb[p1]], axis=0), k=k[g], v=v[g], bb=bb[g]))
    return out


def _rwkv_block(i, C, ts, Ss):
    sl = slice(C * i, C * (i + 1))
    sl1 = slice(LANES + C * i, LANES + C * (i + 1))
    r2, c2 = _iota2(LANES, LANES)
    blk = ((r2 >= RWKV_N) == (c2 >= RWKV_N)).astype(F32)
    xs = [_mm_nt(jnp.concatenate([t['Wa'][sl], t['rt'][sl]], axis=0), S) for t, S in zip(ts, Ss)]
    S_new, ax = [], []
    for t, S, x in zip(ts, Ss, xs):
        X = x[:C]
        U = t['U0'][sl] + X
        gl = t['gc'][C * (i + 1) - 1:C * (i + 1), :]
        e_last = jnp.exp(gl - t['gc'][sl])
        S_new.append(S * jnp.exp(gl) + blk * _mm_tn(
            jnp.concatenate([U, t['v'][sl]], axis=0),
            jnp.concatenate([t['bb'][sl] * e_last, t['k'][sl] * e_last], axis=0)))
        ax.append(_mm(jnp.concatenate([t['A_rb'][sl], t['A_rb'][sl1]], axis=0), _place_rows(X, C * i, LANES)))
    o = [t['O0'][sl] + x[C:] + a[:C] * t['m'][0] + a[C:] * t['m'][1] for t, x, a in zip(ts, xs, ax)]
    return o, S_new


def _gdn_tile(q, k, v, g_row, beta_row, C):
    ri, ci, same, tril, stril = _tile_masks(C)
    eye = ri == ci
    lsum = lambda msk, x: jnp.sum(jnp.where(msk, x, 0.0), axis=1, keepdims=True)
    n = len(q)
    pre = []
    for g in range(n):
        g_b = jnp.broadcast_to(g_row[g], (LANES, LANES))
        beta_col = lsum(eye, jnp.broadcast_to(beta_row[g], (LANES, LANES)))
        g_col = lsum(eye, g_b)
        gc_col = lsum(tril, g_b)
        gl_col = lsum(same, g_b)
        gc_row = jnp.sum(jnp.where(same & (ri <= ci), jnp.broadcast_to(g_col, (LANES, LANES)), 0.0),
                         axis=0, keepdims=True)
        decay = jnp.where(tril, jnp.exp(jnp.minimum(gc_col - gc_row, 0.0)), 0.0)
        pre.append(dict(beta=beta_col, gc=gc_col, gl=gl_col, decay=decay, kb=k[g] * beta_col))
    sc = [_mm_nt(jnp.concatenate([pre[g]['kb'], q[g]], axis=0), k[g]) for g in range(n)]
    Tinv = _unit_lower_inv([jnp.where(stril, sc[g][:LANES] * pre[g]['decay'], 0.0) for g in range(n)], C)
    out = []
    for g in range(n):
        p = pre[g]
        e_gc = jnp.exp(p['gc'])
        uw = _mm(Tinv[g], jnp.concatenate([v[g] * p['beta'], p['kb'] * e_gc], axis=1))
        out.append(dict(u=uw[:, :LANES], wk=uw[:, LANES:], qk=sc[g][LANES:] * p['decay'], qg=q[g] * e_gc,
                        kd=k[g] * jnp.exp(p['gl'] - p['gc']), egl=jnp.exp(p['gl'])))
    return out


def _gdn_block(i, C, ts, Ss):
    sl = slice(C * i, C * (i + 1))
    ws = [_mm(jnp.concatenate([t['wk'][sl], t['qg'][sl]], axis=0), S) for t, S in zip(ts, Ss)]
    v_new = [t['u'][sl] - w[:C] for t, w in zip(ts, ws)]
    S_new = [S * t['egl'][C * i:C * i + 1, :] + _mm_tn(t['kd'][sl], vn) for t, S, vn in zip(ts, Ss, v_new)]
    o = [w[C:] + _mm(t['qk'][sl], _place_rows(vn, C * i, LANES)) for t, w, vn in zip(ts, ws, v_new)]
    return o, S_new


def _gla_tile(q_ref, k_ref, lf_ref, v_ref, probs, C, gc_scr, prod_scr):
    n = len(probs)
    nsb = LANES // GLA_SUB
    ri, ci, same, tril, _ = _tile_masks(C)
    sub = GLA_SUB.bit_length() - 1
    rsub, csub = jnp.right_shift(ri, sub), jnp.right_shift(ci, sub)
    trilf = tril.astype(F32)
    get = lambda ref: [_tile(ref, p) for p in probs]
    q, k, v = get(q_ref), get(k_ref), get(v_ref)
    gc = [_mm_xl(trilf, x) for x in get(lf_ref)]
    rk, ck = _iota2(GLA_SUB * LANES, LANES)
    sel = (jnp.right_shift(rk, LANES.bit_length() - 1) == ck).astype(BF16)
    for g, (h, j) in enumerate(probs):
        gc_scr[g] = gc[g]
        for I in range(nsb):
            s0 = GLA_SUB * I
            ksub = k[g][s0:s0 + GLA_SUB]
            gsub = gc[g][s0:s0 + GLA_SUB]
            for ii in range(GLA_SUB):
                i = s0 + ii
                gi = gc_scr[g, i:i + 1, :]
                qi = q_ref[LANES * j + i:LANES * j + i + 1, LANES * h:LANES * (h + 1)]
                prod_scr[g, s0:s0 + GLA_SUB, LANES * ii:LANES * (ii + 1)] = (
                    qi * ksub * jnp.exp(jnp.minimum(gi - gsub, 0.0)))
    d = [_mm(prod_scr[g], sel) for g in range(n)]
    AT = []
    for g in range(n):
        strips = [d[g][:GLA_SUB]] + [pltpu.roll(d[g][GLA_SUB * I:GLA_SUB * (I + 1)], GLA_SUB * I, axis=1)
                                     for I in range(1, nsb)]
        AT.append(jnp.where((rsub == csub) & (ri <= ci), jnp.concatenate(strips, axis=0), 0.0))
    nsub = C // GLA_SUB
    if nsub > 1:
        for g in range(n):
            off = []
            for b in range(LANES // C):
                b0 = C * b
                kblk = k[g][b0:b0 + C]
                gblk = gc[g][b0:b0 + C]
                acc = jnp.zeros((C, LANES), F32)
                for J in range(1, nsub):
                    s0 = b0 + GLA_SUB * J
                    ref = gc_scr[g, s0 - 1:s0, :]
                    khat = kblk * jnp.exp(jnp.minimum(ref - gblk, 0.0))
                    qhat = q[g][s0:s0 + GLA_SUB] * jnp.exp(gc[g][s0:s0 + GLA_SUB] - ref)
                    acc = acc + _mm_nt(khat, _place_rows(qhat, s0, LANES))
                off.append(acc)
            AT[g] = AT[g] + jnp.where(same & (rsub < csub), jnp.concatenate(off, axis=0), 0.0)
    o_intra = [_mm_tn(AT[g], v[g]) for g in range(n)]
    return [dict(gc=gc[g], k=k[g], v=v[g], qg=q[g] * jnp.exp(gc[g]), o_intra=o_intra[g]) for g in range(n)]


def _gla_block(i, C, ts, STs):
    sl = slice(C * i, C * (i + 1))
    o, ST_new = [], []
    for t, ST in zip(ts, STs):
        gl = t['gc'][C * (i + 1) - 1:C * (i + 1), :]
        o.append(t['o_intra'][sl] + _mm_nt(t['qg'][sl], ST))
        ST_new.append(ST * jnp.exp(gl) + _mm_tn(t['v'][sl], t['k'][sl] * jnp.exp(gl - t['gc'][sl])))
    return o, ST_new


def _shifted(ext, hp, s, tm, back):
    return ext[hp - back * s:hp - back * s + tm, :]


def _rwkv_pre_kernel(has_vres, s, *refs):
    if has_vres:
        (x_ref, vf_ref, sh0_ref, nw_ref, mix_ref, wrkv_ref, w0_ref, w1_ref, w2_ref, a0_ref, a1_ref, a2_ref,
         g1_ref, g2_ref, kkw_ref, ka_ref, e_ref, v0_ref, v1_ref, v2_ref,
         r_out, lw_out, k_out, v_out, kk_out, bb_out, g_out, sh_out, ext) = refs
    else:
        (x_ref, sh0_ref, nw_ref, mix_ref, wrkv_ref, w0_ref, w1_ref, w2_ref, a0_ref, a1_ref, a2_ref,
         g1_ref, g2_ref, kkw_ref, ka_ref, e_ref,
         r_out, lw_out, k_out, v_out, kk_out, bb_out, g_out, sh_out, ext) = refs
    t = pl.program_id(1)
    tm, D = x_ref.shape
    hp = ext.shape[0] - tm
    h = _rmsnorm(x_ref[...], nw_ref[...])

    @pl.when(t == 0)
    def _():
        ext[hp - s:hp, :] = sh0_ref[...]

    ext[hp:, :] = h
    d = _shifted(ext, hp, s, tm, 1) - h
    ext[hp - s:hp, :] = h[tm - s:, :]

    @pl.when(t == pl.num_programs(1) - 1)
    def _():
        sh_out[...] = h[tm - s:, :]

    mixed = lambda i: h + d * mix_ref[i:i + 1, :]
    r = _mm(mixed(0), wrkv_ref[0])
    z = w0_ref[...] + _mm(jnp.tanh(_mm(mixed(1), w1_ref[...])), w2_ref[...])
    lw_out[...] = -math.exp(-0.5) * _sigmoid(z)
    k = _mm(mixed(2), wrkv_ref[1])
    xv = mixed(3)
    v = _mm(xv, wrkv_ref[2])
    a = _sigmoid(a0_ref[...] + _mm(_mm(mixed(4), a1_ref[...]), a2_ref[...]))
    if has_vres:
        gate_v = _sigmoid(v0_ref[...] + _mm(_mm(xv, v1_ref[...]), v2_ref[...]))
        v = v + (vf_ref[...] - v) * gate_v
    g_out[...] = _mm(_sigmoid(_mm(mixed(5), g1_ref[...])), g2_ref[...]).astype(g_out.dtype)
    r_out[...] = r.astype(r_out.dtype)
    v_out[...] = v
    kkraw = k * kkw_ref[...]
    e = e_ref[...]
    for c in range(D // LANES):
        sl = slice(c * LANES, (c + 1) * LANES)
        kc = kkraw[:, sl]
        kkn = kc * lax.rsqrt(_head_sum(kc * kc, e) + L2_EPS)
        kk_out[:, sl] = kkn.astype(kk_out.dtype)
        bb_out[:, sl] = (kkn * a[:, sl]).astype(bb_out.dtype)
    k_out[...] = (k * (1.0 + (a - 1.0) * ka_ref[...])).astype(k_out.dtype)


def _gdn_pre_kernel(s, nheads, x_ref, c0_ref, nw_ref, wqkv_ref, wgate_ref, wbg_ref, cw_ref, alog_ref, dtb_ref,
                    q_out, k_out, v_out, gate_out, bg_out, cnew_out, ext):
    t = pl.program_id(1)
    tm, D = x_ref.shape
    hp = ext.shape[0] - tm
    nh = cw_ref.shape[0] - 1
    h = _rmsnorm(x_ref[...], nw_ref[...])

    @pl.when(t == 0)
    def _():
        ext[hp - nh * s:hp, :] = c0_ref[...]

    ext[hp:, :] = _mm(h, wqkv_ref[...])
    y = _shifted(ext, hp, s, tm, 0) * cw_ref[nh:nh + 1, :]
    for j in range(nh):
        y = y + _shifted(ext, hp, s, tm, nh - j) * cw_ref[j:j + 1, :]
    hist = ext[hp + tm - nh * s:hp + tm, :]

    @pl.when(t == pl.num_programs(1) - 1)
    def _():
        cnew_out[...] = hist

    ext[hp - nh * s:hp, :] = hist
    y = _silu(y)
    qk_w = D
    for c in range(qk_w // LANES):
        sl = slice(c * LANES, (c + 1) * LANES)
        qc = y[:, sl]
        q_out[:, sl] = (qc * (lax.rsqrt(jnp.sum(qc * qc, axis=-1, keepdims=True) + L2_EPS) * LANES ** -0.5)
                        ).astype(q_out.dtype)
        sl2 = slice(qk_w + c * LANES, qk_w + (c + 1) * LANES)
        kc = y[:, sl2]
        k_out[:, sl] = (kc * lax.rsqrt(jnp.sum(kc * kc, axis=-1, keepdims=True) + L2_EPS)).astype(k_out.dtype)
    v_out[...] = y[:, 2 * qk_w:].astype(v_out.dtype)
    gate_out[...] = _mm(h, wgate_ref[...]).astype(gate_out.dtype)
    rest = _mm(h, wbg_ref[...])
    lane = lax.broadcasted_iota(jnp.int32, rest.shape, 1)
    bg_out[...] = jnp.where(lane < nheads, _sigmoid(rest),
                            -jnp.exp(alog_ref[...]) * _softplus(rest + dtb_ref[...]))


def _hgrn_pre_kernel(layer, x_ref, nw_ref, win_ref, lb_ref, q_out, k_out, lf_out, v_out, gate_out):
    D = x_ref.shape[1]
    h = _rmsnorm(x_ref[...], nw_ref[...])
    lbp = lb_ref[...]
    ex = jnp.exp(lbp - jnp.max(lbp, axis=0, keepdims=True))
    soft = ex / jnp.sum(ex, axis=0, keepdims=True)
    row = lax.broadcasted_iota(jnp.int32, soft.shape, 0)
    lb = jnp.sum(jnp.where((row >= 1) & (row <= layer), soft, 0.0), axis=0, keepdims=True)
    p = _mm(h, win_ref[...])
    q_out[...] = _silu(p[:, :D])
    f = lb + (1.0 - lb) * _sigmoid(p[:, D:2 * D])
    k_out[...] = 1.0 - f
    lf_out[...] = jnp.log(f)
    v_out[...] = p[:, 2 * D:3 * D].astype(v_out.dtype)
    gate_out[...] = p[:, 3 * D:].astype(gate_out.dtype)


def _out_proj_kernel(y_ref, x_ref, wo_ref, pw_ref, x_out):
    x_out[...] = x_ref[...] + _rmsnorm(_mm(y_ref[...], wo_ref[...]), pw_ref[...])


def _ffn_kernel(s, x_ref, c0_ref, nw_ref, wup_ref, cw_ref, wdn_ref, pw_ref, x_out, cnew_out, ext):
    t = pl.program_id(1)
    tm, D = x_ref.shape
    hp = ext.shape[0] - tm
    nh = cw_ref.shape[0] - 1
    dff = wdn_ref.shape[0]
    x = x_ref[...]
    h = _rmsnorm(x, nw_ref[...])

    @pl.when(t == 0)
    def _():
        ext[hp - nh * s:hp, :] = c0_ref[...]

    ext[hp:, :] = _mm(h, wup_ref[...])
    y = _shifted(ext, hp, s, tm, 0) * cw_ref[nh:nh + 1, :]
    for j in range(nh):
        y = y + _shifted(ext, hp, s, tm, nh - j) * cw_ref[j:j + 1, :]
    hist = ext[hp + tm - nh * s:hp + tm, :]

    @pl.when(t == pl.num_programs(1) - 1)
    def _():
        cnew_out[...] = hist

    ext[hp - nh * s:hp, :] = hist
    act = _silu(y[:, dff:]) * y[:, :dff]
    x_out[...] = x + _rmsnorm(_mm(act, wdn_ref[...]), pw_ref[...])


def _const_spec(a):
    nd = a.ndim
    return pl.BlockSpec(a.shape, lambda g, t: (0,) * nd, pipeline_mode=pl.Buffered(1))


def _row_call(body, name, tm, tiled_ins, group_ins, const_ins, tiled_out_widths, group_out_shapes, scratch):
    G, R, _ = tiled_ins[0].shape
    assert R % tm == 0
    tile_spec = lambda c: pl.BlockSpec((None, tm, c), lambda g, t: (g, t, 0))
    group_spec = lambda n, c: pl.BlockSpec((None, n, c), lambda g, t: (g, 0, 0))
    group_in_spec = lambda n, c: pl.BlockSpec((None, n, c), lambda g, t: (g, 0, 0), pipeline_mode=pl.Buffered(1))
    in_specs = ([tile_spec(a.shape[2]) for a in tiled_ins] + [group_in_spec(*a.shape[1:]) for a in group_ins]
                + [_const_spec(a) for a in const_ins])
    tiled_outs = [c if isinstance(c, tuple) else (c, F32) for c in tiled_out_widths]
    out_specs = [tile_spec(c) for c, _ in tiled_outs] + [group_spec(n, c) for n, c in group_out_shapes]
    out_shape = ([jax.ShapeDtypeStruct((G, R, c), dt) for c, dt in tiled_outs]
                 + [jax.ShapeDtypeStruct((G, n, c), F32) for n, c in group_out_shapes])
    return pl.pallas_call(
        body, name=name, grid=(G, R // tm), in_specs=in_specs, out_specs=out_specs, out_shape=out_shape,
        scratch_shapes=scratch,
        compiler_params=pltpu.CompilerParams(dimension_semantics=("parallel", "arbitrary"),
                                             vmem_limit_bytes=VMEM_LIMIT),
    )(*tiled_ins, *group_ins, *const_ins)


def _hist_pad(n):
    return -(-n // SUBLANES) * SUBLANES


def _tile(ref, p):
    h, j = p
    return ref[LANES * j:LANES * (j + 1), LANES * h:LANES * (h + 1)].astype(F32)


def _scan_states(chained, C, probs, tiles, scr, load, store, block_fn):
    nb = LANES // C
    tt = pl.program_id(2)
    heads = sorted({h for h, _ in probs})
    nt = len(probs) // len(heads)
    outs = {}
    if chained:
        @pl.when(tt == 0)
        def _():
            for h in heads:
                scr[h] = load((h,))

        Ss = [scr[h] for h in heads]
        for j in range(nt):
            sel = [tiles[probs.index((h, j))] for h in heads]
            for i in range(nb):
                os, Ss = block_fn(i, sel, Ss)
                for h, o in zip(heads, os):
                    outs[(h, j, i)] = o
        for h, S in zip(heads, Ss):
            scr[h] = S

        @pl.when(tt == pl.num_programs(2) - 1)
        def _():
            for h, S in zip(heads, Ss):
                store((h,), S)
    else:
        for i in range(nb):
            Ss = [load((j * nb + i, h)) for h, j in probs]
            os, Ss = block_fn(i, tiles, Ss)
            for (h, j), o, S in zip(probs, os, Ss):
                outs[(h, j, i)] = o
                store((j * nb + i, h), S)
    return [jnp.concatenate([outs[(h, j, i)] for i in range(nb)], axis=0) if nb > 1 else outs[(h, j, 0)]
            for h, j in probs]


def _scan_probs(ref):
    return [(h, j) for h in range(ref.shape[1] // LANES) for j in range(ref.shape[0] // LANES)]


def _rwkv_scan_kernel(C, chained, r_ref, lw_ref, k_ref, v_ref, kk_ref, bb_ref, g_ref, rk_ref, lnw_ref, lnb_ref,
                      e_ref, s0_ref, y_out, s_out, scr, bd):
    N = RWKV_N
    e = e_ref[...]
    probs = _scan_probs(r_ref)
    nt = r_ref.shape[0] // LANES
    nb = LANES // C
    get = lambda ref: [_tile(ref, p) for p in probs]
    r, k, v = get(r_ref), get(k_ref), get(v_ref)
    tiles = _rwkv_tile(r, get(lw_ref), k, v, get(kk_ref), get(bb_ref), C)

    bd[...] = jnp.zeros(bd.shape, F32)
    slot = lambda idx: idx[-1] * nt + (idx[0] // nb if len(idx) > 1 else 0)
    heads = lambda idx: (idx[:-1] + (2 * idx[-1],), idx[:-1] + (2 * idx[-1] + 1,))

    def load(idx):
        p = slot(idx)
        h0, h1 = heads(idx)
        bd[p, :N, :N] = s0_ref[h0]
        bd[p, N:, N:] = s0_ref[h1]
        return bd[p]

    def store(idx, S):
        p = slot(idx)
        h0, h1 = heads(idx)
        bd[p] = S
        s_out[h0] = bd[p, :N, :N]
        s_out[h1] = bd[p, N:, N:]

    O = _scan_states(chained, C, probs, tiles, scr, load, store, lambda i, ts, Ss: _rwkv_block(i, C, ts, Ss))
    np_ = len(probs)
    lanes = [slice(LANES * h, LANES * (h + 1)) for h, _ in probs]
    s1 = _head_sum(jnp.concatenate(O + [r[n] * k[n] * rk_ref[:, lanes[n]] for n in range(np_)], axis=0), e)
    d = [O[n] - s1[LANES * n:LANES * (n + 1)] * (1.0 / N) for n in range(np_)]
    s2 = _head_sum(jnp.concatenate([x * x for x in d], axis=0), e)
    for n, (h, j) in enumerate(probs):
        var = s2[LANES * n:LANES * (n + 1)] * (1.0 / N)
        on = d[n] * lax.rsqrt(var + RWKV_LNX_EPS) * lnw_ref[:, lanes[n]] + lnb_ref[:, lanes[n]]
        bonus = s1[LANES * (np_ + n):LANES * (np_ + n + 1)] * v[n]
        y_out[LANES * j:LANES * (j + 1), lanes[n]] = ((on + bonus) * _tile(g_ref, (h, j))).astype(y_out.dtype)


def _gdn_scan_kernel(C, chained, q_ref, k_ref, v_ref, gate_ref, g_ref, beta_ref, nw_ref, s0_ref, y_out, s_out,
                     scr):
    probs = _scan_probs(q_ref)
    get = lambda ref: [_tile(ref, p) for p in probs]
    tiles = _gdn_tile(get(q_ref), get(k_ref), get(v_ref), [g_ref[h, j:j + 1, :] for h, j in probs],
                      [beta_ref[h, j:j + 1, :] for h, j in probs], C)

    def store(idx, S):
        s_out[idx] = S

    O = _scan_states(chained, C, probs, tiles, scr, lambda idx: s0_ref[idx], store,
                     lambda i, ts, Ss: _gdn_block(i, C, ts, Ss))
    for n, (h, j) in enumerate(probs):
        y_out[LANES * j:LANES * (j + 1), LANES * h:LANES * (h + 1)] = (
            _rmsnorm(O[n], nw_ref[...]) * _silu(_tile(gate_ref, (h, j)))).astype(y_out.dtype)


def _gla_scan_kernel(C, chained, q_ref, k_ref, lf_ref, v_ref, gate_ref, nw_ref, s0_ref, y_out, s_out, scr, gc_scr,
                     prod_scr):
    probs = _scan_probs(q_ref)
    tiles = _gla_tile(q_ref, k_ref, lf_ref, v_ref, probs, C, gc_scr, prod_scr)

    def store(idx, ST):
        s_out[idx] = ST.T

    O = _scan_states(chained, C, probs, tiles, scr, lambda idx: s0_ref[idx].T, store,
                     lambda i, ts, Ss: _gla_block(i, C, ts, Ss))
    for n, (h, j) in enumerate(probs):
        y_out[LANES * j:LANES * (j + 1), LANES * h:LANES * (h + 1)] = (
            _rmsnorm(O[n], nw_ref[...]) * _silu(_tile(gate_ref, (h, j)))).astype(y_out.dtype)


def _scan_call(body, name, C, chained, tiles, extra_ins, extra_specs, s0, nheads, extra_scratch=()):
    NB, RB, D = tiles[0].shape
    rows = min(SCAN_ROWS, RB)
    hg = SCAN_HEADS
    assert RB % rows == 0 and rows % LANES == 0 and nheads % hg == 0
    tile_spec = pl.BlockSpec((None, rows, hg * LANES), lambda b, h, t: (b, t, h))
    hs = hg * (s0.shape[1] // nheads)
    if chained:
        s_spec = pl.BlockSpec((None, hs) + s0.shape[2:], lambda b, h, t: (b, h, 0, 0))
    else:
        nt = RB // rows
        s_spec = pl.BlockSpec((rows // C, hs) + s0.shape[2:], lambda b, h, t: (b * nt + t, h, 0, 0))
    return pl.pallas_call(
        functools.partial(body, C, chained), name=name, grid=(NB, nheads // hg, RB // rows),
        in_specs=[tile_spec] * len(tiles) + list(extra_specs) + [s_spec],
        out_specs=[tile_spec, s_spec],
        out_shape=[jax.ShapeDtypeStruct((NB, RB, D), BF16), jax.ShapeDtypeStruct(s0.shape, F32)],
        scratch_shapes=[pltpu.VMEM((hg, LANES, LANES), F32)] + list(extra_scratch),
        compiler_params=pltpu.CompilerParams(dimension_semantics=("parallel", "parallel", "arbitrary"),
                                             vmem_limit_bytes=VMEM_LIMIT),
    )(*tiles, *extra_ins, s0)


class _Group:
    def __init__(self, B, T, time_major):
        self.B, self.T, self.tm_major = B, T, time_major
        if time_major:
            self.G, self.R, self.s = 1, B * T, B
            self.C = -(-T // BF16_ROWS) * BF16_ROWS
            assert LANES % self.C == 0 and (B * self.C) % LANES == 0
        else:
            self.G, self.R, self.s = B, T, 1
            self.C = SCAN_CHUNK
            assert T % LANES == 0
        self.chained = not time_major
        self.tile = min(ROW_TILE, self.R)
        self.ffn_tile = self.s if time_major else min(FFN_TILE, self.R)

    def to_rows(self, x):
        if self.tm_major:
            return jnp.swapaxes(x, 0, 1).reshape(1, self.R, x.shape[-1])
        return x

    def hist_to_rows(self, h):
        if self.tm_major:
            return jnp.swapaxes(h, 0, 1).reshape(1, -1, h.shape[-1])
        return h

    def hist_from_rows(self, h, n):
        if self.tm_major:
            return jnp.swapaxes(h.reshape(n, self.B, h.shape[-1]), 0, 1)
        return h

    def to_scan(self, a):
        if not self.tm_major:
            return a
        a = jnp.swapaxes(a.reshape(self.T, self.B, a.shape[-1]), 0, 1)
        a = jnp.pad(a, ((0, 0), (0, self.C - self.T), (0, 0)))
        return a.reshape(1, self.B * self.C, a.shape[-1])

    def from_scan(self, a):
        if not self.tm_major:
            return a
        return self.to_rows(a.reshape(self.B, self.C, a.shape[-1])[:, :self.T])


def _rwkv_layer(grp, x, shift0, S0, v_first, P, j):
    D = x.shape[-1]
    s, tm = grp.s, grp.tile
    N = RWKV_N
    has_vres = v_first is not None
    r2, c2 = _iota2(LANES, LANES)
    e = ((r2 // N) == (c2 // N)).astype(BF16)
    row = lambda a: a.reshape(1, -1)
    tiled = [x] + ([v_first] if has_vres else [])
    consts = [row(P['norm_mix_pre_i']), P['rwkv_mix'][j], P['rwkv_w_rkv'][j].astype(BF16),
              row(P['rwkv_w0'][j]), P['rwkv_w1'][j].astype(BF16), P['rwkv_w2'][j].astype(BF16),
              row(P['rwkv_a0'][j]), P['rwkv_a1'][j].astype(BF16), P['rwkv_a2'][j].astype(BF16),
              P['rwkv_g1'][j].astype(BF16), P['rwkv_g2'][j].astype(BF16),
              row(P['rwkv_k_k'][j]), row(P['rwkv_k_a'][j]), e]
    if has_vres:
        consts += [row(P['rwkv_v0'][j - 1]), P['rwkv_v1'][j - 1].astype(BF16), P['rwkv_v2'][j - 1].astype(BF16)]
    r, lw, k, v, kk, bb, g, shift = _row_call(
        functools.partial(_rwkv_pre_kernel, has_vres, s), f"rwkv_pre_{j}", tm, tiled, [shift0], consts,
        [(D, BF16), D, (D, BF16), D, (D, BF16), (D, BF16), (D, BF16)], [(s, D)],
        [pltpu.VMEM((_hist_pad(s) + tm, D), F32)])
    if not has_vres:
        v_first = v
    tiles = [grp.to_scan(a) for a in (r, lw, k, v, kk, bb, g)]
    nprob = SCAN_HEADS * (min(SCAN_ROWS, tiles[0].shape[1]) // LANES)
    vec_spec = pl.BlockSpec((1, SCAN_HEADS * LANES), lambda b, h, t: (0, h))
    e_spec = pl.BlockSpec((LANES, LANES), lambda b, h, t: (0, 0))
    y, S = _scan_call(_rwkv_scan_kernel, f"rwkv_scan_{j}", grp.C, grp.chained, tiles,
                      [row(P['rwkv_r_k'][j]), row(P['rwkv_lnx_w'][j]), row(P['rwkv_lnx_b'][j]), e],
                      [vec_spec] * 3 + [e_spec], S0, D // LANES,
                      extra_scratch=[pltpu.VMEM((nprob, LANES, LANES), F32)])
    return grp.from_scan(y), P['rwkv_w_o'][j], shift, S, v_first


def _gdn_layer(grp, x, conv0, S0, P, j):
    D = x.shape[-1]
    s, tm = grp.s, grp.tile
    w_in = P['gdn_w_in'][j]
    cw = P['gdn_conv_w'][j]
    nh, cdim = cw.shape[0] - 1, cw.shape[1]
    H = S0.shape[1]
    assert grp.T >= nh
    row = lambda a: a.reshape(1, -1)
    lane_pad = lambda a: jnp.pad(a, ((0, 0), (0, LANES - a.shape[1])))
    zeros = jnp.zeros((1, H), F32)
    consts = [row(P['norm_mix_pre_i']), w_in[:, :cdim].astype(BF16), w_in[:, cdim:cdim + D].astype(BF16),
              lane_pad(w_in[:, cdim + D:]).astype(BF16), cw,
              lane_pad(jnp.concatenate([zeros, row(P['gdn_a_log'][j])], axis=1)),
              lane_pad(jnp.concatenate([zeros, row(P['gdn_dt_bias'][j])], axis=1))]
    q, k, v, gate, bg, conv_new = _row_call(
        functools.partial(_gdn_pre_kernel, s, H), f"gdn_pre_{j}", tm, [x], [conv0], consts,
        [(D, BF16)] * 4 + [LANES], [(nh * s, cdim)], [pltpu.VMEM((_hist_pad(nh * s) + tm, cdim), F32)])
    tiles = [grp.to_scan(a) for a in (q, k, v, gate)]
    NB, RB, _ = tiles[0].shape
    rows = min(SCAN_ROWS, RB)
    nt = RB // rows
    bg = grp.to_scan(bg[:, :, :2 * H]).reshape(NB * RB, 2 * H).T.reshape(2 * H, NB * nt, rows // LANES, LANES)
    hg = SCAN_HEADS
    g_spec = pl.BlockSpec((hg, None, rows // LANES, LANES), lambda b, h, t: (H // hg + h, b * nt + t, 0, 0))
    beta_spec = pl.BlockSpec((hg, None, rows // LANES, LANES), lambda b, h, t: (h, b * nt + t, 0, 0))
    vec_spec = pl.BlockSpec((1, LANES), lambda b, h, t: (0, 0))
    y, S = _scan_call(_gdn_scan_kernel, f"gdn_scan_{j}", grp.C, grp.chained, tiles,
                      [bg, bg, row(P['gdn_norm_w'][j])], [g_spec, beta_spec, vec_spec], S0, H)
    return grp.from_scan(y), P['gdn_w_o'][j], conv_new, S


def _hgrn_layer(grp, x, S0, P, i, j):
    D = x.shape[-1]
    tm = grp.tile
    H = S0.shape[1]
    row = lambda a: a.reshape(1, -1)
    consts = [row(P['norm_mix_pre_i']), P['hgrn_w_in'][j].astype(BF16), P['hgrn_lb']]
    q, k, lf, v, gate = _row_call(functools.partial(_hgrn_pre_kernel, i), f"hgrn_pre_{j}", tm, [x], [], consts,
                                  [D, D, D, (D, BF16), (D, BF16)], [], [])
    tiles = [grp.to_scan(a) for a in (q, k, lf, v, gate)]
    nprob = SCAN_HEADS * (min(SCAN_ROWS, tiles[0].shape[1]) // LANES)
    vec_spec = pl.BlockSpec((1, LANES), lambda b, h, t: (0, 0))
    y, S = _scan_call(_gla_scan_kernel, f"hgrn_scan_{j}", grp.C, grp.chained, tiles, [row(P['hgrn_norm_w'][j])],
                      [vec_spec], S0, H,
                      extra_scratch=[pltpu.VMEM((nprob, LANES, LANES), F32),
                                     pltpu.VMEM((nprob, LANES, GLA_SUB * LANES), F32)])
    return grp.from_scan(y), P['hgrn_w_o'][j], S


def _trunk(grp, x, shift0, wkv0, gconv0, gS0, hS0, fconv0, P):
    D = x.shape[-1]
    depth = P['norm_mix_pre'].shape[0]
    row = lambda a: a.reshape(1, -1)
    x = grp.to_rows(x)
    v_first = None
    shift, wkv, gconv, gS, hS, fconv = [], [], [], [], [], []
    for i in range(depth):
        kind, j = i % 3, i // 3
        P = dict(P, norm_mix_pre_i=P['norm_mix_pre'][i])
        if kind == 0:
            y, w_o, s_shift, s_wkv, v_first = _rwkv_layer(grp, x, grp.hist_to_rows(shift0[j][:, None]), wkv0[j],
                                                         v_first, P, j)
            shift.append(grp.hist_from_rows(s_shift, 1)[:, 0])
            wkv.append(s_wkv)
        elif kind == 1:
            y, w_o, c_new, s_new = _gdn_layer(grp, x, grp.hist_to_rows(gconv0[j]), gS0[j], P, j)
            gconv.append(grp.hist_from_rows(c_new, gconv0.shape[2]))
            gS.append(s_new)
        else:
            y, w_o, s_new = _hgrn_layer(grp, x, hS0[j], P, i, j)
            hS.append(s_new)
        (x,) = _row_call(_out_proj_kernel, f"out_proj_{i}", grp.tile, [y, x], [],
                         [w_o.astype(BF16), row(P['norm_mix_post'][i])], [D], [], [])
        nh = P['ffn_conv_w'].shape[1] - 1
        dff2 = P['ffn_w_up'].shape[2]
        x, c_new = _row_call(
            functools.partial(_ffn_kernel, grp.s), f"ffn_{i}", grp.ffn_tile, [x], [grp.hist_to_rows(fconv0[i])],
            [row(P['norm_ffn_pre'][i]), P['ffn_w_up'][i].astype(BF16), P['ffn_conv_w'][i],
             P['ffn_w_down'][i].astype(BF16), row(P['norm_ffn_post'][i])],
            [D], [(nh * grp.s, dff2)], [pltpu.VMEM((_hist_pad(nh * grp.s) + grp.ffn_tile, dff2), F32)])
        fconv.append(grp.hist_from_rows(c_new, nh))
    y = x.reshape(grp.T, grp.B, D).swapaxes(0, 1) if grp.tm_major else x
    return y, (jnp.stack(shift), jnp.stack(wkv), jnp.stack(gconv), jnp.stack(gS), jnp.stack(hS), jnp.stack(fconv))


def kernel(x_prompt, x_sample, state_rwkv_shift, state_rwkv_wkv, state_gdn_conv, state_gdn_S, state_hgrn_S, state_ffn_conv, norm_mix_pre, norm_mix_post, norm_ffn_pre, norm_ffn_post, rwkv_mix, rwkv_w_rkv, rwkv_w0, rwkv_w1, rwkv_w2, rwkv_a0, rwkv_a1, rwkv_a2, rwkv_v0, rwkv_v1, rwkv_v2, rwkv_g1, rwkv_g2, rwkv_k_k, rwkv_k_a, rwkv_r_k, rwkv_lnx_w, rwkv_lnx_b, rwkv_w_o, gdn_w_in, gdn_conv_w, gdn_a_log, gdn_dt_bias, gdn_norm_w, gdn_w_o, hgrn_w_in, hgrn_lb, hgrn_norm_w, hgrn_w_o, ffn_w_up, ffn_conv_w, ffn_w_down):
    P = dict(norm_mix_pre=norm_mix_pre, norm_mix_post=norm_mix_post, norm_ffn_pre=norm_ffn_pre,
             norm_ffn_post=norm_ffn_post, rwkv_mix=rwkv_mix, rwkv_w_rkv=rwkv_w_rkv, rwkv_w0=rwkv_w0,
             rwkv_w1=rwkv_w1, rwkv_w2=rwkv_w2, rwkv_a0=rwkv_a0, rwkv_a1=rwkv_a1, rwkv_a2=rwkv_a2,
             rwkv_v0=rwkv_v0, rwkv_v1=rwkv_v1, rwkv_v2=rwkv_v2, rwkv_g1=rwkv_g1, rwkv_g2=rwkv_g2,
             rwkv_k_k=rwkv_k_k, rwkv_k_a=rwkv_k_a, rwkv_r_k=rwkv_r_k, rwkv_lnx_w=rwkv_lnx_w,
             rwkv_lnx_b=rwkv_lnx_b, rwkv_w_o=rwkv_w_o, gdn_w_in=gdn_w_in, gdn_conv_w=gdn_conv_w,
             gdn_a_log=gdn_a_log, gdn_dt_bias=gdn_dt_bias, gdn_norm_w=gdn_norm_w, gdn_w_o=gdn_w_o,
             hgrn_w_in=hgrn_w_in, hgrn_lb=hgrn_lb, hgrn_norm_w=hgrn_norm_w, hgrn_w_o=hgrn_w_o,
             ffn_w_up=ffn_w_up, ffn_conv_w=ffn_conv_w, ffn_w_down=ffn_w_down)
    Bp, Tp, _ = x_prompt.shape
    Bs, Ts, _ = x_sample.shape
    zero_like = lambda st: jnp.zeros((st.shape[0], Bp) + st.shape[2:], st.dtype)
    y_p, (p_shift, p_wkv, p_gconv, p_gS, p_hS, p_fconv) = _trunk(
        _Group(Bp, Tp, False), x_prompt, zero_like(state_rwkv_shift), zero_like(state_rwkv_wkv),
        zero_like(state_gdn_conv), zero_like(state_gdn_S), zero_like(state_hgrn_S), zero_like(state_ffn_conv), P)
    y_s, (s_shift, s_wkv, s_gconv, s_gS, s_hS, s_fconv) = _trunk(
        _Group(Bs, Ts, True), x_sample, state_rwkv_shift, state_rwkv_wkv, state_gdn_conv, state_gdn_S,
        state_hgrn_S, state_ffn_conv, P)
    return (y_p, y_s, p_shift, s_shift, p_wkv, s_wkv, p_gconv, s_gconv,
            p_gS, s_gS, p_hS, s_hS, p_fconv, s_fconv)
```

```python
import functools
import math

import jax
import jax.numpy as jnp
from jax import lax
from jax.experimental import pallas as pl
from jax.experimental.pallas import tpu as pltpu

F32 = jnp.float32
BF16 = jnp.bfloat16

NORM_EPS = 1e-6
L2_EPS = 1e-6
RWKV_LNX_EPS = 64e-5
RWKV_N = 64
LANES = 128
SUBLANES = 8
VMEM_LIMIT = 56 * 1024 * 1024
ROW_TILE = 256
FFN_TILE = 256
SCAN_ROWS = 256
SCAN_HEADS = 4
SCAN_CHUNK = 64
GLA_SUB = 8


def _dg(a, b, ca, cb):
    return lax.dot_general(a, b, (((ca,), (cb,)), ((), ())), preferred_element_type=F32)


def _mm(a, b):
    return _dg(a.astype(BF16), b.astype(BF16), 1, 0)


def _mm_nt(a, b):
    return _dg(a.astype(BF16), b.astype(BF16), 1, 1)


def _mm_tn(a, b):
    return _dg(a.astype(BF16), b.astype(BF16), 0, 0)


def _split3(x):
    hi = x.astype(BF16)
    r1 = x - hi.astype(F32)
    mid = r1.astype(BF16)
    lo = (r1 - mid.astype(F32)).astype(BF16)
    return hi, mid, lo


def _mm_xl(m, x):
    h, mi, lo = _split3(x)
    m = m.astype(BF16)
    return _dg(m, h, 1, 0) + (_dg(m, mi, 1, 0) + _dg(m, lo, 1, 0))


def _iota2(n, m):
    return (lax.broadcasted_iota(jnp.int32, (n, m), 0), lax.broadcasted_iota(jnp.int32, (n, m), 1))


def _sigmoid(x):
    return 1.0 / (1.0 + jnp.exp(-x))


def _silu(x):
    return x * _sigmoid(x)


def _softplus(x):
    return jnp.maximum(x, 0.0) + jnp.log(1.0 + jnp.exp(-jnp.abs(x)))


def _rmsnorm(x, w):
    return x * lax.rsqrt(jnp.mean(x * x, axis=-1, keepdims=True) + NORM_EPS) * w


def _head_sum(x, e):
    return _mm(x, e)


def _unit_lower_inv(Ls, C):
    n = Ls[0].shape[0]
    ri, ci = _iota2(n, n)
    eye = (ri == ci).astype(F32)
    Xs = [eye - L for L in Ls]
    Ps = list(Ls)
    m = 2
    while m < C:
        Ps = [_mm(P, P) for P in Ps]
        Xs = [X + _mm(X, P) for X, P in zip(Xs, Ps)]
        m *= 2
    return Xs


def _tile_masks(C):
    sh = C.bit_length() - 1
    ri, ci = _iota2(LANES, LANES)
    same = jnp.right_shift(ri, sh) == jnp.right_shift(ci, sh)
    return ri, ci, same, same & (ri >= ci), same & (ri > ci)


def _place_rows(x, r0, n):
    parts = []
    if r0:
        parts.append(jnp.zeros((r0, x.shape[1]), x.dtype))
    parts.append(x)
    if n - r0 - x.shape[0]:
        parts.append(jnp.zeros((n - r0 - x.shape[0], x.shape[1]), x.dtype))
    return jnp.concatenate(parts, axis=0) if len(parts) > 1 else x


def _rwkv_tile(r, lw, k, v, kk, bb, C):
    ri, ci, same, tril, stril = _tile_masks(C)
    lane = lax.broadcasted_iota(jnp.int32, (1, LANES), 1)
    m = [(lane < RWKV_N).astype(F32), (lane >= RWKV_N).astype(F32)]
    trilf = tril.astype(F32)
    n = len(r)
    gc = [_mm_xl(trilf, x) for x in lw]
    at = [-kk[g] * jnp.exp(gc[g] - lw[g]) for g in range(n)]
    rt = [r[g] * jnp.exp(gc[g]) for g in range(n)]
    sc = []
    for g in range(n):
        e_neg = jnp.exp(-gc[g])
        lhs = jnp.concatenate([at[g] * m[0], at[g] * m[1], rt[g] * m[0], rt[g] * m[1]], axis=0)
        sc.append(_mm_nt(lhs, jnp.concatenate([bb[g] * e_neg, k[g] * e_neg], axis=0)))
    gh = [(g, hd) for g in range(n) for hd in range(2)]
    a_blk = [sc[g][LANES * hd:LANES * (hd + 1)] for g, hd in gh]
    r_blk = [sc[g][LANES * (2 + hd):LANES * (3 + hd)] for g, hd in gh]
    A_rb = [jnp.where(tril, x[:, :LANES], 0.0) for x in r_blk]
    A_rk = [jnp.where(tril, x[:, LANES:], 0.0) for x in r_blk]
    Tinv = _unit_lower_inv([jnp.where(stril, -x[:, :LANES], 0.0) for x in a_blk], C)
    akv = [_mm(jnp.where(stril, a_blk[p][:, LANES:], 0.0), v[g]) for p, (g, hd) in enumerate(gh)]
    y = [_mm(Tinv[p], jnp.concatenate([at[g] * m[hd], akv[p]], axis=1)) for p, (g, hd) in enumerate(gh)]
    o0 = [_mm(jnp.concatenate([A_rk[p], A_rb[p]], axis=1), jnp.concatenate([v[g], y[p][:, LANES:]], axis=0))
          for p, (g, hd) in enumerate(gh)]
    out = []
    for g in range(n):
        p0, p1 = 2 * g, 2 * g + 1
        out.append(dict(gc=gc[g], rt=rt[g], m=m, Wa=y[p0][:, :LANES] + y[p1][:, :LANES],
                        U0=y[p0][:, LANES:] * m[0] + y[p1][:, LANES:] * m[1],
                        O0=o0[p0] * m[0] + o0[p1] * m[1],
                        A_rb=jnp.concatenate([A_rb[p0], A_rb[p1]], axis=0), k=k[g], v=v[g], bb=bb[g]))
    return out


def _rwkv_block(i, C, ts, Ss):
    sl = slice(C * i, C * (i + 1))
    sl1 = slice(LANES + C * i, LANES + C * (i + 1))
    r2, c2 = _iota2(LANES, LANES)
    blk = ((r2 >= RWKV_N) == (c2 >= RWKV_N)).astype(F32)
    xs = [_mm_nt(jnp.concatenate([t['Wa'][sl], t['rt'][sl]], axis=0), S) for t, S in zip(ts, Ss)]
    S_new, ax = [], []
    for t, S, x in zip(ts, Ss, xs):
        X = x[:C]
        U = t['U0'][sl] + X
        gl = t['gc'][C * (i + 1) - 1:C * (i + 1), :]
        e_last = jnp.exp(gl - t['gc'][sl])
        S_new.append(S * jnp.exp(gl) + blk * _mm_tn(
            jnp.concatenate([U, t['v'][sl]], axis=0),
            jnp.concatenate([t['bb'][sl] * e_last, t['k'][sl] * e_last], axis=0)))
        ax.append(_mm(jnp.concatenate([t['A_rb'][sl], t['A_rb'][sl1]], axis=0), _place_rows(X, C * i, LANES)))
    o = [t['O0'][sl] + x[C:] + a[:C] * t['m'][0] + a[C:] * t['m'][1] for t, x, a in zip(ts, xs, ax)]
    return o, S_new


def _gdn_tile(q, k, v, g_row, beta_row, C):
    ri, ci, same, tril, stril = _tile_masks(C)
    eye = ri == ci
    lsum = lambda msk, x: jnp.sum(jnp.where(msk, x, 0.0), axis=1, keepdims=True)
    n = len(q)
    pre = []
    for g in range(n):
        g_b = jnp.broadcast_to(g_row[g], (LANES, LANES))
        beta_col = lsum(eye, jnp.broadcast_to(beta_row[g], (LANES, LANES)))
        g_col = lsum(eye, g_b)
        gc_col = lsum(tril, g_b)
        gl_col = lsum(same, g_b)
        gc_row = jnp.sum(jnp.where(same & (ri <= ci), jnp.broadcast_to(g_col, (LANES, LANES)), 0.0),
                         axis=0, keepdims=True)
        decay = jnp.where(tril, jnp.exp(jnp.minimum(gc_col - gc_row, 0.0)), 0.0)
        pre.append(dict(beta=beta_col, gc=gc_col, gl=gl_col, decay=decay, kb=k[g] * beta_col))
    sc = [_mm_nt(jnp.concatenate([pre[g]['kb'], q[g]], axis=0), k[g]) for g in range(n)]
    Tinv = _unit_lower_inv([jnp.where(stril, sc[g][:LANES] * pre[g]['decay'], 0.0) for g in range(n)], C)
    out = []
    for g in range(n):
        p = pre[g]
        e_gc = jnp.exp(p['gc'])
        uw = _mm(Tinv[g], jnp.concatenate([v[g] * p['beta'], p['kb'] * e_gc], axis=1))
        out.append(dict(u=uw[:, :LANES], wk=uw[:, LANES:], qk=sc[g][LANES:] * p['decay'], qg=q[g] * e_gc,
                        kd=k[g] * jnp.exp(p['gl'] - p['gc']), egl=jnp.exp(p['gl'])))
    return out


def _gdn_block(i, C, ts, Ss):
    sl = slice(C * i, C * (i + 1))
    ws = [_mm(jnp.concatenate([t['wk'][sl], t['qg'][sl]], axis=0), S) for t, S in zip(ts, Ss)]
    v_new = [t['u'][sl] - w[:C] for t, w in zip(ts, ws)]
    S_new = [S * t['egl'][C * i:C * i + 1, :] + _mm_tn(t['kd'][sl], vn) for t, S, vn in zip(ts, Ss, v_new)]
    o = [w[C:] + _mm(t['qk'][sl], _place_rows(vn, C * i, LANES)) for t, w, vn in zip(ts, ws, v_new)]
    return o, S_new


def _gla_tile(q_ref, k_ref, lf_ref, v_ref, probs, C, gc_scr, prod_scr):
    n = len(probs)
    nsb = LANES // GLA_SUB
    ri, ci, same, tril, _ = _tile_masks(C)
    sub = GLA_SUB.bit_length() - 1
    rsub, csub = jnp.right_shift(ri, sub), jnp.right_shift(ci, sub)
    trilf = tril.astype(F32)
    get = lambda ref: [_tile(ref, p) for p in probs]
    q, k, v = get(q_ref), get(k_ref), get(v_ref)
    gc = [_mm_xl(trilf, x) for x in get(lf_ref)]
    rk, ck = _iota2(GLA_SUB * LANES, LANES)
    sel = (jnp.right_shift(rk, LANES.bit_length() - 1) == ck).astype(BF16)
    for g, (h, j) in enumerate(probs):
        gc_scr[g] = gc[g]
        for I in range(nsb):
            s0 = GLA_SUB * I
            ksub = k[g][s0:s0 + GLA_SUB]
            gsub = gc[g][s0:s0 + GLA_SUB]
            for ii in range(GLA_SUB):
                i = s0 + ii
                gi = gc_scr[g, i:i + 1, :]
                qi = q_ref[LANES * j + i:LANES * j + i + 1, LANES * h:LANES * (h + 1)]
                prod_scr[g, s0:s0 + GLA_SUB, LANES * ii:LANES * (ii + 1)] = (
                    qi * ksub * jnp.exp(jnp.minimum(gi - gsub, 0.0)))
    d = [_mm(prod_scr[g], sel) for g in range(n)]
    AT = []
    for g in range(n):
        strips = [d[g][:GLA_SUB]] + [pltpu.roll(d[g][GLA_SUB * I:GLA_SUB * (I + 1)], GLA_SUB * I, axis=1)
                                     for I in range(1, nsb)]
        AT.append(jnp.where((rsub == csub) & (ri <= ci), jnp.concatenate(strips, axis=0), 0.0))
    nsub = C // GLA_SUB
    if nsub > 1:
        for g in range(n):
            off = []
            for b in range(LANES // C):
                b0 = C * b
                kblk = k[g][b0:b0 + C]
                gblk = gc[g][b0:b0 + C]
                acc = jnp.zeros((C, LANES), F32)
                for J in range(1, nsub):
                    s0 = b0 + GLA_SUB * J
                    ref = gc_scr[g, s0 - 1:s0, :]
                    khat = kblk * jnp.exp(jnp.minimum(ref - gblk, 0.0))
                    qhat = q[g][s0:s0 + GLA_SUB] * jnp.exp(gc[g][s0:s0 + GLA_SUB] - ref)
                    acc = acc + _mm_nt(khat, _place_rows(qhat, s0, LANES))
                off.append(acc)
            AT[g] = AT[g] + jnp.where(same & (rsub < csub), jnp.concatenate(off, axis=0), 0.0)
    o_intra = [_mm_tn(AT[g], v[g]) for g in range(n)]
    return [dict(gc=gc[g], k=k[g], v=v[g], qg=q[g] * jnp.exp(gc[g]), o_intra=o_intra[g]) for g in range(n)]


def _gla_block(i, C, ts, STs):
    sl = slice(C * i, C * (i + 1))
    o, ST_new = [], []
    for t, ST in zip(ts, STs):
        gl = t['gc'][C * (i + 1) - 1:C * (i + 1), :]
        o.append(t['o_intra'][sl] + _mm_nt(t['qg'][sl], ST))
        ST_new.append(ST * jnp.exp(gl) + _mm_tn(t['v'][sl], t['k'][sl] * jnp.exp(gl - t['gc'][sl])))
    return o, ST_new


def _shifted(ext, hp, s, tm, back):
    return ext[hp - back * s:hp - back * s + tm, :]


def _rwkv_pre_kernel(has_vres, s, *refs):
    if has_vres:
        (x_ref, vf_ref, sh0_ref, nw_ref, mix_ref, wrkv_ref, w0_ref, w1_ref, w2_ref, a0_ref, a1_ref, a2_ref,
         g1_ref, g2_ref, kkw_ref, ka_ref, e_ref, v0_ref, v1_ref, v2_ref,
         r_out, lw_out, k_out, v_out, kk_out, bb_out, g_out, sh_out, ext) = refs
    else:
        (x_ref, sh0_ref, nw_ref, mix_ref, wrkv_ref, w0_ref, w1_ref, w2_ref, a0_ref, a1_ref, a2_ref,
         g1_ref, g2_ref, kkw_ref, ka_ref, e_ref,
         r_out, lw_out, k_out, v_out, kk_out, bb_out, g_out, sh_out, ext) = refs
    t = pl.program_id(1)
    tm, D = x_ref.shape
    hp = ext.shape[0] - tm
    h = _rmsnorm(x_ref[...], nw_ref[...])

    @pl.when(t == 0)
    def _():
        ext[hp - s:hp, :] = sh0_ref[...]

    ext[hp:, :] = h
    d = _shifted(ext, hp, s, tm, 1) - h
    ext[hp - s:hp, :] = h[tm - s:, :]

    @pl.when(t == pl.num_programs(1) - 1)
    def _():
        sh_out[...] = h[tm - s:, :]

    mixed = lambda i: h + d * mix_ref[i:i + 1, :]
    r = _mm(mixed(0), wrkv_ref[0])
    z = w0_ref[...] + _mm(jnp.tanh(_mm(mixed(1), w1_ref[...])), w2_ref[...])
    lw_out[...] = -math.exp(-0.5) * _sigmoid(z)
    k = _mm(mixed(2), wrkv_ref[1])
    xv = mixed(3)
    v = _mm(xv, wrkv_ref[2])
    a = _sigmoid(a0_ref[...] + _mm(_mm(mixed(4), a1_ref[...]), a2_ref[...]))
    if has_vres:
        gate_v = _sigmoid(v0_ref[...] + _mm(_mm(xv, v1_ref[...]), v2_ref[...]))
        v = v + (vf_ref[...] - v) * gate_v
    g_out[...] = _mm(_sigmoid(_mm(mixed(5), g1_ref[...])), g2_ref[...]).astype(g_out.dtype)
    r_out[...] = r.astype(r_out.dtype)
    v_out[...] = v
    kkraw = k * kkw_ref[...]
    e = e_ref[...]
    for c in range(D // LANES):
        sl = slice(c * LANES, (c + 1) * LANES)
        kc = kkraw[:, sl]
        kkn = kc * lax.rsqrt(_head_sum(kc * kc, e) + L2_EPS)
        kk_out[:, sl] = kkn.astype(kk_out.dtype)
        bb_out[:, sl] = (kkn * a[:, sl]).astype(bb_out.dtype)
    k_out[...] = (k * (1.0 + (a - 1.0) * ka_ref[...])).astype(k_out.dtype)


def _gdn_pre_kernel(s, nheads, x_ref, c0_ref, nw_ref, wqkv_ref, wgate_ref, wbg_ref, cw_ref, alog_ref, dtb_ref,
                    q_out, k_out, v_out, gate_out, bg_out, cnew_out, ext):
    t = pl.program_id(1)
    tm, D = x_ref.shape
    hp = ext.shape[0] - tm
    nh = cw_ref.shape[0] - 1
    h = _rmsnorm(x_ref[...], nw_ref[...])

    @pl.when(t == 0)
    def _():
        ext[hp - nh * s:hp, :] = c0_ref[...]

    ext[hp:, :] = _mm(h, wqkv_ref[...])
    y = _shifted(ext, hp, s, tm, 0) * cw_ref[nh:nh + 1, :]
    for j in range(nh):
        y = y + _shifted(ext, hp, s, tm, nh - j) * cw_ref[j:j + 1, :]
    hist = ext[hp + tm - nh * s:hp + tm, :]

    @pl.when(t == pl.num_programs(1) - 1)
    def _():
        cnew_out[...] = hist

    ext[hp - nh * s:hp, :] = hist
    y = _silu(y)
    qk_w = D
    for c in range(qk_w // LANES):
        sl = slice(c * LANES, (c + 1) * LANES)
        qc = y[:, sl]
        q_out[:, sl] = (qc * (lax.rsqrt(jnp.sum(qc * qc, axis=-1, keepdims=True) + L2_EPS) * LANES ** -0.5)
                        ).astype(q_out.dtype)
        sl2 = slice(qk_w + c * LANES, qk_w + (c + 1) * LANES)
        kc = y[:, sl2]
        k_out[:, sl] = (kc * lax.rsqrt(jnp.sum(kc * kc, axis=-1, keepdims=True) + L2_EPS)).astype(k_out.dtype)
    v_out[...] = y[:, 2 * qk_w:].astype(v_out.dtype)
    gate_out[...] = _mm(h, wgate_ref[...]).astype(gate_out.dtype)
    rest = _mm(h, wbg_ref[...])
    lane = lax.broadcasted_iota(jnp.int32, rest.shape, 1)
    bg_out[...] = jnp.where(lane < nheads, _sigmoid(rest),
                            -jnp.exp(alog_ref[...]) * _softplus(rest + dtb_ref[...]))


def _hgrn_pre_kernel(layer, x_ref, nw_ref, win_ref, lb_ref, q_out, k_out, lf_out, v_out, gate_out):
    D = x_ref.shape[1]
    h = _rmsnorm(x_ref[...], nw_ref[...])
    lbp = lb_ref[...]
    ex = jnp.exp(lbp - jnp.max(lbp, axis=0, keepdims=True))
    soft = ex / jnp.sum(ex, axis=0, keepdims=True)
    row = lax.broadcasted_iota(jnp.int32, soft.shape, 0)
    lb = jnp.sum(jnp.where((row >= 1) & (row <= layer), soft, 0.0), axis=0, keepdims=True)
    p = _mm(h, win_ref[...])
    q_out[...] = _silu(p[:, :D])
    f = lb + (1.0 - lb) * _sigmoid(p[:, D:2 * D])
    k_out[...] = 1.0 - f
    lf_out[...] = jnp.log(f)
    v_out[...] = p[:, 2 * D:3 * D].astype(v_out.dtype)
    gate_out[...] = p[:, 3 * D:].astype(gate_out.dtype)


def _out_proj_kernel(y_ref, x_ref, wo_ref, pw_ref, x_out):
    x_out[...] = x_ref[...] + _rmsnorm(_mm(y_ref[...], wo_ref[...]), pw_ref[...])


def _ffn_kernel(s, x_ref, c0_ref, nw_ref, wup_ref, cw_ref, wdn_ref, pw_ref, x_out, cnew_out, ext):
    t = pl.program_id(1)
    tm, D = x_ref.shape
    hp = ext.shape[0] - tm
    nh = cw_ref.shape[0] - 1
    dff = wdn_ref.shape[0]
    x = x_ref[...]
    h = _rmsnorm(x, nw_ref[...])

    @pl.when(t == 0)
    def _():
        ext[hp - nh * s:hp, :] = c0_ref[...]

    ext[hp:, :] = _mm(h, wup_ref[...])
    y = _shifted(ext, hp, s, tm, 0) * cw_ref[nh:nh + 1, :]
    for j in range(nh):
        y = y + _shifted(ext, hp, s, tm, nh - j) * cw_ref[j:j + 1, :]
    hist = ext[hp + tm - nh * s:hp + tm, :]

    @pl.when(t == pl.num_programs(1) - 1)
    def _():
        cnew_out[...] = hist

    ext[hp - nh * s:hp, :] = hist
    act = _silu(y[:, dff:]) * y[:, :dff]
    x_out[...] = x + _rmsnorm(_mm(act, wdn_ref[...]), pw_ref[...])


def _const_spec(a):
    nd = a.ndim
    return pl.BlockSpec(a.shape, lambda g, t: (0,) * nd, pipeline_mode=pl.Buffered(1))


def _row_call(body, name, tm, tiled_ins, group_ins, const_ins, tiled_out_widths, group_out_shapes, scratch):
    G, R, _ = tiled_ins[0].shape
    assert R % tm == 0
    tile_spec = lambda c: pl.BlockSpec((None, tm, c), lambda g, t: (g, t, 0))
    group_spec = lambda n, c: pl.BlockSpec((None, n, c), lambda g, t: (g, 0, 0))
    group_in_spec = lambda n, c: pl.BlockSpec((None, n, c), lambda g, t: (g, 0, 0), pipeline_mode=pl.Buffered(1))
    in_specs = ([tile_spec(a.shape[2]) for a in tiled_ins] + [group_in_spec(*a.shape[1:]) for a in group_ins]
                + [_const_spec(a) for a in const_ins])
    tiled_outs = [c if isinstance(c, tuple) else (c, F32) for c in tiled_out_widths]
    out_specs = [tile_spec(c) for c, _ in tiled_outs] + [group_spec(n, c) for n, c in group_out_shapes]
    out_shape = ([jax.ShapeDtypeStruct((G, R, c), dt) for c, dt in tiled_outs]
                 + [jax.ShapeDtypeStruct((G, n, c), F32) for n, c in group_out_shapes])
    return pl.pallas_call(
        body, name=name, grid=(G, R // tm), in_specs=in_specs, out_specs=out_specs, out_shape=out_shape,
        scratch_shapes=scratch,
        compiler_params=pltpu.CompilerParams(dimension_semantics=("parallel", "arbitrary"),
                                             vmem_limit_bytes=VMEM_LIMIT),
    )(*tiled_ins, *group_ins, *const_ins)


def _hist_pad(n):
    return -(-n // SUBLANES) * SUBLANES


def _tile(ref, p):
    h, j = p
    return ref[LANES * j:LANES * (j + 1), LANES * h:LANES * (h + 1)].astype(F32)


def _scan_states(chained, C, probs, tiles, scr, load, store, block_fn):
    nb = LANES // C
    tt = pl.program_id(2)
    heads = sorted({h for h, _ in probs})
    nt = len(probs) // len(heads)
    outs = {}
    if chained:
        @pl.when(tt == 0)
        def _():
            for h in heads:
                scr[h] = load((h,))

        Ss = [scr[h] for h in heads]
        for j in range(nt):
            sel = [tiles[probs.index((h, j))] for h in heads]
            for i in range(nb):
                os, Ss = block_fn(i, sel, Ss)
                for h, o in zip(heads, os):
                    outs[(h, j, i)] = o
        for h, S in zip(heads, Ss):
            scr[h] = S

        @pl.when(tt == pl.num_programs(2) - 1)
        def _():
            for h, S in zip(heads, Ss):
                store((h,), S)
    else:
        for i in range(nb):
            Ss = [load((j * nb + i, h)) for h, j in probs]
            os, Ss = block_fn(i, tiles, Ss)
            for (h, j), o, S in zip(probs, os, Ss):
                outs[(h, j, i)] = o
                store((j * nb + i, h), S)
    return [jnp.concatenate([outs[(h, j, i)] for i in range(nb)], axis=0) if nb > 1 else outs[(h, j, 0)]
            for h, j in probs]


def _scan_probs(ref):
    return [(h, j) for h in range(ref.shape[1] // LANES) for j in range(ref.shape[0] // LANES)]


def _rwkv_scan_kernel(C, chained, r_ref, lw_ref, k_ref, v_ref, kk_ref, bb_ref, g_ref, rk_ref, lnw_ref, lnb_ref,
                      e_ref, s0_ref, y_out, s_out, scr, bd):
    N = RWKV_N
    e = e_ref[...]
    probs = _scan_probs(r_ref)
    nt = r_ref.shape[0] // LANES
    nb = LANES // C
    get = lambda ref: [_tile(ref, p) for p in probs]
    r, k, v = get(r_ref), get(k_ref), get(v_ref)
    tiles = _rwkv_tile(r, get(lw_ref), k, v, get(kk_ref), get(bb_ref), C)

    bd[...] = jnp.zeros(bd.shape, F32)
    slot = lambda idx: idx[-1] * nt + (idx[0] // nb if len(idx) > 1 else 0)
    heads = lambda idx: (idx[:-1] + (2 * idx[-1],), idx[:-1] + (2 * idx[-1] + 1,))

    def load(idx):
        p = slot(idx)
        h0, h1 = heads(idx)
        bd[p, :N, :N] = s0_ref[h0]
        bd[p, N:, N:] = s0_ref[h1]
        return bd[p]

    def store(idx, S):
        p = slot(idx)
        h0, h1 = heads(idx)
        bd[p] = S
        s_out[h0] = bd[p, :N, :N]
        s_out[h1] = bd[p, N:, N:]

    O = _scan_states(chained, C, probs, tiles, scr, load, store, lambda i, ts, Ss: _rwkv_block(i, C, ts, Ss))
    np_ = len(probs)
    lanes = [slice(LANES * h, LANES * (h + 1)) for h, _ in probs]
    s1 = _head_sum(jnp.concatenate(O + [r[n] * k[n] * rk_ref[:, lanes[n]] for n in range(np_)], axis=0), e)
    d = [O[n] - s1[LANES * n:LANES * (n + 1)] * (1.0 / N) for n in range(np_)]
    s2 = _head_sum(jnp.concatenate([x * x for x in d], axis=0), e)
    for n, (h, j) in enumerate(probs):
        var = s2[LANES * n:LANES * (n + 1)] * (1.0 / N)
        on = d[n] * lax.rsqrt(var + RWKV_LNX_EPS) * lnw_ref[:, lanes[n]] + lnb_ref[:, lanes[n]]
        bonus = s1[LANES * (np_ + n):LANES * (np_ + n + 1)] * v[n]
        y_out[LANES * j:LANES * (j + 1), lanes[n]] = ((on + bonus) * _tile(g_ref, (h, j))).astype(y_out.dtype)


def _gdn_scan_kernel(C, chained, q_ref, k_ref, v_ref, gate_ref, g_ref, beta_ref, nw_ref, s0_ref, y_out, s_out,
                     scr):
    probs = _scan_probs(q_ref)
    get = lambda ref: [_tile(ref, p) for p in probs]
    tiles = _gdn_tile(get(q_ref), get(k_ref), get(v_ref), [g_ref[h, j:j + 1, :] for h, j in probs],
                      [beta_ref[h, j:j + 1, :] for h, j in probs], C)

    def store(idx, S):
        s_out[idx] = S

    O = _scan_states(chained, C, probs, tiles, scr, lambda idx: s0_ref[idx], store,
                     lambda i, ts, Ss: _gdn_block(i, C, ts, Ss))
    for n, (h, j) in enumerate(probs):
        y_out[LANES * j:LANES * (j + 1), LANES * h:LANES * (h + 1)] = (
            _rmsnorm(O[n], nw_ref[...]) * _silu(_tile(gate_ref, (h, j)))).astype(y_out.dtype)


def _gla_scan_kernel(C, chained, q_ref, k_ref, lf_ref, v_ref, gate_ref, nw_ref, s0_ref, y_out, s_out, scr, gc_scr,
                     prod_scr):
    probs = _scan_probs(q_ref)
    tiles = _gla_tile(q_ref, k_ref, lf_ref, v_ref, probs, C, gc_scr, prod_scr)

    def store(idx, ST):
        s_out[idx] = ST.T

    O = _scan_states(chained, C, probs, tiles, scr, lambda idx: s0_ref[idx].T, store,
                     lambda i, ts, Ss: _gla_block(i, C, ts, Ss))
    for n, (h, j) in enumerate(probs):
        y_out[LANES * j:LANES * (j + 1), LANES * h:LANES * (h + 1)] = (
            _rmsnorm(O[n], nw_ref[...]) * _silu(_tile(gate_ref, (h, j)))).astype(y_out.dtype)


def _scan_call(body, name, C, chained, tiles, extra_ins, extra_specs, s0, nheads, extra_scratch=()):
    NB, RB, D = tiles[0].shape
    rows = min(SCAN_ROWS, RB)
    hg = SCAN_HEADS
    assert RB % rows == 0 and rows % LANES == 0 and nheads % hg == 0
    tile_spec = pl.BlockSpec((None, rows, hg * LANES), lambda b, h, t: (b, t, h))
    hs = hg * (s0.shape[1] // nheads)
    if chained:
        s_spec = pl.BlockSpec((None, hs) + s0.shape[2:], lambda b, h, t: (b, h, 0, 0))
    else:
        nt = RB // rows
        s_spec = pl.BlockSpec((rows // C, hs) + s0.shape[2:], lambda b, h, t: (b * nt + t, h, 0, 0))
    return pl.pallas_call(
        functools.partial(body, C, chained), name=name, grid=(NB, nheads // hg, RB // rows),
        in_specs=[tile_spec] * len(tiles) + list(extra_specs) + [s_spec],
        out_specs=[tile_spec, s_spec],
        out_shape=[jax.ShapeDtypeStruct((NB, RB, D), BF16), jax.ShapeDtypeStruct(s0.shape, F32)],
        scratch_shapes=[pltpu.VMEM((hg, LANES, LANES), F32)] + list(extra_scratch),
        compiler_params=pltpu.CompilerParams(dimension_semantics=("parallel", "parallel", "arbitrary"),
                                             vmem_limit_bytes=VMEM_LIMIT),
    )(*tiles, *extra_ins, s0)


class _Group:
    def __init__(self, B, T, time_major):
        self.B, self.T, self.tm_major = B, T, time_major
        if time_major:
            self.G, self.R, self.s = 1, B * T, B
            self.C = -(-T // SUBLANES) * SUBLANES
            assert LANES % self.C == 0 and (B * self.C) % LANES == 0
        else:
            self.G, self.R, self.s = B, T, 1
            self.C = SCAN_CHUNK
            assert T % LANES == 0
        self.chained = not time_major
        self.tile = min(ROW_TILE, self.R)
        self.ffn_tile = self.s if time_major else min(FFN_TILE, self.R)

    def to_rows(self, x):
        if self.tm_major:
            return jnp.swapaxes(x, 0, 1).reshape(1, self.R, x.shape[-1])
        return x

    def hist_to_rows(self, h):
        if self.tm_major:
            return jnp.swapaxes(h, 0, 1).reshape(1, -1, h.shape[-1])
        return h

    def hist_from_rows(self, h, n):
        if self.tm_major:
            return jnp.swapaxes(h.reshape(n, self.B, h.shape[-1]), 0, 1)
        return h

    def to_scan(self, a):
        if not self.tm_major:
            return a
        a = jnp.swapaxes(a.reshape(self.T, self.B, a.shape[-1]), 0, 1)
        a = jnp.pad(a, ((0, 0), (0, self.C - self.T), (0, 0)))
        return a.reshape(1, self.B * self.C, a.shape[-1])

    def from_scan(self, a):
        if not self.tm_major:
            return a
        return self.to_rows(a.reshape(self.B, self.C, a.shape[-1])[:, :self.T])


def _rwkv_layer(grp, x, shift0, S0, v_first, P, j):
    D = x.shape[-1]
    s, tm = grp.s, grp.tile
    N = RWKV_N
    has_vres = v_first is not None
    r2, c2 = _iota2(LANES, LANES)
    e = ((r2 // N) == (c2 // N)).astype(BF16)
    row = lambda a: a.reshape(1, -1)
    tiled = [x] + ([v_first] if has_vres else [])
    consts = [row(P['norm_mix_pre_i']), P['rwkv_mix'][j], P['rwkv_w_rkv'][j].astype(BF16),
              row(P['rwkv_w0'][j]), P['rwkv_w1'][j].astype(BF16), P['rwkv_w2'][j].astype(BF16),
              row(P['rwkv_a0'][j]), P['rwkv_a1'][j].astype(BF16), P['rwkv_a2'][j].astype(BF16),
              P['rwkv_g1'][j].astype(BF16), P['rwkv_g2'][j].astype(BF16),
              row(P['rwkv_k_k'][j]), row(P['rwkv_k_a'][j]), e]
    if has_vres:
        consts += [row(P['rwkv_v0'][j - 1]), P['rwkv_v1'][j - 1].astype(BF16), P['rwkv_v2'][j - 1].astype(BF16)]
    r, lw, k, v, kk, bb, g, shift = _row_call(
        functools.partial(_rwkv_pre_kernel, has_vres, s), f"rwkv_pre_{j}", tm, tiled, [shift0], consts,
        [(D, BF16), D, (D, BF16), D, (D, BF16), (D, BF16), (D, BF16)], [(s, D)],
        [pltpu.VMEM((_hist_pad(s) + tm, D), F32)])
    if not has_vres:
        v_first = v
    tiles = [grp.to_scan(a) for a in (r, lw, k, v, kk, bb, g)]
    nprob = SCAN_HEADS * (min(SCAN_ROWS, tiles[0].shape[1]) // LANES)
    vec_spec = pl.BlockSpec((1, SCAN_HEADS * LANES), lambda b, h, t: (0, h))
    e_spec = pl.BlockSpec((LANES, LANES), lambda b, h, t: (0, 0))
    y, S = _scan_call(_rwkv_scan_kernel, f"rwkv_scan_{j}", grp.C, grp.chained, tiles,
                      [row(P['rwkv_r_k'][j]), row(P['rwkv_lnx_w'][j]), row(P['rwkv_lnx_b'][j]), e],
                      [vec_spec] * 3 + [e_spec], S0, D // LANES,
                      extra_scratch=[pltpu.VMEM((nprob, LANES, LANES), F32)])
    return grp.from_scan(y), P['rwkv_w_o'][j], shift, S, v_first


def _gdn_layer(grp, x, conv0, S0, P, j):
    D = x.shape[-1]
    s, tm = grp.s, grp.tile
    w_in = P['gdn_w_in'][j]
    cw = P['gdn_conv_w'][j]
    nh, cdim = cw.shape[0] - 1, cw.shape[1]
    H = S0.shape[1]
    assert grp.T >= nh
    row = lambda a: a.reshape(1, -1)
    lane_pad = lambda a: jnp.pad(a, ((0, 0), (0, LANES - a.shape[1])))
    zeros = jnp.zeros((1, H), F32)
    consts = [row(P['norm_mix_pre_i']), w_in[:, :cdim].astype(BF16), w_in[:, cdim:cdim + D].astype(BF16),
              lane_pad(w_in[:, cdim + D:]).astype(BF16), cw,
              lane_pad(jnp.concatenate([zeros, row(P['gdn_a_log'][j])], axis=1)),
              lane_pad(jnp.concatenate([zeros, row(P['gdn_dt_bias'][j])], axis=1))]
    q, k, v, gate, bg, conv_new = _row_call(
        functools.partial(_gdn_pre_kernel, s, H), f"gdn_pre_{j}", tm, [x], [conv0], consts,
        [(D, BF16)] * 4 + [LANES], [(nh * s, cdim)], [pltpu.VMEM((_hist_pad(nh * s) + tm, cdim), F32)])
    tiles = [grp.to_scan(a) for a in (q, k, v, gate)]
    NB, RB, _ = tiles[0].shape
    rows = min(SCAN_ROWS, RB)
    nt = RB // rows
    bg = grp.to_scan(bg[:, :, :2 * H]).reshape(NB * RB, 2 * H).T.reshape(2 * H, NB * nt, rows // LANES, LANES)
    hg = SCAN_HEADS
    g_spec = pl.BlockSpec((hg, None, rows // LANES, LANES), lambda b, h, t: (H // hg + h, b * nt + t, 0, 0))
    beta_spec = pl.BlockSpec((hg, None, rows // LANES, LANES), lambda b, h, t: (h, b * nt + t, 0, 0))
    vec_spec = pl.BlockSpec((1, LANES), lambda b, h, t: (0, 0))
    y, S = _scan_call(_gdn_scan_kernel, f"gdn_scan_{j}", grp.C, grp.chained, tiles,
                      [bg, bg, row(P['gdn_norm_w'][j])], [g_spec, beta_spec, vec_spec], S0, H)
    return grp.from_scan(y), P['gdn_w_o'][j], conv_new, S


def _hgrn_layer(grp, x, S0, P, i, j):
    D = x.shape[-1]
    tm = grp.tile
    H = S0.shape[1]
    row = lambda a: a.reshape(1, -1)
    consts = [row(P['norm_mix_pre_i']), P['hgrn_w_in'][j].astype(BF16), P['hgrn_lb']]
    q, k, lf, v, gate = _row_call(functools.partial(_hgrn_pre_kernel, i), f"hgrn_pre_{j}", tm, [x], [], consts,
                                  [D, D, D, (D, BF16), (D, BF16)], [], [])
    tiles = [grp.to_scan(a) for a in (q, k, lf, v, gate)]
    nprob = SCAN_HEADS * (min(SCAN_ROWS, tiles[0].shape[1]) // LANES)
    vec_spec = pl.BlockSpec((1, LANES), lambda b, h, t: (0, 0))
    y, S = _scan_call(_gla_scan_kernel, f"hgrn_scan_{j}", grp.C, grp.chained, tiles, [row(P['hgrn_norm_w'][j])],
                      [vec_spec], S0, H,
                      extra_scratch=[pltpu.VMEM((nprob, LANES, LANES), F32),
                                     pltpu.VMEM((nprob, LANES, GLA_SUB * LANES), F32)])
    return grp.from_scan(y), P['hgrn_w_o'][j], S


def _trunk(grp, x, shift0, wkv0, gconv0, gS0, hS0, fconv0, P):
    D = x.shape[-1]
    depth = P['norm_mix_pre'].shape[0]
    row = lambda a: a.reshape(1, -1)
    x = grp.to_rows(x)
    v_first = None
    shift, wkv, gconv, gS, hS, fconv = [], [], [], [], [], []
    for i in range(depth):
        kind, j = i % 3, i // 3
        P = dict(P, norm_mix_pre_i=P['norm_mix_pre'][i])
        if kind == 0:
            y, w_o, s_shift, s_wkv, v_first = _rwkv_layer(grp, x, grp.hist_to_rows(shift0[j][:, None]), wkv0[j],
                                                         v_first, P, j)
            shift.append(grp.hist_from_rows(s_shift, 1)[:, 0])
            wkv.append(s_wkv)
        elif kind == 1:
            y, w_o, c_new, s_new = _gdn_layer(grp, x, grp.hist_to_rows(gconv0[j]), gS0[j], P, j)
            gconv.append(grp.hist_from_rows(c_new, gconv0.shape[2]))
            gS.append(s_new)
        else:
            y, w_o, s_new = _hgrn_layer(grp, x, hS0[j], P, i, j)
            hS.append(s_new)
        (x,) = _row_call(_out_proj_kernel, f"out_proj_{i}", grp.tile, [y, x], [],
                         [w_o.astype(BF16), row(P['norm_mix_post'][i])], [D], [], [])
        nh = P['ffn_conv_w'].shape[1] - 1
        dff2 = P['ffn_w_up'].shape[2]
        x, c_new = _row_call(
            functools.partial(_ffn_kernel, grp.s), f"ffn_{i}", grp.ffn_tile, [x], [grp.hist_to_rows(fconv0[i])],
            [row(P['norm_ffn_pre'][i]), P['ffn_w_up'][i].astype(BF16), P['ffn_conv_w'][i],
             P['ffn_w_down'][i].astype(BF16), row(P['norm_ffn_post'][i])],
            [D], [(nh * grp.s, dff2)], [pltpu.VMEM((_hist_pad(nh * grp.s) + grp.ffn_tile, dff2), F32)])
        fconv.append(grp.hist_from_rows(c_new, nh))
    y = x.reshape(grp.T, grp.B, D).swapaxes(0, 1) if grp.tm_major else x
    return y, (jnp.stack(shift), jnp.stack(wkv), jnp.stack(gconv), jnp.stack(gS), jnp.stack(hS), jnp.stack(fconv))


def kernel(x_prompt, x_sample, state_rwkv_shift, state_rwkv_wkv, state_gdn_conv, state_gdn_S, state_hgrn_S, state_ffn_conv, norm_mix_pre, norm_mix_post, norm_ffn_pre, norm_ffn_post, rwkv_mix, rwkv_w_rkv, rwkv_w0, rwkv_w1, rwkv_w2, rwkv_a0, rwkv_a1, rwkv_a2, rwkv_v0, rwkv_v1, rwkv_v2, rwkv_g1, rwkv_g2, rwkv_k_k, rwkv_k_a, rwkv_r_k, rwkv_lnx_w, rwkv_lnx_b, rwkv_w_o, gdn_w_in, gdn_conv_w, gdn_a_log, gdn_dt_bias, gdn_norm_w, gdn_w_o, hgrn_w_in, hgrn_lb, hgrn_norm_w, hgrn_w_o, ffn_w_up, ffn_conv_w, ffn_w_down):
    P = dict(norm_mix_pre=norm_mix_pre, norm_mix_post=norm_mix_post, norm_ffn_pre=norm_ffn_pre,
             norm_ffn_post=norm_ffn_post, rwkv_mix=rwkv_mix, rwkv_w_rkv=rwkv_w_rkv, rwkv_w0=rwkv_w0,
             rwkv_w1=rwkv_w1, rwkv_w2=rwkv_w2, rwkv_a0=rwkv_a0, rwkv_a1=rwkv_a1, rwkv_a2=rwkv_a2,
             rwkv_v0=rwkv_v0, rwkv_v1=rwkv_v1, rwkv_v2=rwkv_v2, rwkv_g1=rwkv_g1, rwkv_g2=rwkv_g2,
             rwkv_k_k=rwkv_k_k, rwkv_k_a=rwkv_k_a, rwkv_r_k=rwkv_r_k, rwkv_lnx_w=rwkv_lnx_w,
             rwkv_lnx_b=rwkv_lnx_b, rwkv_w_o=rwkv_w_o, gdn_w_in=gdn_w_in, gdn_conv_w=gdn_conv_w,
             gdn_a_log=gdn_a_log, gdn_dt_bias=gdn_dt_bias, gdn_norm_w=gdn_norm_w, gdn_w_o=gdn_w_o,
             hgrn_w_in=hgrn_w_in, hgrn_lb=hgrn_lb, hgrn_norm_w=hgrn_norm_w, hgrn_w_o=hgrn_w_o,
             ffn_w_up=ffn_w_up, ffn_conv_w=ffn_conv_w, ffn_w_down=ffn_w_down)
    Bp, Tp, _ = x_prompt.shape
    Bs, Ts, _ = x_sample.shape
    zero_like = lambda st: jnp.zeros((st.shape[0], Bp) + st.shape[2:], st.dtype)
    y_p, (p_shift, p_wkv, p_gconv, p_gS, p_hS, p_fconv) = _trunk(
        _Group(Bp, Tp, False), x_prompt, zero_like(state_rwkv_shift), zero_like(state_rwkv_wkv),
        zero_like(state_gdn_conv), zero_like(state_gdn_S), zero_like(state_hgrn_S), zero_like(state_ffn_conv), P)
    y_s, (s_shift, s_wkv, s_gconv, s_gS, s_hS, s_fconv) = _trunk(
        _Group(Bs, Ts, True), x_sample, state_rwkv_shift, state_rwkv_wkv, state_gdn_conv, state_gdn_S,
        state_hgrn_S, state_ffn_conv, P)
    return (y_p, y_s, p_shift, s_shift, p_wkv, s_wkv, p_gconv, s_gconv,
            p_gS, s_gS, p_hS, s_hS, p_fconv, s_fconv)
```

```python
import functools
import math

import jax
import jax.numpy as jnp
from jax import lax
from jax.experimental import pallas as pl
from jax.experimental.pallas import tpu as pltpu

F32 = jnp.float32
BF16 = jnp.bfloat16

NORM_EPS = 1e-6
L2_EPS = 1e-6
RWKV_LNX_EPS = 64e-5
RWKV_N = 64
LANES = 128
SUBLANES = 8
VMEM_LIMIT = 56 * 1024 * 1024
ROW_TILE = 256
FFN_TILE = 256
SCAN_ROWS = 256
SCAN_HEADS = 4
SCAN_CHUNK = 64
GLA_SUB = 8


def _dg(a, b, ca, cb):
    return lax.dot_general(a, b, (((ca,), (cb,)), ((), ())), preferred_element_type=F32)


def _mm(a, b):
    return _dg(a.astype(BF16), b.astype(BF16), 1, 0)


def _mm_nt(a, b):
    return _dg(a.astype(BF16), b.astype(BF16), 1, 1)


def _mm_tn(a, b):
    return _dg(a.astype(BF16), b.astype(BF16), 0, 0)


def _split3(x):
    hi = x.astype(BF16)
    r1 = x - hi.astype(F32)
    mid = r1.astype(BF16)
    lo = (r1 - mid.astype(F32)).astype(BF16)
    return hi, mid, lo


def _mm_xl(m, x):
    h, mi, lo = _split3(x)
    m = m.astype(BF16)
    return _dg(m, h, 1, 0) + (_dg(m, mi, 1, 0) + _dg(m, lo, 1, 0))


def _iota2(n, m):
    return (lax.broadcasted_iota(jnp.int32, (n, m), 0), lax.broadcasted_iota(jnp.int32, (n, m), 1))


def _sigmoid(x):
    return 1.0 / (1.0 + jnp.exp(-x))


def _silu(x):
    return x * _sigmoid(x)


def _softplus(x):
    return jnp.maximum(x, 0.0) + jnp.log(1.0 + jnp.exp(-jnp.abs(x)))


def _rmsnorm(x, w):
    return x * lax.rsqrt(jnp.mean(x * x, axis=-1, keepdims=True) + NORM_EPS) * w


def _head_sum(x, e):
    return _mm(x, e)


def _unit_lower_inv(Ls, C):
    n = Ls[0].shape[0]
    ri, ci = _iota2(n, n)
    eye = (ri == ci).astype(F32)
    Xs = [eye - L for L in Ls]
    Ps = list(Ls)
    m = 2
    while m < C:
        Ps = [_mm(P, P) for P in Ps]
        Xs = [X + _mm(X, P) for X, P in zip(Xs, Ps)]
        m *= 2
    return Xs


def _tile_masks(C):
    sh = C.bit_length() - 1
    ri, ci = _iota2(LANES, LANES)
    same = jnp.right_shift(ri, sh) == jnp.right_shift(ci, sh)
    return ri, ci, same, same & (ri >= ci), same & (ri > ci)


def _place_rows(x, r0, n):
    parts = []
    if r0:
        parts.append(jnp.zeros((r0, x.shape[1]), x.dtype))
    parts.append(x)
    if n - r0 - x.shape[0]:
        parts.append(jnp.zeros((n - r0 - x.shape[0], x.shape[1]), x.dtype))
    return jnp.concatenate(parts, axis=0) if len(parts) > 1 else x


def _rwkv_tile(r, lw, k, v, kk, bb, C):
    ri, ci, same, tril, stril = _tile_masks(C)
    lane = lax.broadcasted_iota(jnp.int32, (1, LANES), 1)
    m = [(lane < RWKV_N).astype(F32), (lane >= RWKV_N).astype(F32)]
    trilf = tril.astype(F32)
    n = len(r)
    gc = [_mm_xl(trilf, x) for x in lw]
    at = [-kk[g] * jnp.exp(gc[g] - lw[g]) for g in range(n)]
    rt = [r[g] * jnp.exp(gc[g]) for g in range(n)]
    sc = []
    for g in range(n):
        e_neg = jnp.exp(-gc[g])
        lhs = jnp.concatenate([at[g] * m[0], at[g] * m[1], rt[g] * m[0], rt[g] * m[1]], axis=0)
        sc.append(_mm_nt(lhs, jnp.concatenate([bb[g] * e_neg, k[g] * e_neg], axis=0)))
    gh = [(g, hd) for g in range(n) for hd in range(2)]
    a_blk = [sc[g][LANES * hd:LANES * (hd + 1)] for g, hd in gh]
    r_blk = [sc[g][LANES * (2 + hd):LANES * (3 + hd)] for g, hd in gh]
    A_rb = [jnp.where(tril, x[:, :LANES], 0.0) for x in r_blk]
    A_rk = [jnp.where(tril, x[:, LANES:], 0.0) for x in r_blk]
    Tinv = _unit_lower_inv([jnp.where(stril, -x[:, :LANES], 0.0) for x in a_blk], C)
    akv = [_mm(jnp.where(stril, a_blk[p][:, LANES:], 0.0), v[g]) for p, (g, hd) in enumerate(gh)]
    y = [_mm(Tinv[p], jnp.concatenate([at[g] * m[hd], akv[p]], axis=1)) for p, (g, hd) in enumerate(gh)]
    o0 = [_mm(jnp.concatenate([A_rk[p], A_rb[p]], axis=1), jnp.concatenate([v[g], y[p][:, LANES:]], axis=0))
          for p, (g, hd) in enumerate(gh)]
    out = []
    for g in range(n):
        p0, p1 = 2 * g, 2 * g + 1
        out.append(dict(gc=gc[g], rt=rt[g], m=m, Wa=y[p0][:, :LANES] + y[p1][:, :LANES],
                        U0=y[p0][:, LANES:] * m[0] + y[p1][:, LANES:] * m[1],
                        O0=o0[p0] * m[0] + o0[p1] * m[1],
                        A_rb=jnp.concatenate([A_rb[p0], A_rb[p1]], axis=0), k=k[g], v=v[g], bb=bb[g]))
    return out


def _rwkv_block(i, C, ts, Ss):
    sl = slice(C * i, C * (i + 1))
    sl1 = slice(LANES + C * i, LANES + C * (i + 1))
    r2, c2 = _iota2(LANES, LANES)
    blk = ((r2 >= RWKV_N) == (c2 >= RWKV_N)).astype(F32)
    xs = [_mm_nt(jnp.concatenate([t['Wa'][sl], t['rt'][sl]], axis=0), S) for t, S in zip(ts, Ss)]
    S_new, ax = [], []
    for t, S, x in zip(ts, Ss, xs):
        X = x[:C]
        U = t['U0'][sl] + X
        gl = t['gc'][C * (i + 1) - 1:C * (i + 1), :]
        e_last = jnp.exp(gl - t['gc'][sl])
        S_new.append(S * jnp.exp(gl) + blk * _mm_tn(
            jnp.concatenate([U, t['v'][sl]], axis=0),
            jnp.concatenate([t['bb'][sl] * e_last, t['k'][sl] * e_last], axis=0)))
        ax.append(_mm(jnp.concatenate([t['A_rb'][sl], t['A_rb'][sl1]], axis=0), _place_rows(X, C * i, LANES)))
    o = [t['O0'][sl] + x[C:] + a[:C] * t['m'][0] + a[C:] * t['m'][1] for t, x, a in zip(ts, xs, ax)]
    return o, S_new


def _gdn_tile(q, k, v, g_row, beta_row, C):
    ri, ci, same, tril, stril = _tile_masks(C)
    eye = ri == ci
    lsum = lambda msk, x: jnp.sum(jnp.where(msk, x, 0.0), axis=1, keepdims=True)
    n = len(q)
    pre = []
    for g in range(n):
        g_b = jnp.broadcast_to(g_row[g], (LANES, LANES))
        beta_col = lsum(eye, jnp.broadcast_to(beta_row[g], (LANES, LANES)))
        g_col = lsum(eye, g_b)
        gc_col = lsum(tril, g_b)
        gl_col = lsum(same, g_b)
        gc_row = jnp.sum(jnp.where(same & (ri <= ci), jnp.broadcast_to(g_col, (LANES, LANES)), 0.0),
                         axis=0, keepdims=True)
        decay = jnp.where(tril, jnp.exp(jnp.minimum(gc_col - gc_row, 0.0)), 0.0)
        pre.append(dict(beta=beta_col, gc=gc_col, gl=gl_col, decay=decay, kb=k[g] * beta_col))
    sc = [_mm_nt(jnp.concatenate([pre[g]['kb'], q[g]], axis=0), k[g]) for g in range(n)]
    Tinv = _unit_lower_inv([jnp.where(stril, sc[g][:LANES] * pre[g]['decay'], 0.0) for g in range(n)], C)
    out = []
    for g in range(n):
        p = pre[g]
        e_gc = jnp.exp(p['gc'])
        uw = _mm(Tinv[g], jnp.concatenate([v[g] * p['beta'], p['kb'] * e_gc], axis=1))
        out.append(dict(u=uw[:, :LANES], wk=uw[:, LANES:], qk=sc[g][LANES:] * p['decay'], qg=q[g] * e_gc,
                        kd=k[g] * jnp.exp(p['gl'] - p['gc']), egl=jnp.exp(p['gl'])))
    return out


def _gdn_block(i, C, ts, Ss):
    sl = slice(C * i, C * (i + 1))
    ws = [_mm(jnp.concatenate([t['wk'][sl], t['qg'][sl]], axis=0), S) for t, S in zip(ts, Ss)]
    v_new = [t['u'][sl] - w[:C] for t, w in zip(ts, ws)]
    S_new = [S * t['egl'][C * i:C * i + 1, :] + _mm_tn(t['kd'][sl], vn) for t, S, vn in zip(ts, Ss, v_new)]
    o = [w[C:] + _mm(t['qk'][sl], _place_rows(vn, C * i, LANES)) for t, w, vn in zip(ts, ws, v_new)]
    return o, S_new


def _gla_tile(q_ref, k_ref, lf_ref, v_ref, probs, C, gc_scr, prod_scr):
    n = len(probs)
    nsb = LANES // GLA_SUB
    ri, ci, same, tril, _ = _tile_masks(C)
    sub = GLA_SUB.bit_length() - 1
    rsub, csub = jnp.right_shift(ri, sub), jnp.right_shift(ci, sub)
    trilf = tril.astype(F32)
    get = lambda ref: [_tile(ref, p) for p in probs]
    q, k, v = get(q_ref), get(k_ref), get(v_ref)
    gc = [_mm_xl(trilf, x) for x in get(lf_ref)]
    rk, ck = _iota2(GLA_SUB * LANES, LANES)
    sel = (jnp.right_shift(rk, LANES.bit_length() - 1) == ck).astype(BF16)
    for g, (h, j) in enumerate(probs):
        gc_scr[g] = gc[g]
        for I in range(nsb):
            s0 = GLA_SUB * I
            ksub = k[g][s0:s0 + GLA_SUB]
            gsub = gc[g][s0:s0 + GLA_SUB]
            for ii in range(GLA_SUB):
                i = s0 + ii
                gi = gc_scr[g, i:i + 1, :]
                qi = q_ref[LANES * j + i:LANES * j + i + 1, LANES * h:LANES * (h + 1)]
                prod_scr[g, s0:s0 + GLA_SUB, LANES * ii:LANES * (ii + 1)] = (
                    qi * ksub * jnp.exp(jnp.minimum(gi - gsub, 0.0)))
    d = [_mm(prod_scr[g], sel) for g in range(n)]
    AT = []
    for g in range(n):
        strips = [d[g][:GLA_SUB]] + [pltpu.roll(d[g][GLA_SUB * I:GLA_SUB * (I + 1)], GLA_SUB * I, axis=1)
                                     for I in range(1, nsb)]
        AT.append(jnp.where((rsub == csub) & (ri <= ci), jnp.concatenate(strips, axis=0), 0.0))
    nsub = C // GLA_SUB
    if nsub > 1:
        for g in range(n):
            off = []
            for b in range(LANES // C):
                b0 = C * b
                kblk = k[g][b0:b0 + C]
                gblk = gc[g][b0:b0 + C]
                acc = jnp.zeros((C, LANES), F32)
                for J in range(1, nsub):
                    s0 = b0 + GLA_SUB * J
                    ref = gc_scr[g, s0 - 1:s0, :]
                    khat = kblk * jnp.exp(jnp.minimum(ref - gblk, 0.0))
                    qhat = q[g][s0:s0 + GLA_SUB] * jnp.exp(gc[g][s0:s0 + GLA_SUB] - ref)
                    acc = acc + _mm_nt(khat, _place_rows(qhat, s0, LANES))
                off.append(acc)
            AT[g] = AT[g] + jnp.where(same & (rsub < csub), jnp.concatenate(off, axis=0), 0.0)
    o_intra = [_mm_tn(AT[g], v[g]) for g in range(n)]
    return [dict(gc=gc[g], k=k[g], v=v[g], qg=q[g] * jnp.exp(gc[g]), o_intra=o_intra[g]) for g in range(n)]


def _gla_block(i, C, ts, STs):
    sl = slice(C * i, C * (i + 1))
    o, ST_new = [], []
    for t, ST in zip(ts, STs):
        gl = t['gc'][C * (i + 1) - 1:C * (i + 1), :]
        o.append(t['o_intra'][sl] + _mm_nt(t['qg'][sl], ST))
        ST_new.append(ST * jnp.exp(gl) + _mm_tn(t['v'][sl], t['k'][sl] * jnp.exp(gl - t['gc'][sl])))
    return o, ST_new


def _shifted(ext, hp, s, tm, back):
    return ext[hp - back * s:hp - back * s + tm, :]


def _rwkv_pre_kernel(has_vres, s, *refs):
    if has_vres:
        (x_ref, vf_ref, sh0_ref, nw_ref, mix_ref, wrkv_ref, w0_ref, w1_ref, w2_ref, a0_ref, a1_ref, a2_ref,
         g1_ref, g2_ref, kkw_ref, ka_ref, e_ref, v0_ref, v1_ref, v2_ref,
         r_out, lw_out, k_out, v_out, kk_out, bb_out, g_out, sh_out, ext) = refs
    else:
        (x_ref, sh0_ref, nw_ref, mix_ref, wrkv_ref, w0_ref, w1_ref, w2_ref, a0_ref, a1_ref, a2_ref,
         g1_ref, g2_ref, kkw_ref, ka_ref, e_ref,
         r_out, lw_out, k_out, v_out, kk_out, bb_out, g_out, sh_out, ext) = refs
    t = pl.program_id(1)
    tm, D = x_ref.shape
    hp = ext.shape[0] - tm
    h = _rmsnorm(x_ref[...], nw_ref[...])

    @pl.when(t == 0)
    def _():
        ext[hp - s:hp, :] = sh0_ref[...]

    ext[hp:, :] = h
    d = _shifted(ext, hp, s, tm, 1) - h
    ext[hp - s:hp, :] = h[tm - s:, :]

    @pl.when(t == pl.num_programs(1) - 1)
    def _():
        sh_out[...] = h[tm - s:, :]

    mixed = lambda i: h + d * mix_ref[i:i + 1, :]
    r = _mm(mixed(0), wrkv_ref[0])
    z = w0_ref[...] + _mm(jnp.tanh(_mm(mixed(1), w1_ref[...])), w2_ref[...])
    lw_out[...] = -math.exp(-0.5) * _sigmoid(z)
    k = _mm(mixed(2), wrkv_ref[1])
    xv = mixed(3)
    v = _mm(xv, wrkv_ref[2])
    a = _sigmoid(a0_ref[...] + _mm(_mm(mixed(4), a1_ref[...]), a2_ref[...]))
    if has_vres:
        gate_v = _sigmoid(v0_ref[...] + _mm(_mm(xv, v1_ref[...]), v2_ref[...]))
        v = v + (vf_ref[...] - v) * gate_v
    g_out[...] = _mm(_sigmoid(_mm(mixed(5), g1_ref[...])), g2_ref[...]).astype(g_out.dtype)
    r_out[...] = r.astype(r_out.dtype)
    v_out[...] = v
    kkraw = k * kkw_ref[...]
    e = e_ref[...]
    for c in range(D // LANES):
        sl = slice(c * LANES, (c + 1) * LANES)
        kc = kkraw[:, sl]
        kkn = kc * lax.rsqrt(_head_sum(kc * kc, e) + L2_EPS)
        kk_out[:, sl] = kkn.astype(kk_out.dtype)
        bb_out[:, sl] = (kkn * a[:, sl]).astype(bb_out.dtype)
    k_out[...] = (k * (1.0 + (a - 1.0) * ka_ref[...])).astype(k_out.dtype)


def _gdn_pre_kernel(s, nheads, x_ref, c0_ref, nw_ref, wqkv_ref, wgate_ref, wbg_ref, cw_ref, alog_ref, dtb_ref,
                    q_out, k_out, v_out, gate_out, bg_out, cnew_out, ext):
    t = pl.program_id(1)
    tm, D = x_ref.shape
    hp = ext.shape[0] - tm
    nh = cw_ref.shape[0] - 1
    h = _rmsnorm(x_ref[...], nw_ref[...])

    @pl.when(t == 0)
    def _():
        ext[hp - nh * s:hp, :] = c0_ref[...]

    ext[hp:, :] = _mm(h, wqkv_ref[...])
    y = _shifted(ext, hp, s, tm, 0) * cw_ref[nh:nh + 1, :]
    for j in range(nh):
        y = y + _shifted(ext, hp, s, tm, nh - j) * cw_ref[j:j + 1, :]
    hist = ext[hp + tm - nh * s:hp + tm, :]

    @pl.when(t == pl.num_programs(1) - 1)
    def _():
        cnew_out[...] = hist

    ext[hp - nh * s:hp, :] = hist
    y = _silu(y)
    qk_w = D
    for c in range(qk_w // LANES):
        sl = slice(c * LANES, (c + 1) * LANES)
        qc = y[:, sl]
        q_out[:, sl] = (qc * (lax.rsqrt(jnp.sum(qc * qc, axis=-1, keepdims=True) + L2_EPS) * LANES ** -0.5)
                        ).astype(q_out.dtype)
        sl2 = slice(qk_w + c * LANES, qk_w + (c + 1) * LANES)
        kc = y[:, sl2]
        k_out[:, sl] = (kc * lax.rsqrt(jnp.sum(kc * kc, axis=-1, keepdims=True) + L2_EPS)).astype(k_out.dtype)
    v_out[...] = y[:, 2 * qk_w:].astype(v_out.dtype)
    gate_out[...] = _mm(h, wgate_ref[...]).astype(gate_out.dtype)
    rest = _mm(h, wbg_ref[...])
    lane = lax.broadcasted_iota(jnp.int32, rest.shape, 1)
    bg_out[...] = jnp.where(lane < nheads, _sigmoid(rest),
                            -jnp.exp(alog_ref[...]) * _softplus(rest + dtb_ref[...]))


def _hgrn_pre_kernel(layer, x_ref, nw_ref, win_ref, lb_ref, q_out, k_out, lf_out, v_out, gate_out):
    D = x_ref.shape[1]
    h = _rmsnorm(x_ref[...], nw_ref[...])
    lbp = lb_ref[...]
    ex = jnp.exp(lbp - jnp.max(lbp, axis=0, keepdims=True))
    soft = ex / jnp.sum(ex, axis=0, keepdims=True)
    row = lax.broadcasted_iota(jnp.int32, soft.shape, 0)
    lb = jnp.sum(jnp.where((row >= 1) & (row <= layer), soft, 0.0), axis=0, keepdims=True)
    p = _mm(h, win_ref[...])
    q_out[...] = _silu(p[:, :D])
    f = lb + (1.0 - lb) * _sigmoid(p[:, D:2 * D])
    k_out[...] = 1.0 - f
    lf_out[...] = jnp.log(f)
    v_out[...] = p[:, 2 * D:3 * D].astype(v_out.dtype)
    gate_out[...] = p[:, 3 * D:].astype(gate_out.dtype)


def _ffn_kernel(s, y_ref, x_ref, c0_ref, wo_ref, mpw_ref, nw_ref, wup_ref, cw_ref, wdn_ref, pw_ref, x_out, cnew_out,
                ext):
    t = pl.program_id(1)
    tm, D = x_ref.shape
    hp = ext.shape[0] - tm
    nh = cw_ref.shape[0] - 1
    dff = wdn_ref.shape[0]
    x = x_ref[...] + _rmsnorm(_mm(y_ref[...], wo_ref[...]), mpw_ref[...])
    h = _rmsnorm(x, nw_ref[...])

    @pl.when(t == 0)
    def _():
        ext[hp - nh * s:hp, :] = c0_ref[...]

    ext[hp:, :] = _mm(h, wup_ref[...])
    y = _shifted(ext, hp, s, tm, 0) * cw_ref[nh:nh + 1, :]
    for j in range(nh):
        y = y + _shifted(ext, hp, s, tm, nh - j) * cw_ref[j:j + 1, :]
    hist = ext[hp + tm - nh * s:hp + tm, :]

    @pl.when(t == pl.num_programs(1) - 1)
    def _():
        cnew_out[...] = hist

    ext[hp - nh * s:hp, :] = hist
    act = _silu(y[:, dff:]) * y[:, :dff]
    x_out[...] = x + _rmsnorm(_mm(act, wdn_ref[...]), pw_ref[...])


def _const_spec(a):
    if isinstance(a, tuple):
        a, i = a
        nd = a.ndim - 1
        return pl.BlockSpec((None,) + a.shape[1:], lambda g, t: (i,) + (0,) * nd, pipeline_mode=pl.Buffered(1))
    nd = a.ndim
    return pl.BlockSpec(a.shape, lambda g, t: (0,) * nd, pipeline_mode=pl.Buffered(1))


def _row_call(body, name, tm, tiled_ins, group_ins, const_ins, tiled_out_widths, group_out_shapes, scratch):
    G, R, _ = tiled_ins[0].shape
    assert R % tm == 0
    tile_spec = lambda c: pl.BlockSpec((None, tm, c), lambda g, t: (g, t, 0))
    group_spec = lambda n, c: pl.BlockSpec((None, n, c), lambda g, t: (g, 0, 0))
    group_in_spec = lambda n, c: pl.BlockSpec((None, n, c), lambda g, t: (g, 0, 0), pipeline_mode=pl.Buffered(1))
    in_specs = ([tile_spec(a.shape[2]) for a in tiled_ins] + [group_in_spec(*a.shape[1:]) for a in group_ins]
                + [_const_spec(a) for a in const_ins])
    tiled_outs = [c if isinstance(c, tuple) else (c, F32) for c in tiled_out_widths]
    out_specs = [tile_spec(c) for c, _ in tiled_outs] + [group_spec(n, c) for n, c in group_out_shapes]
    out_shape = ([jax.ShapeDtypeStruct((G, R, c), dt) for c, dt in tiled_outs]
                 + [jax.ShapeDtypeStruct((G, n, c), F32) for n, c in group_out_shapes])
    return pl.pallas_call(
        body, name=name, grid=(G, R // tm), in_specs=in_specs, out_specs=out_specs, out_shape=out_shape,
        scratch_shapes=scratch,
        compiler_params=pltpu.CompilerParams(dimension_semantics=("parallel", "arbitrary"),
                                             vmem_limit_bytes=VMEM_LIMIT),
    )(*tiled_ins, *group_ins, *[a[0] if isinstance(a, tuple) else a for a in const_ins])


def _hist_pad(n):
    return -(-n // SUBLANES) * SUBLANES


def _tile(ref, p):
    h, j = p
    return ref[LANES * j:LANES * (j + 1), LANES * h:LANES * (h + 1)].astype(F32)


def _scan_states(chained, C, probs, tiles, scr, load, store, block_fn):
    nb = LANES // C
    tt = pl.program_id(2)
    heads = sorted({h for h, _ in probs})
    nt = len(probs) // len(heads)
    outs = {}
    if chained:
        @pl.when(tt == 0)
        def _():
            for h in heads:
                scr[h] = load((h,))

        Ss = [scr[h] for h in heads]
        for j in range(nt):
            sel = [tiles[probs.index((h, j))] for h in heads]
            for i in range(nb):
                os, Ss = block_fn(i, sel, Ss)
                for h, o in zip(heads, os):
                    outs[(h, j, i)] = o
        for h, S in zip(heads, Ss):
            scr[h] = S

        @pl.when(tt == pl.num_programs(2) - 1)
        def _():
            for h, S in zip(heads, Ss):
                store((h,), S)
    else:
        for i in range(nb):
            Ss = [load((j * nb + i, h)) for h, j in probs]
            os, Ss = block_fn(i, tiles, Ss)
            for (h, j), o, S in zip(probs, os, Ss):
                outs[(h, j, i)] = o
                store((j * nb + i, h), S)
    return [jnp.concatenate([outs[(h, j, i)] for i in range(nb)], axis=0) if nb > 1 else outs[(h, j, 0)]
            for h, j in probs]


def _scan_probs(ref):
    return [(h, j) for h in range(ref.shape[1] // LANES) for j in range(ref.shape[0] // LANES)]


def _rwkv_scan_kernel(C, chained, r_ref, lw_ref, k_ref, v_ref, kk_ref, bb_ref, g_ref, rk_ref, lnw_ref, lnb_ref,
                      e_ref, s0_ref, y_out, s_out, scr, bd):
    N = RWKV_N
    e = e_ref[...]
    probs = _scan_probs(r_ref)
    nt = r_ref.shape[0] // LANES
    nb = LANES // C
    get = lambda ref: [_tile(ref, p) for p in probs]
    r, k, v = get(r_ref), get(k_ref), get(v_ref)
    tiles = _rwkv_tile(r, get(lw_ref), k, v, get(kk_ref), get(bb_ref), C)

    bd[...] = jnp.zeros(bd.shape, F32)
    slot = lambda idx: idx[-1] * nt + (idx[0] // nb if len(idx) > 1 else 0)
    heads = lambda idx: (idx[:-1] + (2 * idx[-1],), idx[:-1] + (2 * idx[-1] + 1,))

    def load(idx):
        p = slot(idx)
        h0, h1 = heads(idx)
        bd[p, :N, :N] = s0_ref[h0]
        bd[p, N:, N:] = s0_ref[h1]
        return bd[p]

    def store(idx, S):
        p = slot(idx)
        h0, h1 = heads(idx)
        bd[p] = S
        s_out[h0] = bd[p, :N, :N]
        s_out[h1] = bd[p, N:, N:]

    O = _scan_states(chained, C, probs, tiles, scr, load, store, lambda i, ts, Ss: _rwkv_block(i, C, ts, Ss))
    np_ = len(probs)
    lanes = [slice(LANES * h, LANES * (h + 1)) for h, _ in probs]
    s1 = _head_sum(jnp.concatenate(O + [r[n] * k[n] * rk_ref[:, lanes[n]] for n in range(np_)], axis=0), e)
    d = [O[n] - s1[LANES * n:LANES * (n + 1)] * (1.0 / N) for n in range(np_)]
    s2 = _head_sum(jnp.concatenate([x * x for x in d], axis=0), e)
    for n, (h, j) in enumerate(probs):
        var = s2[LANES * n:LANES * (n + 1)] * (1.0 / N)
        on = d[n] * lax.rsqrt(var + RWKV_LNX_EPS) * lnw_ref[:, lanes[n]] + lnb_ref[:, lanes[n]]
        bonus = s1[LANES * (np_ + n):LANES * (np_ + n + 1)] * v[n]
        y_out[LANES * j:LANES * (j + 1), lanes[n]] = ((on + bonus) * _tile(g_ref, (h, j))).astype(y_out.dtype)


def _gdn_scan_kernel(C, chained, q_ref, k_ref, v_ref, gate_ref, g_ref, beta_ref, nw_ref, s0_ref, y_out, s_out,
                     scr):
    probs = _scan_probs(q_ref)
    get = lambda ref: [_tile(ref, p) for p in probs]
    tiles = _gdn_tile(get(q_ref), get(k_ref), get(v_ref), [g_ref[h, j:j + 1, :] for h, j in probs],
                      [beta_ref[h, j:j + 1, :] for h, j in probs], C)

    def store(idx, S):
        s_out[idx] = S

    O = _scan_states(chained, C, probs, tiles, scr, lambda idx: s0_ref[idx], store,
                     lambda i, ts, Ss: _gdn_block(i, C, ts, Ss))
    for n, (h, j) in enumerate(probs):
        y_out[LANES * j:LANES * (j + 1), LANES * h:LANES * (h + 1)] = (
            _rmsnorm(O[n], nw_ref[...]) * _silu(_tile(gate_ref, (h, j)))).astype(y_out.dtype)


def _gla_scan_kernel(C, chained, q_ref, k_ref, lf_ref, v_ref, gate_ref, nw_ref, s0_ref, y_out, s_out, scr, gc_scr,
                     prod_scr):
    probs = _scan_probs(q_ref)
    tiles = _gla_tile(q_ref, k_ref, lf_ref, v_ref, probs, C, gc_scr, prod_scr)

    def store(idx, ST):
        s_out[idx] = ST.T

    O = _scan_states(chained, C, probs, tiles, scr, lambda idx: s0_ref[idx].T, store,
                     lambda i, ts, Ss: _gla_block(i, C, ts, Ss))
    for n, (h, j) in enumerate(probs):
        y_out[LANES * j:LANES * (j + 1), LANES * h:LANES * (h + 1)] = (
            _rmsnorm(O[n], nw_ref[...]) * _silu(_tile(gate_ref, (h, j)))).astype(y_out.dtype)


def _scan_call(body, name, C, chained, tiles, extra_ins, extra_specs, s0, layer, s_prev, nheads, extra_scratch=()):
    NB, RB, D = tiles[0].shape
    rows = min(SCAN_ROWS, RB)
    hg = SCAN_HEADS
    assert RB % rows == 0 and rows % LANES == 0 and nheads % hg == 0
    tile_spec = pl.BlockSpec((None, rows, hg * LANES), lambda b, h, t: (b, t, h))
    hs = hg * (s0.shape[2] // nheads)
    if chained:
        s_spec = pl.BlockSpec((None, None, hs) + s0.shape[3:], lambda b, h, t: (layer, b, h, 0, 0))
    else:
        nt = RB // rows
        s_spec = pl.BlockSpec((None, rows // C, hs) + s0.shape[3:], lambda b, h, t: (layer, b * nt + t, h, 0, 0))
    n_in = len(tiles) + len(extra_ins) + 1
    kern = functools.partial(body, C, chained)
    ins = [*tiles, *extra_ins, s0]
    in_specs = [tile_spec] * len(tiles) + list(extra_specs) + [s_spec]
    aliases = {}
    if s_prev is not None:
        ins.append(s_prev)
        in_specs.append(pl.BlockSpec(memory_space=pl.ANY))
        aliases = {n_in: 1}
        kern = lambda *refs: body(C, chained, *refs[:n_in], *refs[n_in + 1:])
    return pl.pallas_call(
        kern, name=name, grid=(NB, nheads // hg, RB // rows),
        in_specs=in_specs, out_specs=[tile_spec, s_spec],
        out_shape=[jax.ShapeDtypeStruct((NB, RB, D), BF16), jax.ShapeDtypeStruct(s0.shape, F32)],
        scratch_shapes=[pltpu.VMEM((hg, LANES, LANES), F32)] + list(extra_scratch),
        input_output_aliases=aliases,
        compiler_params=pltpu.CompilerParams(dimension_semantics=("parallel", "parallel", "arbitrary"),
                                             vmem_limit_bytes=VMEM_LIMIT),
    )(*ins)


class _Group:
    def __init__(self, B, T, time_major):
        self.B, self.T, self.tm_major = B, T, time_major
        if time_major:
            self.G, self.R, self.s = 1, B * T, B
            self.C = -(-T // SUBLANES) * SUBLANES
            assert LANES % self.C == 0 and (B * self.C) % LANES == 0
        else:
            self.G, self.R, self.s = B, T, 1
            self.C = SCAN_CHUNK
            assert T % LANES == 0
        self.chained = not time_major
        self.tile = min(ROW_TILE, self.R)
        self.ffn_tile = self.s if time_major else min(FFN_TILE, self.R)

    def to_rows(self, x):
        if self.tm_major:
            return jnp.swapaxes(x, 0, 1).reshape(1, self.R, x.shape[-1])
        return x

    def hist_to_rows(self, h):
        if self.tm_major:
            return jnp.swapaxes(h, 0, 1).reshape(1, -1, h.shape[-1])
        return h

    def hist_from_rows(self, h, n):
        if self.tm_major:
            return jnp.swapaxes(h.reshape(n, self.B, h.shape[-1]), 0, 1)
        return h

    def to_scan(self, a):
        if not self.tm_major:
            return a
        a = jnp.swapaxes(a.reshape(self.T, self.B, a.shape[-1]), 0, 1)
        a = jnp.pad(a, ((0, 0), (0, self.C - self.T), (0, 0)))
        return a.reshape(1, self.B * self.C, a.shape[-1])

    def from_scan(self, a):
        if not self.tm_major:
            return a
        return self.to_rows(a.reshape(self.B, self.C, a.shape[-1])[:, :self.T])


def _rwkv_layer(grp, x, shift0, S0, S_prev, v_first, P, j):
    D = x.shape[-1]
    s, tm = grp.s, grp.tile
    N = RWKV_N
    has_vres = v_first is not None
    r2, c2 = _iota2(LANES, LANES)
    e = ((r2 // N) == (c2 // N)).astype(BF16)
    row = lambda a: a.reshape(1, -1)
    tiled = [x] + ([v_first] if has_vres else [])
    consts = [row(P['norm_mix_pre_i']), P['rwkv_mix'][j], (P['rwkv_w_rkv_bf'], j),
              row(P['rwkv_w0'][j]), P['rwkv_w1'][j].astype(BF16), P['rwkv_w2'][j].astype(BF16),
              row(P['rwkv_a0'][j]), P['rwkv_a1'][j].astype(BF16), P['rwkv_a2'][j].astype(BF16),
              P['rwkv_g1'][j].astype(BF16), P['rwkv_g2'][j].astype(BF16),
              row(P['rwkv_k_k'][j]), row(P['rwkv_k_a'][j]), e]
    if has_vres:
        consts += [row(P['rwkv_v0'][j - 1]), P['rwkv_v1'][j - 1].astype(BF16), P['rwkv_v2'][j - 1].astype(BF16)]
    r, lw, k, v, kk, bb, g, shift = _row_call(
        functools.partial(_rwkv_pre_kernel, has_vres, s), f"rwkv_pre_{j}", tm, tiled, [shift0], consts,
        [(D, BF16), D, (D, BF16), D, (D, BF16), (D, BF16), (D, BF16)], [(s, D)],
        [pltpu.VMEM((_hist_pad(s) + tm, D), F32)])
    if not has_vres:
        v_first = v
    tiles = [grp.to_scan(a) for a in (r, lw, k, v, kk, bb, g)]
    nprob = SCAN_HEADS * (min(SCAN_ROWS, tiles[0].shape[1]) // LANES)
    vec_spec = pl.BlockSpec((1, SCAN_HEADS * LANES), lambda b, h, t: (0, h))
    e_spec = pl.BlockSpec((LANES, LANES), lambda b, h, t: (0, 0))
    y, S = _scan_call(_rwkv_scan_kernel, f"rwkv_scan_{j}", grp.C, grp.chained, tiles,
                      [row(P['rwkv_r_k'][j]), row(P['rwkv_lnx_w'][j]), row(P['rwkv_lnx_b'][j]), e],
                      [vec_spec] * 3 + [e_spec], S0, j, S_prev, D // LANES,
                      extra_scratch=[pltpu.VMEM((nprob, LANES, LANES), F32)])
    return grp.from_scan(y), (P['rwkv_w_o_bf'], j), shift, S, v_first


def _gdn_layer(grp, x, conv0, S0, S_prev, P, j):
    D = x.shape[-1]
    s, tm = grp.s, grp.tile
    w_in = P['gdn_w_in'][j]
    cw = P['gdn_conv_w'][j]
    nh, cdim = cw.shape[0] - 1, cw.shape[1]
    H = S0.shape[2]
    assert grp.T >= nh
    row = lambda a: a.reshape(1, -1)
    lane_pad = lambda a: jnp.pad(a, ((0, 0), (0, LANES - a.shape[1])))
    zeros = jnp.zeros((1, H), F32)
    consts = [row(P['norm_mix_pre_i']), w_in[:, :cdim].astype(BF16), w_in[:, cdim:cdim + D].astype(BF16),
              lane_pad(w_in[:, cdim + D:]).astype(BF16), cw,
              lane_pad(jnp.concatenate([zeros, row(P['gdn_a_log'][j])], axis=1)),
              lane_pad(jnp.concatenate([zeros, row(P['gdn_dt_bias'][j])], axis=1))]
    q, k, v, gate, bg, conv_new = _row_call(
        functools.partial(_gdn_pre_kernel, s, H), f"gdn_pre_{j}", tm, [x], [conv0], consts,
        [(D, BF16)] * 4 + [LANES], [(nh * s, cdim)], [pltpu.VMEM((_hist_pad(nh * s) + tm, cdim), F32)])
    tiles = [grp.to_scan(a) for a in (q, k, v, gate)]
    NB, RB, _ = tiles[0].shape
    rows = min(SCAN_ROWS, RB)
    nt = RB // rows
    bg = grp.to_scan(bg[:, :, :2 * H]).reshape(NB * RB, 2 * H).T.reshape(2 * H, NB * nt, rows // LANES, LANES)
    hg = SCAN_HEADS
    g_spec = pl.BlockSpec((hg, None, rows // LANES, LANES), lambda b, h, t: (H // hg + h, b * nt + t, 0, 0))
    beta_spec = pl.BlockSpec((hg, None, rows // LANES, LANES), lambda b, h, t: (h, b * nt + t, 0, 0))
    vec_spec = pl.BlockSpec((1, LANES), lambda b, h, t: (0, 0))
    y, S = _scan_call(_gdn_scan_kernel, f"gdn_scan_{j}", grp.C, grp.chained, tiles,
                      [bg, bg, row(P['gdn_norm_w'][j])], [g_spec, beta_spec, vec_spec], S0, j, S_prev, H)
    return grp.from_scan(y), (P['gdn_w_o_bf'], j), conv_new, S


def _hgrn_layer(grp, x, S0, S_prev, P, i, j):
    D = x.shape[-1]
    tm = grp.tile
    H = S0.shape[2]
    row = lambda a: a.reshape(1, -1)
    consts = [row(P['norm_mix_pre_i']), (P['hgrn_w_in_bf'], j), P['hgrn_lb']]
    q, k, lf, v, gate = _row_call(functools.partial(_hgrn_pre_kernel, i), f"hgrn_pre_{j}", tm, [x], [], consts,
                                  [D, D, D, (D, BF16), (D, BF16)], [], [])
    tiles = [grp.to_scan(a) for a in (q, k, lf, v, gate)]
    nprob = SCAN_HEADS * (min(SCAN_ROWS, tiles[0].shape[1]) // LANES)
    vec_spec = pl.BlockSpec((1, LANES), lambda b, h, t: (0, 0))
    y, S = _scan_call(_gla_scan_kernel, f"hgrn_scan_{j}", grp.C, grp.chained, tiles, [row(P['hgrn_norm_w'][j])],
                      [vec_spec], S0, j, S_prev, H,
                      extra_scratch=[pltpu.VMEM((nprob, LANES, LANES), F32),
                                     pltpu.VMEM((nprob, LANES, GLA_SUB * LANES), F32)])
    return grp.from_scan(y), (P['hgrn_w_o_bf'], j), S


def _trunk(grp, x, shift0, wkv0, gconv0, gS0, hS0, fconv0, P):
    D = x.shape[-1]
    depth = P['norm_mix_pre'].shape[0]
    row = lambda a: a.reshape(1, -1)
    x = grp.to_rows(x)
    v_first = None
    shift, gconv, fconv = [], [], []
    wkv = gS = hS = None
    for i in range(depth):
        kind, j = i % 3, i // 3
        P = dict(P, norm_mix_pre_i=P['norm_mix_pre'][i])
        if kind == 0:
            y, w_o, s_shift, wkv, v_first = _rwkv_layer(grp, x, grp.hist_to_rows(shift0[j][:, None]), wkv0, wkv,
                                                       v_first, P, j)
            shift.append(grp.hist_from_rows(s_shift, 1)[:, 0])
        elif kind == 1:
            y, w_o, c_new, gS = _gdn_layer(grp, x, grp.hist_to_rows(gconv0[j]), gS0, gS, P, j)
            gconv.append(grp.hist_from_rows(c_new, gconv0.shape[2]))
        else:
            y, w_o, hS = _hgrn_layer(grp, x, hS0, hS, P, i, j)
        nh = P['ffn_conv_w'].shape[1] - 1
        dff2 = P['ffn_w_up'].shape[2]
        x, c_new = _row_call(
            functools.partial(_ffn_kernel, grp.s), f"ffn_{i}", grp.ffn_tile, [y, x], [grp.hist_to_rows(fconv0[i])],
            [w_o, row(P['norm_mix_post'][i]),
             row(P['norm_ffn_pre'][i]), (P['ffn_w_up_bf'], i), P['ffn_conv_w'][i],
             (P['ffn_w_down_bf'], i), row(P['norm_ffn_post'][i])],
            [D], [(nh * grp.s, dff2)], [pltpu.VMEM((_hist_pad(nh * grp.s) + grp.ffn_tile, dff2), F32)])
        fconv.append(grp.hist_from_rows(c_new, nh))
    y = x.reshape(grp.T, grp.B, D).swapaxes(0, 1) if grp.tm_major else x
    return y, (jnp.stack(shift), wkv, jnp.stack(gconv), gS, hS, jnp.stack(fconv))


def kernel(x_prompt, x_sample, state_rwkv_shift, state_rwkv_wkv, state_gdn_conv, state_gdn_S, state_hgrn_S, state_ffn_conv, norm_mix_pre, norm_mix_post, norm_ffn_pre, norm_ffn_post, rwkv_mix, rwkv_w_rkv, rwkv_w0, rwkv_w1, rwkv_w2, rwkv_a0, rwkv_a1, rwkv_a2, rwkv_v0, rwkv_v1, rwkv_v2, rwkv_g1, rwkv_g2, rwkv_k_k, rwkv_k_a, rwkv_r_k, rwkv_lnx_w, rwkv_lnx_b, rwkv_w_o, gdn_w_in, gdn_conv_w, gdn_a_log, gdn_dt_bias, gdn_norm_w, gdn_w_o, hgrn_w_in, hgrn_lb, hgrn_norm_w, hgrn_w_o, ffn_w_up, ffn_conv_w, ffn_w_down):
    P = dict(norm_mix_pre=norm_mix_pre, norm_mix_post=norm_mix_post, norm_ffn_pre=norm_ffn_pre,
             norm_ffn_post=norm_ffn_post, rwkv_mix=rwkv_mix, rwkv_w_rkv=rwkv_w_rkv, rwkv_w0=rwkv_w0,
             rwkv_w1=rwkv_w1, rwkv_w2=rwkv_w2, rwkv_a0=rwkv_a0, rwkv_a1=rwkv_a1, rwkv_a2=rwkv_a2,
             rwkv_v0=rwkv_v0, rwkv_v1=rwkv_v1, rwkv_v2=rwkv_v2, rwkv_g1=rwkv_g1, rwkv_g2=rwkv_g2,
             rwkv_k_k=rwkv_k_k, rwkv_k_a=rwkv_k_a, rwkv_r_k=rwkv_r_k, rwkv_lnx_w=rwkv_lnx_w,
             rwkv_lnx_b=rwkv_lnx_b, rwkv_w_o=rwkv_w_o, gdn_w_in=gdn_w_in, gdn_conv_w=gdn_conv_w,
             gdn_a_log=gdn_a_log, gdn_dt_bias=gdn_dt_bias, gdn_norm_w=gdn_norm_w, gdn_w_o=gdn_w_o,
             hgrn_w_in=hgrn_w_in, hgrn_lb=hgrn_lb, hgrn_norm_w=hgrn_norm_w, hgrn_w_o=hgrn_w_o,
             ffn_w_up=ffn_w_up, ffn_conv_w=ffn_conv_w, ffn_w_down=ffn_w_down)
    for name in ('rwkv_w_rkv', 'rwkv_w_o', 'gdn_w_o', 'hgrn_w_in', 'hgrn_w_o', 'ffn_w_up', 'ffn_w_down'):
        P[name + '_bf'] = P[name].astype(BF16)
    Bp, Tp, _ = x_prompt.shape
    Bs, Ts, _ = x_sample.shape
    zero_like = lambda st: jnp.zeros((st.shape[0], Bp) + st.shape[2:], st.dtype)
    y_p, (p_shift, p_wkv, p_gconv, p_gS, p_hS, p_fconv) = _trunk(
        _Group(Bp, Tp, False), x_prompt, zero_like(state_rwkv_shift), zero_like(state_rwkv_wkv),
        zero_like(state_gdn_conv), zero_like(state_gdn_S), zero_like(state_hgrn_S), zero_like(state_ffn_conv), P)
    y_s, (s_shift, s_wkv, s_gconv, s_gS, s_hS, s_fconv) = _trunk(
        _Group(Bs, Ts, True), x_sample, state_rwkv_shift, state_rwkv_wkv, state_gdn_conv, state_gdn_S,
        state_hgrn_S, state_ffn_conv, P)
    return (y_p, y_s, p_shift, s_shift, p_wkv, s_wkv, p_gconv, s_gconv,
            p_gS, s_gS, p_hS, s_hS, p_fconv, s_fconv)
```

```python
import functools
import math

import jax
import jax.numpy as jnp
from jax import lax
from jax.experimental import pallas as pl
from jax.experimental.pallas import tpu as pltpu

F32 = jnp.float32
BF16 = jnp.bfloat16

NORM_EPS = 1e-6
L2_EPS = 1e-6
RWKV_LNX_EPS = 64e-5
RWKV_N = 64
LANES = 128
SUBLANES = 8
VMEM_LIMIT = 56 * 1024 * 1024
ROW_TILE = 256
WIDE_TILE = 512
FFN_TILE = 256
SCAN_ROWS = 256
SCAN_HEADS = 4
SCAN_CHUNK = 64
GLA_SUB = 8


def _dg(a, b, ca, cb):
    return lax.dot_general(a, b, (((ca,), (cb,)), ((), ())), preferred_element_type=F32)


def _mm(a, b):
    return _dg(a.astype(BF16), b.astype(BF16), 1, 0)


def _mm_nt(a, b):
    return _dg(a.astype(BF16), b.astype(BF16), 1, 1)


def _mm_tn(a, b):
    return _dg(a.astype(BF16), b.astype(BF16), 0, 0)


def _split3(x):
    hi = x.astype(BF16)
    r1 = x - hi.astype(F32)
    mid = r1.astype(BF16)
    lo = (r1 - mid.astype(F32)).astype(BF16)
    return hi, mid, lo


def _mm_xl(m, x):
    h, mi, lo = _split3(x)
    m = m.astype(BF16)
    return _dg(m, h, 1, 0) + (_dg(m, mi, 1, 0) + _dg(m, lo, 1, 0))


def _iota2(n, m):
    return (lax.broadcasted_iota(jnp.int32, (n, m), 0), lax.broadcasted_iota(jnp.int32, (n, m), 1))


def _sigmoid(x):
    return 1.0 / (1.0 + jnp.exp(-x))


def _silu(x):
    return x * _sigmoid(x)


def _softplus(x):
    return jnp.maximum(x, 0.0) + jnp.log(1.0 + jnp.exp(-jnp.abs(x)))


def _rmsnorm(x, w):
    return x * lax.rsqrt(jnp.mean(x * x, axis=-1, keepdims=True) + NORM_EPS) * w


def _head_sum(x, e):
    return _mm(x, e)


def _unit_lower_inv(Ls, C):
    n = Ls[0].shape[0]
    ri, ci = _iota2(n, n)
    eye = (ri == ci).astype(F32)
    Xs = [eye - L for L in Ls]
    Ps = list(Ls)
    m = 2
    while m < C:
        Ps = [_mm(P, P) for P in Ps]
        Xs = [X + _mm(X, P) for X, P in zip(Xs, Ps)]
        m *= 2
    return Xs


def _tile_masks(C):
    sh = C.bit_length() - 1
    ri, ci = _iota2(LANES, LANES)
    same = jnp.right_shift(ri, sh) == jnp.right_shift(ci, sh)
    return ri, ci, same, same & (ri >= ci), same & (ri > ci)


def _place_rows(x, r0, n):
    parts = []
    if r0:
        parts.append(jnp.zeros((r0, x.shape[1]), x.dtype))
    parts.append(x)
    if n - r0 - x.shape[0]:
        parts.append(jnp.zeros((n - r0 - x.shape[0], x.shape[1]), x.dtype))
    return jnp.concatenate(parts, axis=0) if len(parts) > 1 else x


def _rwkv_tile(r, lw, k, v, kk, bb, C):
    ri, ci, same, tril, stril = _tile_masks(C)
    lane = lax.broadcasted_iota(jnp.int32, (1, LANES), 1)
    m = [(lane < RWKV_N).astype(F32), (lane >= RWKV_N).astype(F32)]
    trilf = tril.astype(F32)
    n = len(r)
    gc = [_mm_xl(trilf, x) for x in lw]
    at = [-kk[g] * jnp.exp(gc[g] - lw[g]) for g in range(n)]
    rt = [r[g] * jnp.exp(gc[g]) for g in range(n)]
    sc = []
    for g in range(n):
        e_neg = jnp.exp(-gc[g])
        lhs = jnp.concatenate([at[g] * m[0], at[g] * m[1], rt[g] * m[0], rt[g] * m[1]], axis=0)
        sc.append(_mm_nt(lhs, jnp.concatenate([bb[g] * e_neg, k[g] * e_neg], axis=0)))
    gh = [(g, hd) for g in range(n) for hd in range(2)]
    a_blk = [sc[g][LANES * hd:LANES * (hd + 1)] for g, hd in gh]
    r_blk = [sc[g][LANES * (2 + hd):LANES * (3 + hd)] for g, hd in gh]
    A_rb = [jnp.where(tril, x[:, :LANES], 0.0) for x in r_blk]
    A_rk = [jnp.where(tril, x[:, LANES:], 0.0) for x in r_blk]
    Tinv = _unit_lower_inv([jnp.where(stril, -x[:, :LANES], 0.0) for x in a_blk], C)
    akv = [_mm(jnp.where(stril, a_blk[p][:, LANES:], 0.0), v[g]) for p, (g, hd) in enumerate(gh)]
    y = [_mm(Tinv[p], jnp.concatenate([at[g] * m[hd], akv[p]], axis=1)) for p, (g, hd) in enumerate(gh)]
    o0 = [_mm(jnp.concatenate([A_rk[p], A_rb[p]], axis=1), jnp.concatenate([v[g], y[p][:, LANES:]], axis=0))
          for p, (g, hd) in enumerate(gh)]
    out = []
    for g in range(n):
        p0, p1 = 2 * g, 2 * g + 1
        out.append(dict(gc=gc[g], rt=rt[g], m=m, Wa=y[p0][:, :LANES] + y[p1][:, :LANES],
                        U0=y[p0][:, LANES:] * m[0] + y[p1][:, LANES:] * m[1],
                        O0=o0[p0] * m[0] + o0[p1] * m[1],
                        A_rb=jnp.concatenate([A_rb[p0], A_rb[p1]], axis=0), k=k[g], v=v[g], bb=bb[g]))
    return out


def _rwkv_block(i, C, ts, Ss):
    sl = slice(C * i, C * (i + 1))
    sl1 = slice(LANES + C * i, LANES + C * (i + 1))
    r2, c2 = _iota2(LANES, LANES)
    blk = ((r2 >= RWKV_N) == (c2 >= RWKV_N)).astype(F32)
    xs = [_mm_nt(jnp.concatenate([t['Wa'][sl], t['rt'][sl]], axis=0), S) for t, S in zip(ts, Ss)]
    S_new, ax = [], []
    for t, S, x in zip(ts, Ss, xs):
        X = x[:C]
        U = t['U0'][sl] + X
        gl = t['gc'][C * (i + 1) - 1:C * (i + 1), :]
        e_last = jnp.exp(gl - t['gc'][sl])
        S_new.append(S * jnp.exp(gl) + blk * _mm_tn(
            jnp.concatenate([U, t['v'][sl]], axis=0),
            jnp.concatenate([t['bb'][sl] * e_last, t['k'][sl] * e_last], axis=0)))
        ax.append(_mm(jnp.concatenate([t['A_rb'][sl], t['A_rb'][sl1]], axis=0), _place_rows(X, C * i, LANES)))
    o = [t['O0'][sl] + x[C:] + a[:C] * t['m'][0] + a[C:] * t['m'][1] for t, x, a in zip(ts, xs, ax)]
    return o, S_new


def _gdn_tile(q, k, v, g_row, beta_row, C):
    ri, ci, same, tril, stril = _tile_masks(C)
    eye = ri == ci
    lsum = lambda msk, x: jnp.sum(jnp.where(msk, x, 0.0), axis=1, keepdims=True)
    n = len(q)
    pre = []
    for g in range(n):
        g_b = jnp.broadcast_to(g_row[g], (LANES, LANES))
        beta_col = lsum(eye, jnp.broadcast_to(beta_row[g], (LANES, LANES)))
        g_col = lsum(eye, g_b)
        gc_col = lsum(tril, g_b)
        gl_col = lsum(same, g_b)
        gc_row = jnp.sum(jnp.where(same & (ri <= ci), jnp.broadcast_to(g_col, (LANES, LANES)), 0.0),
                         axis=0, keepdims=True)
        decay = jnp.where(tril, jnp.exp(jnp.minimum(gc_col - gc_row, 0.0)), 0.0)
        pre.append(dict(beta=beta_col, gc=gc_col, gl=gl_col, decay=decay, kb=k[g] * beta_col))
    sc = [_mm_nt(jnp.concatenate([pre[g]['kb'], q[g]], axis=0), k[g]) for g in range(n)]
    Tinv = _unit_lower_inv([jnp.where(stril, sc[g][:LANES] * pre[g]['decay'], 0.0) for g in range(n)], C)
    out = []
    for g in range(n):
        p = pre[g]
        e_gc = jnp.exp(p['gc'])
        uw = _mm(Tinv[g], jnp.concatenate([v[g] * p['beta'], p['kb'] * e_gc], axis=1))
        out.append(dict(u=uw[:, :LANES], wk=uw[:, LANES:], qk=sc[g][LANES:] * p['decay'], qg=q[g] * e_gc,
                        kd=k[g] * jnp.exp(p['gl'] - p['gc']), egl=jnp.exp(p['gl'])))
    return out


def _gdn_block(i, C, ts, Ss):
    sl = slice(C * i, C * (i + 1))
    ws = [_mm(jnp.concatenate([t['wk'][sl], t['qg'][sl]], axis=0), S) for t, S in zip(ts, Ss)]
    v_new = [t['u'][sl] - w[:C] for t, w in zip(ts, ws)]
    S_new = [S * t['egl'][C * i:C * i + 1, :] + _mm_tn(t['kd'][sl], vn) for t, S, vn in zip(ts, Ss, v_new)]
    o = [w[C:] + _mm(t['qk'][sl], _place_rows(vn, C * i, LANES)) for t, w, vn in zip(ts, ws, v_new)]
    return o, S_new


def _gla_tile(q_ref, k_ref, lf_ref, v_ref, probs, C, gc_scr, prod_scr):
    n = len(probs)
    nsb = LANES // GLA_SUB
    ri, ci, same, tril, _ = _tile_masks(C)
    sub = GLA_SUB.bit_length() - 1
    rsub, csub = jnp.right_shift(ri, sub), jnp.right_shift(ci, sub)
    trilf = tril.astype(F32)
    get = lambda ref: [_tile(ref, p) for p in probs]
    q, k, v = get(q_ref), get(k_ref), get(v_ref)
    gc = [_mm_xl(trilf, x) for x in get(lf_ref)]
    rk, ck = _iota2(GLA_SUB * LANES, LANES)
    sel = (jnp.right_shift(rk, LANES.bit_length() - 1) == ck).astype(BF16)
    for g, (h, j) in enumerate(probs):
        gc_scr[g] = gc[g]
        for I in range(nsb):
            s0 = GLA_SUB * I
            ksub = k[g][s0:s0 + GLA_SUB]
            gsub = gc[g][s0:s0 + GLA_SUB]
            for ii in range(GLA_SUB):
                i = s0 + ii
                gi = gc_scr[g, i:i + 1, :]
                qi = q_ref[LANES * j + i:LANES * j + i + 1, LANES * h:LANES * (h + 1)]
                prod_scr[g, s0:s0 + GLA_SUB, LANES * ii:LANES * (ii + 1)] = (
                    qi * ksub * jnp.exp(jnp.minimum(gi - gsub, 0.0)))
    d = [_mm(prod_scr[g], sel) for g in range(n)]
    AT = []
    for g in range(n):
        strips = [d[g][:GLA_SUB]] + [pltpu.roll(d[g][GLA_SUB * I:GLA_SUB * (I + 1)], GLA_SUB * I, axis=1)
                                     for I in range(1, nsb)]
        AT.append(jnp.where((rsub == csub) & (ri <= ci), jnp.concatenate(strips, axis=0), 0.0))
    nsub = C // GLA_SUB
    if nsub > 1:
        for g in range(n):
            off = []
            for b in range(LANES // C):
                b0 = C * b
                kblk = k[g][b0:b0 + C]
                gblk = gc[g][b0:b0 + C]
                acc = jnp.zeros((C, LANES), F32)
                for J in range(1, nsub):
                    s0 = b0 + GLA_SUB * J
                    ref = gc_scr[g, s0 - 1:s0, :]
                    khat = kblk * jnp.exp(jnp.minimum(ref - gblk, 0.0))
                    qhat = q[g][s0:s0 + GLA_SUB] * jnp.exp(gc[g][s0:s0 + GLA_SUB] - ref)
                    acc = acc + _mm_nt(khat, _place_rows(qhat, s0, LANES))
                off.append(acc)
            AT[g] = AT[g] + jnp.where(same & (rsub < csub), jnp.concatenate(off, axis=0), 0.0)
    o_intra = [_mm_tn(AT[g], v[g]) for g in range(n)]
    return [dict(gc=gc[g], k=k[g], v=v[g], qg=q[g] * jnp.exp(gc[g]), o_intra=o_intra[g]) for g in range(n)]


def _gla_block(i, C, ts, STs):
    sl = slice(C * i, C * (i + 1))
    o, ST_new = [], []
    for t, ST in zip(ts, STs):
        gl = t['gc'][C * (i + 1) - 1:C * (i + 1), :]
        o.append(t['o_intra'][sl] + _mm_nt(t['qg'][sl], ST))
        ST_new.append(ST * jnp.exp(gl) + _mm_tn(t['v'][sl], t['k'][sl] * jnp.exp(gl - t['gc'][sl])))
    return o, ST_new


def _shifted(ext, hp, s, tm, back):
    return ext[hp - back * s:hp - back * s + tm, :]


def _rwkv_pre_kernel(has_vres, s, *refs):
    if has_vres:
        (x_ref, vf_ref, sh0_ref, nw_ref, mix_ref, wrkv_ref, w0_ref, w1_ref, w2_ref, a0_ref, a1_ref, a2_ref,
         g1_ref, g2_ref, kkw_ref, ka_ref, e_ref, v0_ref, v1_ref, v2_ref,
         r_out, lw_out, k_out, v_out, kk_out, bb_out, g_out, sh_out, ext) = refs
    else:
        (x_ref, sh0_ref, nw_ref, mix_ref, wrkv_ref, w0_ref, w1_ref, w2_ref, a0_ref, a1_ref, a2_ref,
         g1_ref, g2_ref, kkw_ref, ka_ref, e_ref,
         r_out, lw_out, k_out, v_out, kk_out, bb_out, g_out, sh_out, ext) = refs
    t = pl.program_id(1)
    tm, D = x_ref.shape
    hp = ext.shape[0] - tm
    h = _rmsnorm(x_ref[...], nw_ref[...])

    @pl.when(t == 0)
    def _():
        ext[hp - s:hp, :] = sh0_ref[...]

    ext[hp:, :] = h
    d = _shifted(ext, hp, s, tm, 1) - h
    ext[hp - s:hp, :] = h[tm - s:, :]

    @pl.when(t == pl.num_programs(1) - 1)
    def _():
        sh_out[...] = h[tm - s:, :]

    mixed = lambda i: h + d * mix_ref[i:i + 1, :]
    r = _mm(mixed(0), wrkv_ref[0])
    z = w0_ref[...] + _mm(jnp.tanh(_mm(mixed(1), w1_ref[...])), w2_ref[...])
    lw_out[...] = -math.exp(-0.5) * _sigmoid(z)
    k = _mm(mixed(2), wrkv_ref[1])
    xv = mixed(3)
    v = _mm(xv, wrkv_ref[2])
    a = _sigmoid(a0_ref[...] + _mm(_mm(mixed(4), a1_ref[...]), a2_ref[...]))
    if has_vres:
        gate_v = _sigmoid(v0_ref[...] + _mm(_mm(xv, v1_ref[...]), v2_ref[...]))
        v = v + (vf_ref[...] - v) * gate_v
    g_out[...] = _mm(_sigmoid(_mm(mixed(5), g1_ref[...])), g2_ref[...]).astype(g_out.dtype)
    r_out[...] = r.astype(r_out.dtype)
    v_out[...] = v
    kkraw = k * kkw_ref[...]
    e = e_ref[...]
    for c in range(D // LANES):
        sl = slice(c * LANES, (c + 1) * LANES)
        kc = kkraw[:, sl]
        kkn = kc * lax.rsqrt(_head_sum(kc * kc, e) + L2_EPS)
        kk_out[:, sl] = kkn.astype(kk_out.dtype)
        bb_out[:, sl] = (kkn * a[:, sl]).astype(bb_out.dtype)
    k_out[...] = (k * (1.0 + (a - 1.0) * ka_ref[...])).astype(k_out.dtype)


def _gdn_pre_kernel(s, nheads, x_ref, c0_ref, nw_ref, wqkv_ref, wgate_ref, wbg_ref, cw_ref, alog_ref, dtb_ref,
                    q_out, k_out, v_out, gate_out, bg_out, cnew_out, ext):
    t = pl.program_id(1)
    tm, D = x_ref.shape
    hp = ext.shape[0] - tm
    nh = cw_ref.shape[0] - 1
    h = _rmsnorm(x_ref[...], nw_ref[...])

    @pl.when(t == 0)
    def _():
        ext[hp - nh * s:hp, :] = c0_ref[...]

    ext[hp:, :] = _mm(h, wqkv_ref[...])
    y = _shifted(ext, hp, s, tm, 0) * cw_ref[nh:nh + 1, :]
    for j in range(nh):
        y = y + _shifted(ext, hp, s, tm, nh - j) * cw_ref[j:j + 1, :]
    hist = ext[hp + tm - nh * s:hp + tm, :]

    @pl.when(t == pl.num_programs(1) - 1)
    def _():
        cnew_out[...] = hist

    ext[hp - nh * s:hp, :] = hist
    y = _silu(y)
    qk_w = D
    for c in range(qk_w // LANES):
        sl = slice(c * LANES, (c + 1) * LANES)
        qc = y[:, sl]
        q_out[:, sl] = (qc * (lax.rsqrt(jnp.sum(qc * qc, axis=-1, keepdims=True) + L2_EPS) * LANES ** -0.5)
                        ).astype(q_out.dtype)
        sl2 = slice(qk_w + c * LANES, qk_w + (c + 1) * LANES)
        kc = y[:, sl2]
        k_out[:, sl] = (kc * lax.rsqrt(jnp.sum(kc * kc, axis=-1, keepdims=True) + L2_EPS)).astype(k_out.dtype)
    v_out[...] = y[:, 2 * qk_w:].astype(v_out.dtype)
    gate_out[...] = _mm(h, wgate_ref[...]).astype(gate_out.dtype)
    rest = _mm(h, wbg_ref[...])
    lane = lax.broadcasted_iota(jnp.int32, rest.shape, 1)
    bg_out[...] = jnp.where(lane < nheads, _sigmoid(rest),
                            -jnp.exp(alog_ref[...]) * _softplus(rest + dtb_ref[...]))


def _hgrn_pre_kernel(layer, x_ref, nw_ref, win_ref, lb_ref, q_out, k_out, lf_out, v_out, gate_out):
    D = x_ref.shape[1]
    h = _rmsnorm(x_ref[...], nw_ref[...])
    lbp = lb_ref[...]
    ex = jnp.exp(lbp - jnp.max(lbp, axis=0, keepdims=True))
    soft = ex / jnp.sum(ex, axis=0, keepdims=True)
    row = lax.broadcasted_iota(jnp.int32, soft.shape, 0)
    lb = jnp.sum(jnp.where((row >= 1) & (row <= layer), soft, 0.0), axis=0, keepdims=True)
    p = _mm(h, win_ref[...])
    q_out[...] = _silu(p[:, :D])
    f = lb + (1.0 - lb) * _sigmoid(p[:, D:2 * D])
    k_out[...] = 1.0 - f
    lf_out[...] = jnp.log(f)
    v_out[...] = p[:, 2 * D:3 * D].astype(v_out.dtype)
    gate_out[...] = p[:, 3 * D:].astype(gate_out.dtype)


def _ffn_kernel(s, y_ref, x_ref, c0_ref, wo_ref, mpw_ref, nw_ref, wup_ref, cw_ref, wdn_ref, pw_ref, x_out, cnew_out,
                ext):
    t = pl.program_id(1)
    tm, D = x_ref.shape
    hp = ext.shape[0] - tm
    nh = cw_ref.shape[0] - 1
    dff = wdn_ref.shape[0]
    x = x_ref[...] + _rmsnorm(_mm(y_ref[...], wo_ref[...]), mpw_ref[...])
    h = _rmsnorm(x, nw_ref[...])

    @pl.when(t == 0)
    def _():
        ext[hp - nh * s:hp, :] = c0_ref[...]

    ext[hp:, :] = _mm(h, wup_ref[...])
    y = _shifted(ext, hp, s, tm, 0) * cw_ref[nh:nh + 1, :]
    for j in range(nh):
        y = y + _shifted(ext, hp, s, tm, nh - j) * cw_ref[j:j + 1, :]
    hist = ext[hp + tm - nh * s:hp + tm, :]

    @pl.when(t == pl.num_programs(1) - 1)
    def _():
        cnew_out[...] = hist

    ext[hp - nh * s:hp, :] = hist
    act = _silu(y[:, dff:]) * y[:, :dff]
    x_out[...] = x + _rmsnorm(_mm(act, wdn_ref[...]), pw_ref[...])


def _const_spec(a):
    if isinstance(a, tuple):
        a, i = a
        nd = a.ndim - 1
        return pl.BlockSpec((None,) + a.shape[1:], lambda g, t: (i,) + (0,) * nd, pipeline_mode=pl.Buffered(1))
    nd = a.ndim
    return pl.BlockSpec(a.shape, lambda g, t: (0,) * nd, pipeline_mode=pl.Buffered(1))


def _row_call(body, name, tm, tiled_ins, group_ins, const_ins, tiled_out_widths, group_out_shapes, scratch):
    G, R, _ = tiled_ins[0].shape
    assert R % tm == 0
    tile_spec = lambda c: pl.BlockSpec((None, tm, c), lambda g, t: (g, t, 0))
    group_spec = lambda n, c: pl.BlockSpec((None, n, c), lambda g, t: (g, 0, 0))
    group_in_spec = lambda n, c: pl.BlockSpec((None, n, c), lambda g, t: (g, 0, 0), pipeline_mode=pl.Buffered(1))
    in_specs = ([tile_spec(a.shape[2]) for a in tiled_ins] + [group_in_spec(*a.shape[1:]) for a in group_ins]
                + [_const_spec(a) for a in const_ins])
    tiled_outs = [c if isinstance(c, tuple) else (c, F32) for c in tiled_out_widths]
    out_specs = [tile_spec(c) for c, _ in tiled_outs] + [group_spec(n, c) for n, c in group_out_shapes]
    out_shape = ([jax.ShapeDtypeStruct((G, R, c), dt) for c, dt in tiled_outs]
                 + [jax.ShapeDtypeStruct((G, n, c), F32) for n, c in group_out_shapes])
    return pl.pallas_call(
        body, name=name, grid=(G, R // tm), in_specs=in_specs, out_specs=out_specs, out_shape=out_shape,
        scratch_shapes=scratch,
        compiler_params=pltpu.CompilerParams(dimension_semantics=("parallel", "arbitrary"),
                                             vmem_limit_bytes=VMEM_LIMIT),
    )(*tiled_ins, *group_ins, *[a[0] if isinstance(a, tuple) else a for a in const_ins])


def _hist_pad(n):
    return -(-n // SUBLANES) * SUBLANES


def _tile(ref, p):
    h, j = p
    return ref[LANES * j:LANES * (j + 1), LANES * h:LANES * (h + 1)].astype(F32)


def _scan_states(chained, C, probs, tiles, scr, load, store, block_fn):
    nb = LANES // C
    tt = pl.program_id(2)
    heads = sorted({h for h, _ in probs})
    nt = len(probs) // len(heads)
    outs = {}
    if chained:
        @pl.when(tt == 0)
        def _():
            for h in heads:
                scr[h] = load((h,))

        Ss = [scr[h] for h in heads]
        for j in range(nt):
            sel = [tiles[probs.index((h, j))] for h in heads]
            for i in range(nb):
                os, Ss = block_fn(i, sel, Ss)
                for h, o in zip(heads, os):
                    outs[(h, j, i)] = o
        for h, S in zip(heads, Ss):
            scr[h] = S

        @pl.when(tt == pl.num_programs(2) - 1)
        def _():
            for h, S in zip(heads, Ss):
                store((h,), S)
    else:
        for i in range(nb):
            Ss = [load((j * nb + i, h)) for h, j in probs]
            os, Ss = block_fn(i, tiles, Ss)
            for (h, j), o, S in zip(probs, os, Ss):
                outs[(h, j, i)] = o
                store((j * nb + i, h), S)
    return [jnp.concatenate([outs[(h, j, i)] for i in range(nb)], axis=0) if nb > 1 else outs[(h, j, 0)]
            for h, j in probs]


def _scan_probs(ref):
    return [(h, j) for h in range(ref.shape[1] // LANES) for j in range(ref.shape[0] // LANES)]


def _rwkv_scan_kernel(C, chained, r_ref, lw_ref, k_ref, v_ref, kk_ref, bb_ref, g_ref, rk_ref, lnw_ref, lnb_ref,
                      e_ref, s0_ref, y_out, s_out, scr, bd):
    N = RWKV_N
    e = e_ref[...]
    probs = _scan_probs(r_ref)
    nt = r_ref.shape[0] // LANES
    nb = LANES // C
    get = lambda ref: [_tile(ref, p) for p in probs]
    r, k, v = get(r_ref), get(k_ref), get(v_ref)
    tiles = _rwkv_tile(r, get(lw_ref), k, v, get(kk_ref), get(bb_ref), C)

    bd[...] = jnp.zeros(bd.shape, F32)
    slot = lambda idx: idx[-1] * nt + (idx[0] // nb if len(idx) > 1 else 0)
    heads = lambda idx: (idx[:-1] + (2 * idx[-1],), idx[:-1] + (2 * idx[-1] + 1,))

    def load(idx):
        p = slot(idx)
        h0, h1 = heads(idx)
        bd[p, :N, :N] = s0_ref[h0]
        bd[p, N:, N:] = s0_ref[h1]
        return bd[p]

    def store(idx, S):
        p = slot(idx)
        h0, h1 = heads(idx)
        bd[p] = S
        s_out[h0] = bd[p, :N, :N]
        s_out[h1] = bd[p, N:, N:]

    O = _scan_states(chained, C, probs, tiles, scr, load, store, lambda i, ts, Ss: _rwkv_block(i, C, ts, Ss))
    np_ = len(probs)
    lanes = [slice(LANES * h, LANES * (h + 1)) for h, _ in probs]
    s1 = _head_sum(jnp.concatenate(O + [r[n] * k[n] * rk_ref[:, lanes[n]] for n in range(np_)], axis=0), e)
    d = [O[n] - s1[LANES * n:LANES * (n + 1)] * (1.0 / N) for n in range(np_)]
    s2 = _head_sum(jnp.concatenate([x * x for x in d], axis=0), e)
    for n, (h, j) in enumerate(probs):
        var = s2[LANES * n:LANES * (n + 1)] * (1.0 / N)
        on = d[n] * lax.rsqrt(var + RWKV_LNX_EPS) * lnw_ref[:, lanes[n]] + lnb_ref[:, lanes[n]]
        bonus = s1[LANES * (np_ + n):LANES * (np_ + n + 1)] * v[n]
        y_out[LANES * j:LANES * (j + 1), lanes[n]] = ((on + bonus) * _tile(g_ref, (h, j))).astype(y_out.dtype)


def _gdn_scan_kernel(C, chained, q_ref, k_ref, v_ref, gate_ref, g_ref, beta_ref, nw_ref, s0_ref, y_out, s_out,
                     scr):
    probs = _scan_probs(q_ref)
    get = lambda ref: [_tile(ref, p) for p in probs]
    tiles = _gdn_tile(get(q_ref), get(k_ref), get(v_ref), [g_ref[h, j:j + 1, :] for h, j in probs],
                      [beta_ref[h, j:j + 1, :] for h, j in probs], C)

    def store(idx, S):
        s_out[idx] = S

    O = _scan_states(chained, C, probs, tiles, scr, lambda idx: s0_ref[idx], store,
                     lambda i, ts, Ss: _gdn_block(i, C, ts, Ss))
    for n, (h, j) in enumerate(probs):
        y_out[LANES * j:LANES * (j + 1), LANES * h:LANES * (h + 1)] = (
            _rmsnorm(O[n], nw_ref[...]) * _silu(_tile(gate_ref, (h, j)))).astype(y_out.dtype)


def _gla_scan_kernel(C, chained, q_ref, k_ref, lf_ref, v_ref, gate_ref, nw_ref, s0_ref, y_out, s_out, scr, gc_scr,
                     prod_scr):
    probs = _scan_probs(q_ref)
    tiles = _gla_tile(q_ref, k_ref, lf_ref, v_ref, probs, C, gc_scr, prod_scr)

    def store(idx, ST):
        s_out[idx] = ST.T

    O = _scan_states(chained, C, probs, tiles, scr, lambda idx: s0_ref[idx].T, store,
                     lambda i, ts, Ss: _gla_block(i, C, ts, Ss))
    for n, (h, j) in enumerate(probs):
        y_out[LANES * j:LANES * (j + 1), LANES * h:LANES * (h + 1)] = (
            _rmsnorm(O[n], nw_ref[...]) * _silu(_tile(gate_ref, (h, j)))).astype(y_out.dtype)


def _scan_call(body, name, C, chained, tiles, extra_ins, extra_specs, s0, layer, nlayers, s_prev, nheads,
               extra_scratch=()):
    NB, RB, D = tiles[0].shape
    rows = min(SCAN_ROWS, RB)
    hg = SCAN_HEADS
    assert RB % rows == 0 and rows % LANES == 0 and nheads % hg == 0
    tile_spec = pl.BlockSpec((None, rows, hg * LANES), lambda b, h, t: (b, t, h))
    hs = hg * (s0.shape[2] // nheads)
    nt = RB // rows
    if chained:
        s_spec = lambda l: pl.BlockSpec((None, None, hs) + s0.shape[3:], lambda b, h, t: (l, b, h, 0, 0))
    else:
        s_spec = lambda l: pl.BlockSpec((None, rows // C, hs) + s0.shape[3:],
                                        lambda b, h, t: (l, b * nt + t, h, 0, 0))
    s0_layer = layer if s0.shape[0] == nlayers else 0
    n_in = len(tiles) + len(extra_ins) + 1
    kern = functools.partial(body, C, chained)
    ins = [*tiles, *extra_ins, s0]
    in_specs = [tile_spec] * len(tiles) + list(extra_specs) + [s_spec(s0_layer)]
    aliases = {}
    if s_prev is not None:
        ins.append(s_prev)
        in_specs.append(pl.BlockSpec(memory_space=pl.ANY))
        aliases = {n_in: 1}
        kern = lambda *refs: body(C, chained, *refs[:n_in], *refs[n_in + 1:])
    return pl.pallas_call(
        kern, name=name, grid=(NB, nheads // hg, RB // rows),
        in_specs=in_specs, out_specs=[tile_spec, s_spec(layer)],
        out_shape=[jax.ShapeDtypeStruct((NB, RB, D), BF16),
                   jax.ShapeDtypeStruct((nlayers,) + s0.shape[1:], F32)],
        scratch_shapes=[pltpu.VMEM((hg, LANES, LANES), F32)] + list(extra_scratch),
        input_output_aliases=aliases,
        compiler_params=pltpu.CompilerParams(dimension_semantics=("parallel", "parallel", "arbitrary"),
                                             vmem_limit_bytes=VMEM_LIMIT),
    )(*ins)


class _Group:
    def __init__(self, B, T, time_major):
        self.B, self.T, self.tm_major = B, T, time_major
        if time_major:
            self.G, self.R, self.s = 1, B * T, B
            self.C = -(-T // SUBLANES) * SUBLANES
            assert LANES % self.C == 0 and (B * self.C) % LANES == 0
        else:
            self.G, self.R, self.s = B, T, 1
            self.C = SCAN_CHUNK
            assert T % LANES == 0
        self.chained = not time_major
        self.tile = min(ROW_TILE, self.R)
        self.wide_tile = min(WIDE_TILE, self.R)
        self.ffn_tile = self.s if time_major else min(FFN_TILE, self.R)

    def to_rows(self, x):
        if self.tm_major:
            return jnp.swapaxes(x, 0, 1).reshape(1, self.R, x.shape[-1])
        return x

    def hist_to_rows(self, h):
        if self.tm_major:
            return jnp.swapaxes(h, 0, 1).reshape(1, -1, h.shape[-1])
        return h

    def hist_from_rows(self, h, n):
        if self.tm_major:
            return jnp.swapaxes(h.reshape(n, self.B, h.shape[-1]), 0, 1)
        return h

    def to_scan(self, a):
        if not self.tm_major:
            return a
        a = jnp.swapaxes(a.reshape(self.T, self.B, a.shape[-1]), 0, 1)
        a = jnp.pad(a, ((0, 0), (0, self.C - self.T), (0, 0)))
        return a.reshape(1, self.B * self.C, a.shape[-1])

    def from_scan(self, a):
        if not self.tm_major:
            return a
        return self.to_rows(a.reshape(self.B, self.C, a.shape[-1])[:, :self.T])


def _rwkv_layer(grp, x, shift0, S0, S_prev, v_first, P, j):
    D = x.shape[-1]
    s, tm = grp.s, grp.tile
    N = RWKV_N
    has_vres = v_first is not None
    r2, c2 = _iota2(LANES, LANES)
    e = ((r2 // N) == (c2 // N)).astype(BF16)
    row = lambda a: a.reshape(1, -1)
    tiled = [x] + ([v_first] if has_vres else [])
    consts = [row(P['norm_mix_pre_i']), P['rwkv_mix'][j], (P['rwkv_w_rkv_bf'], j),
              row(P['rwkv_w0'][j]), P['rwkv_w1'][j].astype(BF16), P['rwkv_w2'][j].astype(BF16),
              row(P['rwkv_a0'][j]), P['rwkv_a1'][j].astype(BF16), P['rwkv_a2'][j].astype(BF16),
              P['rwkv_g1'][j].astype(BF16), P['rwkv_g2'][j].astype(BF16),
              row(P['rwkv_k_k'][j]), row(P['rwkv_k_a'][j]), e]
    if has_vres:
        consts += [row(P['rwkv_v0'][j - 1]), P['rwkv_v1'][j - 1].astype(BF16), P['rwkv_v2'][j - 1].astype(BF16)]
    r, lw, k, v, kk, bb, g, shift = _row_call(
        functools.partial(_rwkv_pre_kernel, has_vres, s), f"rwkv_pre_{j}", tm, tiled, [shift0], consts,
        [(D, BF16), D, (D, BF16), D, (D, BF16), (D, BF16), (D, BF16)], [(s, D)],
        [pltpu.VMEM((_hist_pad(s) + tm, D), F32)])
    if not has_vres:
        v_first = v
    tiles = [grp.to_scan(a) for a in (r, lw, k, v, kk, bb, g)]
    nprob = SCAN_HEADS * (min(SCAN_ROWS, tiles[0].shape[1]) // LANES)
    vec_spec = pl.BlockSpec((1, SCAN_HEADS * LANES), lambda b, h, t: (0, h))
    e_spec = pl.BlockSpec((LANES, LANES), lambda b, h, t: (0, 0))
    y, S = _scan_call(_rwkv_scan_kernel, f"rwkv_scan_{j}", grp.C, grp.chained, tiles,
                      [row(P['rwkv_r_k'][j]), row(P['rwkv_lnx_w'][j]), row(P['rwkv_lnx_b'][j]), e],
                      [vec_spec] * 3 + [e_spec], S0[j][None], j, S0.shape[0], S_prev, D // LANES,
                      extra_scratch=[pltpu.VMEM((nprob, LANES, LANES), F32)])
    return grp.from_scan(y), (P['rwkv_w_o_bf'], j), shift, S, v_first


def _gdn_layer(grp, x, conv0, S0, S_prev, P, j):
    D = x.shape[-1]
    s, tm = grp.s, grp.wide_tile
    w_in = P['gdn_w_in'][j]
    cw = P['gdn_conv_w'][j]
    nh, cdim = cw.shape[0] - 1, cw.shape[1]
    H = S0.shape[2]
    assert grp.T >= nh
    row = lambda a: a.reshape(1, -1)
    lane_pad = lambda a: jnp.pad(a, ((0, 0), (0, LANES - a.shape[1])))
    zeros = jnp.zeros((1, H), F32)
    consts = [row(P['norm_mix_pre_i']), w_in[:, :cdim].astype(BF16), w_in[:, cdim:cdim + D].astype(BF16),
              lane_pad(w_in[:, cdim + D:]).astype(BF16), cw,
              lane_pad(jnp.concatenate([zeros, row(P['gdn_a_log'][j])], axis=1)),
              lane_pad(jnp.concatenate([zeros, row(P['gdn_dt_bias'][j])], axis=1))]
    q, k, v, gate, bg, conv_new = _row_call(
        functools.partial(_gdn_pre_kernel, s, H), f"gdn_pre_{j}", tm, [x], [conv0], consts,
        [(D, BF16)] * 4 + [LANES], [(nh * s, cdim)], [pltpu.VMEM((_hist_pad(nh * s) + tm, cdim), F32)])
    tiles = [grp.to_scan(a) for a in (q, k, v, gate)]
    NB, RB, _ = tiles[0].shape
    rows = min(SCAN_ROWS, RB)
    nt = RB // rows
    bg = grp.to_scan(bg[:, :, :2 * H]).reshape(NB * RB, 2 * H).T.reshape(2 * H, NB * nt, rows // LANES, LANES)
    hg = SCAN_HEADS
    g_spec = pl.BlockSpec((hg, None, rows // LANES, LANES), lambda b, h, t: (H // hg + h, b * nt + t, 0, 0))
    beta_spec = pl.BlockSpec((hg, None, rows // LANES, LANES), lambda b, h, t: (h, b * nt + t, 0, 0))
    vec_spec = pl.BlockSpec((1, LANES), lambda b, h, t: (0, 0))
    y, S = _scan_call(_gdn_scan_kernel, f"gdn_scan_{j}", grp.C, grp.chained, tiles,
                      [bg, bg, row(P['gdn_norm_w'][j])], [g_spec, beta_spec, vec_spec], S0, j, S0.shape[0], S_prev, H)
    return grp.from_scan(y), (P['gdn_w_o_bf'], j), conv_new, S


def _hgrn_layer(grp, x, S0, S_prev, P, i, j):
    D = x.shape[-1]
    tm = grp.wide_tile
    H = S0.shape[2]
    row = lambda a: a.reshape(1, -1)
    consts = [row(P['norm_mix_pre_i']), (P['hgrn_w_in_bf'], j), P['hgrn_lb']]
    q, k, lf, v, gate = _row_call(functools.partial(_hgrn_pre_kernel, i), f"hgrn_pre_{j}", tm, [x], [], consts,
                                  [D, D, D, (D, BF16), (D, BF16)], [], [])
    tiles = [grp.to_scan(a) for a in (q, k, lf, v, gate)]
    nprob = SCAN_HEADS * (min(SCAN_ROWS, tiles[0].shape[1]) // LANES)
    vec_spec = pl.BlockSpec((1, LANES), lambda b, h, t: (0, 0))
    y, S = _scan_call(_gla_scan_kernel, f"hgrn_scan_{j}", grp.C, grp.chained, tiles, [row(P['hgrn_norm_w'][j])],
                      [vec_spec], S0, j, S0.shape[0], S_prev, H,
                      extra_scratch=[pltpu.VMEM((nprob, LANES, LANES), F32),
                                     pltpu.VMEM((nprob, LANES, GLA_SUB * LANES), F32)])
    return grp.from_scan(y), (P['hgrn_w_o_bf'], j), S


def _trunk(grp, x, shift0, wkv0, gconv0, gS0, hS0, fconv0, P):
    D = x.shape[-1]
    depth = P['norm_mix_pre'].shape[0]
    row = lambda a: a.reshape(1, -1)
    x = grp.to_rows(x)
    v_first = None
    shift, gconv, fconv = [], [], []
    wkv = gS = hS = None
    for i in range(depth):
        kind, j = i % 3, i // 3
        P = dict(P, norm_mix_pre_i=P['norm_mix_pre'][i])
        if kind == 0:
            y, w_o, s_shift, wkv, v_first = _rwkv_layer(grp, x, grp.hist_to_rows(shift0[j][:, None]), wkv0, wkv,
                                                       v_first, P, j)
            shift.append(grp.hist_from_rows(s_shift, 1)[:, 0])
        elif kind == 1:
            y, w_o, c_new, gS = _gdn_layer(grp, x, grp.hist_to_rows(gconv0[j]), gS0, gS, P, j)
            gconv.append(grp.hist_from_rows(c_new, gconv0.shape[2]))
        else:
            y, w_o, hS = _hgrn_layer(grp, x, hS0, hS, P, i, j)
        nh = P['ffn_conv_w'].shape[1] - 1
        dff2 = P['ffn_w_up'].shape[2]
        x, c_new = _row_call(
            functools.partial(_ffn_kernel, grp.s), f"ffn_{i}", grp.ffn_tile, [y, x], [grp.hist_to_rows(fconv0[i])],
            [w_o, row(P['norm_mix_post'][i]),
             row(P['norm_ffn_pre'][i]), (P['ffn_w_up_bf'], i), P['ffn_conv_w'][i],
             (P['ffn_w_down_bf'], i), row(P['norm_ffn_post'][i])],
            [D], [(nh * grp.s, dff2)], [pltpu.VMEM((_hist_pad(nh * grp.s) + grp.ffn_tile, dff2), F32)])
        fconv.append(grp.hist_from_rows(c_new, nh))
    y = x.reshape(grp.T, grp.B, D).swapaxes(0, 1) if grp.tm_major else x
    return y, (jnp.stack(shift), wkv, jnp.stack(gconv), gS, hS, jnp.stack(fconv))


def kernel(x_prompt, x_sample, state_rwkv_shift, state_rwkv_wkv, state_gdn_conv, state_gdn_S, state_hgrn_S, state_ffn_conv, norm_mix_pre, norm_mix_post, norm_ffn_pre, norm_ffn_post, rwkv_mix, rwkv_w_rkv, rwkv_w0, rwkv_w1, rwkv_w2, rwkv_a0, rwkv_a1, rwkv_a2, rwkv_v0, rwkv_v1, rwkv_v2, rwkv_g1, rwkv_g2, rwkv_k_k, rwkv_k_a, rwkv_r_k, rwkv_lnx_w, rwkv_lnx_b, rwkv_w_o, gdn_w_in, gdn_conv_w, gdn_a_log, gdn_dt_bias, gdn_norm_w, gdn_w_o, hgrn_w_in, hgrn_lb, hgrn_norm_w, hgrn_w_o, ffn_w_up, ffn_conv_w, ffn_w_down):
    P = dict(norm_mix_pre=norm_mix_pre, norm_mix_post=norm_mix_post, norm_ffn_pre=norm_ffn_pre,
             norm_ffn_post=norm_ffn_post, rwkv_mix=rwkv_mix, rwkv_w_rkv=rwkv_w_rkv, rwkv_w0=rwkv_w0,
             rwkv_w1=rwkv_w1, rwkv_w2=rwkv_w2, rwkv_a0=rwkv_a0, rwkv_a1=rwkv_a1, rwkv_a2=rwkv_a2,
             rwkv_v0=rwkv_v0, rwkv_v1=rwkv_v1, rwkv_v2=rwkv_v2, rwkv_g1=rwkv_g1, rwkv_g2=rwkv_g2,
             rwkv_k_k=rwkv_k_k, rwkv_k_a=rwkv_k_a, rwkv_r_k=rwkv_r_k, rwkv_lnx_w=rwkv_lnx_w,
             rwkv_lnx_b=rwkv_lnx_b, rwkv_w_o=rwkv_w_o, gdn_w_in=gdn_w_in, gdn_conv_w=gdn_conv_w,
             gdn_a_log=gdn_a_log, gdn_dt_bias=gdn_dt_bias, gdn_norm_w=gdn_norm_w, gdn_w_o=gdn_w_o,
             hgrn_w_in=hgrn_w_in, hgrn_lb=hgrn_lb, hgrn_norm_w=hgrn_norm_w, hgrn_w_o=hgrn_w_o,
             ffn_w_up=ffn_w_up, ffn_conv_w=ffn_conv_w, ffn_w_down=ffn_w_down)
    for name in ('rwkv_w_rkv', 'rwkv_w_o', 'gdn_w_o', 'hgrn_w_in', 'hgrn_w_o', 'ffn_w_up', 'ffn_w_down'):
        P[name + '_bf'] = P[name].astype(BF16)
    Bp, Tp, _ = x_prompt.shape
    Bs, Ts, _ = x_sample.shape
    zero_like = lambda st: jnp.zeros((st.shape[0], Bp) + st.shape[2:], st.dtype)
    y_p, (p_shift, p_wkv, p_gconv, p_gS, p_hS, p_fconv) = _trunk(
        _Group(Bp, Tp, False), x_prompt, zero_like(state_rwkv_shift), zero_like(state_rwkv_wkv),
        zero_like(state_gdn_conv), zero_like(state_gdn_S), zero_like(state_hgrn_S), zero_like(state_ffn_conv), P)
    y_s, (s_shift, s_wkv, s_gconv, s_gS, s_hS, s_fconv) = _trunk(
        _Group(Bs, Ts, True), x_sample, state_rwkv_shift, state_rwkv_wkv, state_gdn_conv, state_gdn_S,
        state_hgrn_S, state_ffn_conv, P)
    return (y_p, y_s, p_shift, s_shift, p_wkv, s_wkv, p_gconv, s_gconv,
            p_gS, s_gS, p_hS, s_hS, p_fconv, s_fconv)
```

```python
import functools
import math

import jax
import jax.numpy as jnp
from jax import lax
from jax.experimental import pallas as pl
from jax.experimental.pallas import tpu as pltpu

F32 = jnp.float32
BF16 = jnp.bfloat16

NORM_EPS = 1e-6
L2_EPS = 1e-6
RWKV_LNX_EPS = 64e-5
RWKV_N = 64
LANES = 128
SUBLANES = 8
VMEM_LIMIT = 56 * 1024 * 1024
ROW_TILE = 256
WIDE_TILE = 512
FFN_TILE = 256
SCAN_ROWS = 256
SCAN_HEADS = 4
SCAN_CHUNK = 64
GLA_SUB = 8


def _dg(a, b, ca, cb):
    return lax.dot_general(a, b, (((ca,), (cb,)), ((), ())), preferred_element_type=F32)


def _mm(a, b):
    return _dg(a.astype(BF16), b.astype(BF16), 1, 0)


def _mm_nt(a, b):
    return _dg(a.astype(BF16), b.astype(BF16), 1, 1)


def _mm_tn(a, b):
    return _dg(a.astype(BF16), b.astype(BF16), 0, 0)


def _split3(x):
    hi = x.astype(BF16)
    r1 = x - hi.astype(F32)
    mid = r1.astype(BF16)
    lo = (r1 - mid.astype(F32)).astype(BF16)
    return hi, mid, lo


def _mm_xl(m, x):
    h, mi, lo = _split3(x)
    m = m.astype(BF16)
    return _dg(m, h, 1, 0) + (_dg(m, mi, 1, 0) + _dg(m, lo, 1, 0))


def _iota2(n, m):
    return (lax.broadcasted_iota(jnp.int32, (n, m), 0), lax.broadcasted_iota(jnp.int32, (n, m), 1))


def _sigmoid(x):
    return 1.0 / (1.0 + jnp.exp(-x))


def _silu(x):
    return x * _sigmoid(x)


def _softplus(x):
    return jnp.maximum(x, 0.0) + jnp.log(1.0 + jnp.exp(-jnp.abs(x)))


def _rmsnorm(x, w):
    return x * lax.rsqrt(jnp.mean(x * x, axis=-1, keepdims=True) + NORM_EPS) * w


def _head_sum(x, e):
    return _mm(x, e)


def _unit_lower_inv(Ls, C):
    n = Ls[0].shape[0]
    ri, ci = _iota2(n, n)
    eye = (ri == ci).astype(F32)
    Xs = [eye - L for L in Ls]
    Ps = list(Ls)
    m = 2
    while m < C:
        Ps = [_mm(P, P) for P in Ps]
        Xs = [X + _mm(X, P) for X, P in zip(Xs, Ps)]
        m *= 2
    return Xs


def _tile_masks(C):
    sh = C.bit_length() - 1
    ri, ci = _iota2(LANES, LANES)
    same = jnp.right_shift(ri, sh) == jnp.right_shift(ci, sh)
    return ri, ci, same, same & (ri >= ci), same & (ri > ci)


def _place_rows(x, r0, n):
    parts = []
    if r0:
        parts.append(jnp.zeros((r0, x.shape[1]), x.dtype))
    parts.append(x)
    if n - r0 - x.shape[0]:
        parts.append(jnp.zeros((n - r0 - x.shape[0], x.shape[1]), x.dtype))
    return jnp.concatenate(parts, axis=0) if len(parts) > 1 else x


def _rwkv_tile(r, lw, k, v, kk, bb, C):
    ri, ci, same, tril, stril = _tile_masks(C)
    lane = lax.broadcasted_iota(jnp.int32, (1, LANES), 1)
    m = [(lane < RWKV_N).astype(F32), (lane >= RWKV_N).astype(F32)]
    trilf = tril.astype(F32)
    n = len(r)
    gc = [_mm_xl(trilf, x) for x in lw]
    at = [-kk[g] * jnp.exp(gc[g] - lw[g]) for g in range(n)]
    rt = [r[g] * jnp.exp(gc[g]) for g in range(n)]
    sc = []
    for g in range(n):
        e_neg = jnp.exp(-gc[g])
        lhs = jnp.concatenate([at[g] * m[0], at[g] * m[1], rt[g] * m[0], rt[g] * m[1]], axis=0)
        sc.append(_mm_nt(lhs, jnp.concatenate([bb[g] * e_neg, k[g] * e_neg], axis=0)))
    gh = [(g, hd) for g in range(n) for hd in range(2)]
    a_blk = [sc[g][LANES * hd:LANES * (hd + 1)] for g, hd in gh]
    r_blk = [sc[g][LANES * (2 + hd):LANES * (3 + hd)] for g, hd in gh]
    A_rb = [jnp.where(tril, x[:, :LANES], 0.0) for x in r_blk]
    A_rk = [jnp.where(tril, x[:, LANES:], 0.0) for x in r_blk]
    Tinv = _unit_lower_inv([jnp.where(stril, -x[:, :LANES], 0.0) for x in a_blk], C)
    akv = [_mm(jnp.where(stril, a_blk[p][:, LANES:], 0.0), v[g]) for p, (g, hd) in enumerate(gh)]
    y = [_mm(Tinv[p], jnp.concatenate([at[g] * m[hd], akv[p]], axis=1)) for p, (g, hd) in enumerate(gh)]
    o0 = [_mm(jnp.concatenate([A_rk[p], A_rb[p]], axis=1), jnp.concatenate([v[g], y[p][:, LANES:]], axis=0))
          for p, (g, hd) in enumerate(gh)]
    out = []
    for g in range(n):
        p0, p1 = 2 * g, 2 * g + 1
        out.append(dict(gc=gc[g], rt=rt[g], m=m, Wa=y[p0][:, :LANES] + y[p1][:, :LANES],
                        U0=y[p0][:, LANES:] * m[0] + y[p1][:, LANES:] * m[1],
                        O0=o0[p0] * m[0] + o0[p1] * m[1],
                        A_rb=jnp.concatenate([A_rb[p0], A_rb[p1]], axis=0), k=k[g], v=v[g], bb=bb[g]))
    return out


def _rwkv_block(i, C, ts, Ss):
    sl = slice(C * i, C * (i + 1))
    sl1 = slice(LANES + C * i, LANES + C * (i + 1))
    r2, c2 = _iota2(LANES, LANES)
    blk = ((r2 >= RWKV_N) == (c2 >= RWKV_N)).astype(F32)
    xs = [_mm_nt(jnp.concatenate([t['Wa'][sl], t['rt'][sl]], axis=0), S) for t, S in zip(ts, Ss)]
    S_new, ax = [], []
    for t, S, x in zip(ts, Ss, xs):
        X = x[:C]
        U = t['U0'][sl] + X
        gl = t['gc'][C * (i + 1) - 1:C * (i + 1), :]
        e_last = jnp.exp(gl - t['gc'][sl])
        S_new.append(S * jnp.exp(gl) + blk * _mm_tn(
            jnp.concatenate([U, t['v'][sl]], axis=0),
            jnp.concatenate([t['bb'][sl] * e_last, t['k'][sl] * e_last], axis=0)))
        ax.append(_mm(jnp.concatenate([t['A_rb'][sl], t['A_rb'][sl1]], axis=0), _place_rows(X, C * i, LANES)))
    o = [t['O0'][sl] + x[C:] + a[:C] * t['m'][0] + a[C:] * t['m'][1] for t, x, a in zip(ts, xs, ax)]
    return o, S_new


def _gdn_tile(q, k, v, g_row, beta_row, C):
    ri, ci, same, tril, stril = _tile_masks(C)
    eye = ri == ci
    lsum = lambda msk, x: jnp.sum(jnp.where(msk, x, 0.0), axis=1, keepdims=True)
    n = len(q)
    pre = []
    for g in range(n):
        g_b = jnp.broadcast_to(g_row[g], (LANES, LANES))
        beta_col = lsum(eye, jnp.broadcast_to(beta_row[g], (LANES, LANES)))
        g_col = lsum(eye, g_b)
        gc_col = lsum(tril, g_b)
        gl_col = lsum(same, g_b)
        gc_row = jnp.sum(jnp.where(same & (ri <= ci), jnp.broadcast_to(g_col, (LANES, LANES)), 0.0),
                         axis=0, keepdims=True)
        decay = jnp.where(tril, jnp.exp(jnp.minimum(gc_col - gc_row, 0.0)), 0.0)
        pre.append(dict(beta=beta_col, gc=gc_col, gl=gl_col, decay=decay, kb=k[g] * beta_col))
    sc = [_mm_nt(jnp.concatenate([pre[g]['kb'], q[g]], axis=0), k[g]) for g in range(n)]
    Tinv = _unit_lower_inv([jnp.where(stril, sc[g][:LANES] * pre[g]['decay'], 0.0) for g in range(n)], C)
    out = []
    for g in range(n):
        p = pre[g]
        e_gc = jnp.exp(p['gc'])
        uw = _mm(Tinv[g], jnp.concatenate([v[g] * p['beta'], p['kb'] * e_gc], axis=1))
        out.append(dict(u=uw[:, :LANES], wk=uw[:, LANES:], qk=sc[g][LANES:] * p['decay'], qg=q[g] * e_gc,
                        kd=k[g] * jnp.exp(p['gl'] - p['gc']), egl=jnp.exp(p['gl'])))
    return out


def _gdn_block(i, C, ts, Ss):
    sl = slice(C * i, C * (i + 1))
    ws = [_mm(jnp.concatenate([t['wk'][sl], t['qg'][sl]], axis=0), S) for t, S in zip(ts, Ss)]
    v_new = [t['u'][sl] - w[:C] for t, w in zip(ts, ws)]
    S_new = [S * t['egl'][C * i:C * i + 1, :] + _mm_tn(t['kd'][sl], vn) for t, S, vn in zip(ts, Ss, v_new)]
    o = [w[C:] + _mm(t['qk'][sl], _place_rows(vn, C * i, LANES)) for t, w, vn in zip(ts, ws, v_new)]
    return o, S_new


def _gla_tile(q_ref, k_ref, lf_ref, v_ref, probs, C, gc_scr, prod_scr):
    n = len(probs)
    nsb = LANES // GLA_SUB
    ri, ci, same, tril, _ = _tile_masks(C)
    sub = GLA_SUB.bit_length() - 1
    rsub, csub = jnp.right_shift(ri, sub), jnp.right_shift(ci, sub)
    trilf = tril.astype(F32)
    get = lambda ref: [_tile(ref, p) for p in probs]
    q, k, v = get(q_ref), get(k_ref), get(v_ref)
    gc = [_mm_xl(trilf, x) for x in get(lf_ref)]
    rk, ck = _iota2(GLA_SUB * LANES, LANES)
    sel = (jnp.right_shift(rk, LANES.bit_length() - 1) == ck).astype(BF16)
    for g, (h, j) in enumerate(probs):
        gc_scr[g] = gc[g]
        for I in range(nsb):
            s0 = GLA_SUB * I
            ksub = k[g][s0:s0 + GLA_SUB]
            gsub = gc[g][s0:s0 + GLA_SUB]
            for ii in range(GLA_SUB):
                i = s0 + ii
                gi = gc_scr[g, i:i + 1, :]
                qi = q_ref[LANES * j + i:LANES * j + i + 1, LANES * h:LANES * (h + 1)]
                prod_scr[g, s0:s0 + GLA_SUB, LANES * ii:LANES * (ii + 1)] = (
                    qi * ksub * jnp.exp(jnp.minimum(gi - gsub, 0.0)))
    d = [_mm(prod_scr[g], sel) for g in range(n)]
    AT = []
    for g in range(n):
        strips = [d[g][:GLA_SUB]] + [pltpu.roll(d[g][GLA_SUB * I:GLA_SUB * (I + 1)], GLA_SUB * I, axis=1)
                                     for I in range(1, nsb)]
        AT.append(jnp.where((rsub == csub) & (ri <= ci), jnp.concatenate(strips, axis=0), 0.0))
    nsub = C // GLA_SUB
    if nsub > 1:
        for g in range(n):
            off = []
            for b in range(LANES // C):
                b0 = C * b
                kblk = k[g][b0:b0 + C]
                gblk = gc[g][b0:b0 + C]
                acc = jnp.zeros((C, LANES), F32)
                for J in range(1, nsub):
                    s0 = b0 + GLA_SUB * J
                    ref = gc_scr[g, s0 - 1:s0, :]
                    khat = kblk * jnp.exp(jnp.minimum(ref - gblk, 0.0))
                    qhat = q[g][s0:s0 + GLA_SUB] * jnp.exp(gc[g][s0:s0 + GLA_SUB] - ref)
                    acc = acc + _mm_nt(khat, _place_rows(qhat, s0, LANES))
                off.append(acc)
            AT[g] = AT[g] + jnp.where(same & (rsub < csub), jnp.concatenate(off, axis=0), 0.0)
    o_intra = [_mm_tn(AT[g], v[g]) for g in range(n)]
    return [dict(gc=gc[g], k=k[g], v=v[g], qg=q[g] * jnp.exp(gc[g]), o_intra=o_intra[g]) for g in range(n)]


def _gla_block(i, C, ts, STs):
    sl = slice(C * i, C * (i + 1))
    o, ST_new = [], []
    for t, ST in zip(ts, STs):
        gl = t['gc'][C * (i + 1) - 1:C * (i + 1), :]
        o.append(t['o_intra'][sl] + _mm_nt(t['qg'][sl], ST))
        ST_new.append(ST * jnp.exp(gl) + _mm_tn(t['v'][sl], t['k'][sl] * jnp.exp(gl - t['gc'][sl])))
    return o, ST_new


def _shifted(ext, hp, s, tm, back):
    return ext[hp - back * s:hp - back * s + tm, :]


def _rwkv_pre_kernel(has_vres, s, *refs):
    if has_vres:
        (x_ref, vf_ref, sh0_ref, nw_ref, mix_ref, wrkv_ref, w0_ref, w1_ref, w2_ref, a0_ref, a1_ref, a2_ref,
         g1_ref, g2_ref, kkw_ref, ka_ref, e_ref, v0_ref, v1_ref, v2_ref,
         r_out, lw_out, k_out, v_out, kk_out, bb_out, g_out, sh_out, ext) = refs
    else:
        (x_ref, sh0_ref, nw_ref, mix_ref, wrkv_ref, w0_ref, w1_ref, w2_ref, a0_ref, a1_ref, a2_ref,
         g1_ref, g2_ref, kkw_ref, ka_ref, e_ref,
         r_out, lw_out, k_out, v_out, kk_out, bb_out, g_out, sh_out, ext) = refs
    t = pl.program_id(1)
    tm, D = x_ref.shape
    hp = ext.shape[0] - tm
    h = _rmsnorm(x_ref[...], nw_ref[...])

    @pl.when(t == 0)
    def _():
        ext[hp - s:hp, :] = sh0_ref[...]

    ext[hp:, :] = h
    d = _shifted(ext, hp, s, tm, 1) - h
    ext[hp - s:hp, :] = h[tm - s:, :]

    @pl.when(t == pl.num_programs(1) - 1)
    def _():
        sh_out[...] = h[tm - s:, :]

    mixed = lambda i: h + d * mix_ref[i:i + 1, :]
    r = _mm(mixed(0), wrkv_ref[0])
    z = w0_ref[...] + _mm(jnp.tanh(_mm(mixed(1), w1_ref[...])), w2_ref[...])
    lw_out[...] = -math.exp(-0.5) * _sigmoid(z)
    k = _mm(mixed(2), wrkv_ref[1])
    xv = mixed(3)
    v = _mm(xv, wrkv_ref[2])
    a = _sigmoid(a0_ref[...] + _mm(_mm(mixed(4), a1_ref[...]), a2_ref[...]))
    if has_vres:
        gate_v = _sigmoid(v0_ref[...] + _mm(_mm(xv, v1_ref[...]), v2_ref[...]))
        v = v + (vf_ref[...] - v) * gate_v
    g_out[...] = _mm(_sigmoid(_mm(mixed(5), g1_ref[...])), g2_ref[...]).astype(g_out.dtype)
    r_out[...] = r.astype(r_out.dtype)
    v_out[...] = v
    kkraw = k * kkw_ref[...]
    e = e_ref[...]
    for c in range(D // LANES):
        sl = slice(c * LANES, (c + 1) * LANES)
        kc = kkraw[:, sl]
        kkn = kc * lax.rsqrt(_head_sum(kc * kc, e) + L2_EPS)
        kk_out[:, sl] = kkn.astype(kk_out.dtype)
        bb_out[:, sl] = (kkn * a[:, sl]).astype(bb_out.dtype)
    k_out[...] = (k * (1.0 + (a - 1.0) * ka_ref[...])).astype(k_out.dtype)


def _gdn_pre_kernel(s, nheads, x_ref, c0_ref, nw_ref, wqkv_ref, wgate_ref, wbg_ref, cw_ref, alog_ref, dtb_ref,
                    q_out, k_out, v_out, gate_out, bg_out, cnew_out, ext):
    t = pl.program_id(1)
    tm, D = x_ref.shape
    hp = ext.shape[0] - tm
    nh = cw_ref.shape[0] - 1
    h = _rmsnorm(x_ref[...], nw_ref[...])

    @pl.when(t == 0)
    def _():
        ext[hp - nh * s:hp, :] = c0_ref[...]

    ext[hp:, :] = _mm(h, wqkv_ref[...])
    y = _shifted(ext, hp, s, tm, 0) * cw_ref[nh:nh + 1, :]
    for j in range(nh):
        y = y + _shifted(ext, hp, s, tm, nh - j) * cw_ref[j:j + 1, :]
    hist = ext[hp + tm - nh * s:hp + tm, :]

    @pl.when(t == pl.num_programs(1) - 1)
    def _():
        cnew_out[...] = hist

    ext[hp - nh * s:hp, :] = hist
    y = _silu(y)
    qk_w = D
    for c in range(qk_w // LANES):
        sl = slice(c * LANES, (c + 1) * LANES)
        qc = y[:, sl]
        q_out[:, sl] = (qc * (lax.rsqrt(jnp.sum(qc * qc, axis=-1, keepdims=True) + L2_EPS) * LANES ** -0.5)
                        ).astype(q_out.dtype)
        sl2 = slice(qk_w + c * LANES, qk_w + (c + 1) * LANES)
        kc = y[:, sl2]
        k_out[:, sl] = (kc * lax.rsqrt(jnp.sum(kc * kc, axis=-1, keepdims=True) + L2_EPS)).astype(k_out.dtype)
    v_out[...] = y[:, 2 * qk_w:].astype(v_out.dtype)
    gate_out[...] = _mm(h, wgate_ref[...]).astype(gate_out.dtype)
    rest = _mm(h, wbg_ref[...])
    lane = lax.broadcasted_iota(jnp.int32, rest.shape, 1)
    bg_out[...] = jnp.where(lane < nheads, _sigmoid(rest),
                            -jnp.exp(alog_ref[...]) * _softplus(rest + dtb_ref[...]))


def _hgrn_pre_kernel(layer, x_ref, nw_ref, win_ref, lb_ref, q_out, k_out, lf_out, v_out, gate_out):
    D = x_ref.shape[1]
    h = _rmsnorm(x_ref[...], nw_ref[...])
    lbp = lb_ref[...]
    ex = jnp.exp(lbp - jnp.max(lbp, axis=0, keepdims=True))
    soft = ex / jnp.sum(ex, axis=0, keepdims=True)
    row = lax.broadcasted_iota(jnp.int32, soft.shape, 0)
    lb = jnp.sum(jnp.where((row >= 1) & (row <= layer), soft, 0.0), axis=0, keepdims=True)
    p = _mm(h, win_ref[...])
    q_out[...] = _silu(p[:, :D])
    f = lb + (1.0 - lb) * _sigmoid(p[:, D:2 * D])
    k_out[...] = 1.0 - f
    lf_out[...] = jnp.log(f)
    v_out[...] = p[:, 2 * D:3 * D].astype(v_out.dtype)
    gate_out[...] = p[:, 3 * D:].astype(gate_out.dtype)


def _ffn_kernel(s, y_ref, x_ref, c0_ref, wo_ref, mpw_ref, nw_ref, wup_ref, cw_ref, wdn_ref, pw_ref, x_out, cnew_out,
                ext):
    t = pl.program_id(1)
    tm, D = x_ref.shape
    hp = ext.shape[0] - tm
    nh = cw_ref.shape[0] - 1
    dff = wdn_ref.shape[0]
    x = x_ref[...] + _rmsnorm(_mm(y_ref[...], wo_ref[...]), mpw_ref[...])
    h = _rmsnorm(x, nw_ref[...])

    @pl.when(t == 0)
    def _():
        ext[hp - nh * s:hp, :] = c0_ref[...]

    ext[hp:, :] = _mm(h, wup_ref[...])
    y = _shifted(ext, hp, s, tm, 0) * cw_ref[nh:nh + 1, :]
    for j in range(nh):
        y = y + _shifted(ext, hp, s, tm, nh - j) * cw_ref[j:j + 1, :]
    hist = ext[hp + tm - nh * s:hp + tm, :]

    @pl.when(t == pl.num_programs(1) - 1)
    def _():
        cnew_out[...] = hist

    ext[hp - nh * s:hp, :] = hist
    act = _silu(y[:, dff:]) * y[:, :dff]
    x_out[...] = x + _rmsnorm(_mm(act, wdn_ref[...]), pw_ref[...])


def _const_spec(a):
    if isinstance(a, tuple):
        a, i = a
        nd = a.ndim - 1
        return pl.BlockSpec((None,) + a.shape[1:], lambda g, t: (i,) + (0,) * nd, pipeline_mode=pl.Buffered(1))
    nd = a.ndim
    return pl.BlockSpec(a.shape, lambda g, t: (0,) * nd, pipeline_mode=pl.Buffered(1))


def _row_call(body, name, tm, tiled_ins, group_ins, const_ins, tiled_out_widths, group_out_shapes, scratch):
    G, R, _ = tiled_ins[0].shape
    assert R % tm == 0
    tile_spec = lambda c: pl.BlockSpec((None, tm, c), lambda g, t: (g, t, 0))
    group_spec = lambda n, c: pl.BlockSpec((None, n, c), lambda g, t: (g, 0, 0))
    group_in_spec = lambda n, c: pl.BlockSpec((None, n, c), lambda g, t: (g, 0, 0), pipeline_mode=pl.Buffered(1))
    in_specs = ([tile_spec(a.shape[2]) for a in tiled_ins] + [group_in_spec(*a.shape[1:]) for a in group_ins]
                + [_const_spec(a) for a in const_ins])
    tiled_outs = [c if isinstance(c, tuple) else (c, F32) for c in tiled_out_widths]
    out_specs = [tile_spec(c) for c, _ in tiled_outs] + [group_spec(n, c) for n, c in group_out_shapes]
    out_shape = ([jax.ShapeDtypeStruct((G, R, c), dt) for c, dt in tiled_outs]
                 + [jax.ShapeDtypeStruct((G, n, c), F32) for n, c in group_out_shapes])
    return pl.pallas_call(
        body, name=name, grid=(G, R // tm), in_specs=in_specs, out_specs=out_specs, out_shape=out_shape,
        scratch_shapes=scratch,
        compiler_params=pltpu.CompilerParams(dimension_semantics=("parallel", "arbitrary"),
                                             vmem_limit_bytes=VMEM_LIMIT),
    )(*tiled_ins, *group_ins, *[a[0] if isinstance(a, tuple) else a for a in const_ins])


def _hist_pad(n):
    return -(-n // SUBLANES) * SUBLANES


def _tile(ref, p):
    h, j = p
    return ref[LANES * j:LANES * (j + 1), LANES * h:LANES * (h + 1)].astype(F32)


def _scan_states(chained, C, probs, tiles, scr, load, store, block_fn):
    nb = LANES // C
    tt = pl.program_id(2)
    heads = sorted({h for h, _ in probs})
    nt = len(probs) // len(heads)
    outs = {}
    if chained:
        @pl.when(tt == 0)
        def _():
            for h in heads:
                scr[h] = load((h,))

        Ss = [scr[h] for h in heads]
        for j in range(nt):
            sel = [tiles[probs.index((h, j))] for h in heads]
            for i in range(nb):
                os, Ss = block_fn(i, sel, Ss)
                for h, o in zip(heads, os):
                    outs[(h, j, i)] = o
        for h, S in zip(heads, Ss):
            scr[h] = S

        @pl.when(tt == pl.num_programs(2) - 1)
        def _():
            for h, S in zip(heads, Ss):
                store((h,), S)
    else:
        for i in range(nb):
            Ss = [load((j * nb + i, h)) for h, j in probs]
            os, Ss = block_fn(i, tiles, Ss)
            for (h, j), o, S in zip(probs, os, Ss):
                outs[(h, j, i)] = o
                store((j * nb + i, h), S)
    return [jnp.concatenate([outs[(h, j, i)] for i in range(nb)], axis=0) if nb > 1 else outs[(h, j, 0)]
            for h, j in probs]


def _scan_probs(ref):
    return [(h, j) for h in range(ref.shape[1] // LANES) for j in range(ref.shape[0] // LANES)]


def _rwkv_scan_kernel(C, chained, r_ref, lw_ref, k_ref, v_ref, kk_ref, bb_ref, g_ref, rk_ref, lnw_ref, lnb_ref,
                      e_ref, s0_ref, y_out, s_out, scr, bd):
    N = RWKV_N
    e = e_ref[...]
    probs = _scan_probs(r_ref)
    nt = r_ref.shape[0] // LANES
    nb = LANES // C
    get = lambda ref: [_tile(ref, p) for p in probs]
    r, k, v = get(r_ref), get(k_ref), get(v_ref)
    tiles = _rwkv_tile(r, get(lw_ref), k, v, get(kk_ref), get(bb_ref), C)

    bd[...] = jnp.zeros(bd.shape, F32)
    slot = lambda idx: idx[-1] * nt + (idx[0] // nb if len(idx) > 1 else 0)
    heads = lambda idx: (idx[:-1] + (2 * idx[-1],), idx[:-1] + (2 * idx[-1] + 1,))

    def load(idx):
        p = slot(idx)
        h0, h1 = heads(idx)
        bd[p, :N, :N] = s0_ref[h0]
        bd[p, N:, N:] = s0_ref[h1]
        return bd[p]

    def store(idx, S):
        p = slot(idx)
        h0, h1 = heads(idx)
        bd[p] = S
        s_out[h0] = bd[p, :N, :N]
        s_out[h1] = bd[p, N:, N:]

    O = _scan_states(chained, C, probs, tiles, scr, load, store, lambda i, ts, Ss: _rwkv_block(i, C, ts, Ss))
    np_ = len(probs)
    lanes = [slice(LANES * h, LANES * (h + 1)) for h, _ in probs]
    s1 = _head_sum(jnp.concatenate(O + [r[n] * k[n] * rk_ref[:, lanes[n]] for n in range(np_)], axis=0), e)
    d = [O[n] - s1[LANES * n:LANES * (n + 1)] * (1.0 / N) for n in range(np_)]
    s2 = _head_sum(jnp.concatenate([x * x for x in d], axis=0), e)
    for n, (h, j) in enumerate(probs):
        var = s2[LANES * n:LANES * (n + 1)] * (1.0 / N)
        on = d[n] * lax.rsqrt(var + RWKV_LNX_EPS) * lnw_ref[:, lanes[n]] + lnb_ref[:, lanes[n]]
        bonus = s1[LANES * (np_ + n):LANES * (np_ + n + 1)] * v[n]
        y_out[LANES * j:LANES * (j + 1), lanes[n]] = ((on + bonus) * _tile(g_ref, (h, j))).astype(y_out.dtype)


def _gdn_scan_kernel(C, chained, q_ref, k_ref, v_ref, gate_ref, g_ref, beta_ref, nw_ref, s0_ref, y_out, s_out,
                     scr):
    probs = _scan_probs(q_ref)
    get = lambda ref: [_tile(ref, p) for p in probs]
    tiles = _gdn_tile(get(q_ref), get(k_ref), get(v_ref), [g_ref[h, j:j + 1, :] for h, j in probs],
                      [beta_ref[h, j:j + 1, :] for h, j in probs], C)

    def store(idx, S):
        s_out[idx] = S

    O = _scan_states(chained, C, probs, tiles, scr, lambda idx: s0_ref[idx], store,
                     lambda i, ts, Ss: _gdn_block(i, C, ts, Ss))
    for n, (h, j) in enumerate(probs):
        y_out[LANES * j:LANES * (j + 1), LANES * h:LANES * (h + 1)] = (
            _rmsnorm(O[n], nw_ref[...]) * _silu(_tile(gate_ref, (h, j)))).astype(y_out.dtype)


def _gla_scan_kernel(C, chained, q_ref, k_ref, lf_ref, v_ref, gate_ref, nw_ref, s0_ref, y_out, s_out, scr, gc_scr,
                     prod_scr):
    probs = _scan_probs(q_ref)
    tiles = _gla_tile(q_ref, k_ref, lf_ref, v_ref, probs, C, gc_scr, prod_scr)

    def store(idx, ST):
        s_out[idx] = ST.T

    O = _scan_states(chained, C, probs, tiles, scr, lambda idx: s0_ref[idx].T, store,
                     lambda i, ts, Ss: _gla_block(i, C, ts, Ss))
    for n, (h, j) in enumerate(probs):
        y_out[LANES * j:LANES * (j + 1), LANES * h:LANES * (h + 1)] = (
            _rmsnorm(O[n], nw_ref[...]) * _silu(_tile(gate_ref, (h, j)))).astype(y_out.dtype)


def _scan_call(body, name, C, chained, tiles, extra_ins, extra_specs, s0, layer, nlayers, s_prev, nheads,
               extra_scratch=()):
    NB, RB, D = tiles[0].shape
    rows = min(SCAN_ROWS, RB)
    hg = SCAN_HEADS
    assert RB % rows == 0 and rows % LANES == 0 and nheads % hg == 0
    tile_spec = pl.BlockSpec((None, rows, hg * LANES), lambda b, h, t: (b, t, h))
    s_shape = s0.shape[-4:]
    hs = hg * (s_shape[1] // nheads)
    nt = RB // rows
    lead = (None,) if chained else (rows // C,)
    first = (lambda b, t: b) if chained else (lambda b, t: b * nt + t)

    def s_spec(l):
        if l is None:
            return pl.BlockSpec(lead + (hs,) + s_shape[2:], lambda b, h, t: (first(b, t), h, 0, 0))
        return pl.BlockSpec((None,) + lead + (hs,) + s_shape[2:], lambda b, h, t: (l, first(b, t), h, 0, 0))

    s0_layer = None if s0.ndim == 4 else layer
    n_in = len(tiles) + len(extra_ins) + 1
    kern = functools.partial(body, C, chained)
    ins = [*tiles, *extra_ins, s0]
    in_specs = [tile_spec] * len(tiles) + list(extra_specs) + [s_spec(s0_layer)]
    aliases = {}
    if s_prev is not None:
        ins.append(s_prev)
        in_specs.append(pl.BlockSpec(memory_space=pl.ANY))
        aliases = {n_in: 1}
        kern = lambda *refs: body(C, chained, *refs[:n_in], *refs[n_in + 1:])
    return pl.pallas_call(
        kern, name=name, grid=(NB, nheads // hg, RB // rows),
        in_specs=in_specs, out_specs=[tile_spec, s_spec(layer)],
        out_shape=[jax.ShapeDtypeStruct((NB, RB, D), BF16),
                   jax.ShapeDtypeStruct((nlayers,) + s_shape, F32)],
        scratch_shapes=[pltpu.VMEM((hg, LANES, LANES), F32)] + list(extra_scratch),
        input_output_aliases=aliases,
        compiler_params=pltpu.CompilerParams(dimension_semantics=("parallel", "parallel", "arbitrary"),
                                             vmem_limit_bytes=VMEM_LIMIT),
    )(*ins)


class _Group:
    def __init__(self, B, T, time_major):
        self.B, self.T, self.tm_major = B, T, time_major
        if time_major:
            self.G, self.R, self.s = 1, B * T, B
            self.C = -(-T // SUBLANES) * SUBLANES
            assert LANES % self.C == 0 and (B * self.C) % LANES == 0
        else:
            self.G, self.R, self.s = B, T, 1
            self.C = SCAN_CHUNK
            assert T % LANES == 0
        self.chained = not time_major
        self.tile = min(ROW_TILE, self.R)
        self.wide_tile = min(WIDE_TILE, self.R)
        self.ffn_tile = self.s if time_major else min(FFN_TILE, self.R)

    def to_rows(self, x):
        if self.tm_major:
            return jnp.swapaxes(x, 0, 1).reshape(1, self.R, x.shape[-1])
        return x

    def hist_to_rows(self, h):
        if self.tm_major:
            return jnp.swapaxes(h, 0, 1).reshape(1, -1, h.shape[-1])
        return h

    def hist_from_rows(self, h, n):
        if self.tm_major:
            return jnp.swapaxes(h.reshape(n, self.B, h.shape[-1]), 0, 1)
        return h

    def to_scan(self, a):
        if not self.tm_major:
            return a
        a = jnp.swapaxes(a.reshape(self.T, self.B, a.shape[-1]), 0, 1)
        a = jnp.pad(a, ((0, 0), (0, self.C - self.T), (0, 0)))
        return a.reshape(1, self.B * self.C, a.shape[-1])

    def from_scan(self, a):
        if not self.tm_major:
            return a
        return self.to_rows(a.reshape(self.B, self.C, a.shape[-1])[:, :self.T])


def _rwkv_layer(grp, x, shift0, S0, S_prev, v_first, P, j):
    D = x.shape[-1]
    s, tm = grp.s, grp.wide_tile
    N = RWKV_N
    has_vres = v_first is not None
    r2, c2 = _iota2(LANES, LANES)
    e = ((r2 // N) == (c2 // N)).astype(BF16)
    row = lambda a: a.reshape(1, -1)
    tiled = [x] + ([v_first] if has_vres else [])
    consts = [row(P['norm_mix_pre_i']), P['rwkv_mix'][j], (P['rwkv_w_rkv_bf'], j),
              row(P['rwkv_w0'][j]), P['rwkv_w1'][j].astype(BF16), P['rwkv_w2'][j].astype(BF16),
              row(P['rwkv_a0'][j]), P['rwkv_a1'][j].astype(BF16), P['rwkv_a2'][j].astype(BF16),
              P['rwkv_g1'][j].astype(BF16), P['rwkv_g2'][j].astype(BF16),
              row(P['rwkv_k_k'][j]), row(P['rwkv_k_a'][j]), e]
    if has_vres:
        consts += [row(P['rwkv_v0'][j - 1]), P['rwkv_v1'][j - 1].astype(BF16), P['rwkv_v2'][j - 1].astype(BF16)]
    r, lw, k, v, kk, bb, g, shift = _row_call(
        functools.partial(_rwkv_pre_kernel, has_vres, s), f"rwkv_pre_{j}", tm, tiled, [shift0], consts,
        [(D, BF16), D, (D, BF16), D, (D, BF16), (D, BF16), (D, BF16)], [(s, D)],
        [pltpu.VMEM((_hist_pad(s) + tm, D), F32)])
    if not has_vres:
        v_first = v
    tiles = [grp.to_scan(a) for a in (r, lw, k, v, kk, bb, g)]
    nprob = SCAN_HEADS * (min(SCAN_ROWS, tiles[0].shape[1]) // LANES)
    vec_spec = pl.BlockSpec((1, SCAN_HEADS * LANES), lambda b, h, t: (0, h))
    e_spec = pl.BlockSpec((LANES, LANES), lambda b, h, t: (0, 0))
    y, S = _scan_call(_rwkv_scan_kernel, f"rwkv_scan_{j}", grp.C, grp.chained, tiles,
                      [row(P['rwkv_r_k'][j]), row(P['rwkv_lnx_w'][j]), row(P['rwkv_lnx_b'][j]), e],
                      [vec_spec] * 3 + [e_spec], S0[j], j, S0.shape[0], S_prev, D // LANES,
                      extra_scratch=[pltpu.VMEM((nprob, LANES, LANES), F32)])
    return grp.from_scan(y), (P['rwkv_w_o_bf'], j), shift, S, v_first


def _gdn_layer(grp, x, conv0, S0, S_prev, P, j):
    D = x.shape[-1]
    s, tm = grp.s, grp.wide_tile
    w_in = P['gdn_w_in'][j]
    cw = P['gdn_conv_w'][j]
    nh, cdim = cw.shape[0] - 1, cw.shape[1]
    H = S0.shape[2]
    assert grp.T >= nh
    row = lambda a: a.reshape(1, -1)
    lane_pad = lambda a: jnp.pad(a, ((0, 0), (0, LANES - a.shape[1])))
    zeros = jnp.zeros((1, H), F32)
    consts = [row(P['norm_mix_pre_i']), w_in[:, :cdim].astype(BF16), w_in[:, cdim:cdim + D].astype(BF16),
              lane_pad(w_in[:, cdim + D:]).astype(BF16), cw,
              lane_pad(jnp.concatenate([zeros, row(P['gdn_a_log'][j])], axis=1)),
              lane_pad(jnp.concatenate([zeros, row(P['gdn_dt_bias'][j])], axis=1))]
    q, k, v, gate, bg, conv_new = _row_call(
        functools.partial(_gdn_pre_kernel, s, H), f"gdn_pre_{j}", tm, [x], [conv0], consts,
        [(D, BF16)] * 4 + [LANES], [(nh * s, cdim)], [pltpu.VMEM((_hist_pad(nh * s) + tm, cdim), F32)])
    tiles = [grp.to_scan(a) for a in (q, k, v, gate)]
    NB, RB, _ = tiles[0].shape
    rows = min(SCAN_ROWS, RB)
    nt = RB // rows
    bg = grp.to_scan(bg[:, :, :2 * H]).reshape(NB * RB, 2 * H).T.reshape(2 * H, NB * nt, rows // LANES, LANES)
    hg = SCAN_HEADS
    g_spec = pl.BlockSpec((hg, None, rows // LANES, LANES), lambda b, h, t: (H // hg + h, b * nt + t, 0, 0))
    beta_spec = pl.BlockSpec((hg, None, rows // LANES, LANES), lambda b, h, t: (h, b * nt + t, 0, 0))
    vec_spec = pl.BlockSpec((1, LANES), lambda b, h, t: (0, 0))
    y, S = _scan_call(_gdn_scan_kernel, f"gdn_scan_{j}", grp.C, grp.chained, tiles,
                      [bg, bg, row(P['gdn_norm_w'][j])], [g_spec, beta_spec, vec_spec], S0, j, S0.shape[0], S_prev, H)
    return grp.from_scan(y), (P['gdn_w_o_bf'], j), conv_new, S


def _hgrn_layer(grp, x, S0, S_prev, P, i, j):
    D = x.shape[-1]
    tm = grp.wide_tile
    H = S0.shape[2]
    row = lambda a: a.reshape(1, -1)
    consts = [row(P['norm_mix_pre_i']), (P['hgrn_w_in_bf'], j), P['hgrn_lb']]
    q, k, lf, v, gate = _row_call(functools.partial(_hgrn_pre_kernel, i), f"hgrn_pre_{j}", tm, [x], [], consts,
                                  [D, D, D, (D, BF16), (D, BF16)], [], [])
    tiles = [grp.to_scan(a) for a in (q, k, lf, v, gate)]
    nprob = SCAN_HEADS * (min(SCAN_ROWS, tiles[0].shape[1]) // LANES)
    vec_spec = pl.BlockSpec((1, LANES), lambda b, h, t: (0, 0))
    y, S = _scan_call(_gla_scan_kernel, f"hgrn_scan_{j}", grp.C, grp.chained, tiles, [row(P['hgrn_norm_w'][j])],
                      [vec_spec], S0, j, S0.shape[0], S_prev, H,
                      extra_scratch=[pltpu.VMEM((nprob, LANES, LANES), F32),
                                     pltpu.VMEM((nprob, LANES, GLA_SUB * LANES), F32)])
    return grp.from_scan(y), (P['hgrn_w_o_bf'], j), S


def _trunk(grp, x, shift0, wkv0, gconv0, gS0, hS0, fconv0, P):
    D = x.shape[-1]
    depth = P['norm_mix_pre'].shape[0]
    row = lambda a: a.reshape(1, -1)
    x = grp.to_rows(x)
    v_first = None
    shift, gconv, fconv = [], [], []
    wkv = gS = hS = None
    for i in range(depth):
        kind, j = i % 3, i // 3
        P = dict(P, norm_mix_pre_i=P['norm_mix_pre'][i])
        if kind == 0:
            y, w_o, s_shift, wkv, v_first = _rwkv_layer(grp, x, grp.hist_to_rows(shift0[j][:, None]), wkv0, wkv,
                                                       v_first, P, j)
            shift.append(grp.hist_from_rows(s_shift, 1)[:, 0])
        elif kind == 1:
            y, w_o, c_new, gS = _gdn_layer(grp, x, grp.hist_to_rows(gconv0[j]), gS0, gS, P, j)
            gconv.append(grp.hist_from_rows(c_new, gconv0.shape[2]))
        else:
            y, w_o, hS = _hgrn_layer(grp, x, hS0, hS, P, i, j)
        nh = P['ffn_conv_w'].shape[1] - 1
        dff2 = P['ffn_w_up'].shape[2]
        x, c_new = _row_call(
            functools.partial(_ffn_kernel, grp.s), f"ffn_{i}", grp.ffn_tile, [y, x], [grp.hist_to_rows(fconv0[i])],
            [w_o, row(P['norm_mix_post'][i]),
             row(P['norm_ffn_pre'][i]), (P['ffn_w_up_bf'], i), P['ffn_conv_w'][i],
             (P['ffn_w_down_bf'], i), row(P['norm_ffn_post'][i])],
            [D], [(nh * grp.s, dff2)], [pltpu.VMEM((_hist_pad(nh * grp.s) + grp.ffn_tile, dff2), F32)])
        fconv.append(grp.hist_from_rows(c_new, nh))
    y = x.reshape(grp.T, grp.B, D).swapaxes(0, 1) if grp.tm_major else x
    return y, (jnp.stack(shift), wkv, jnp.stack(gconv), gS, hS, jnp.stack(fconv))


def kernel(x_prompt, x_sample, state_rwkv_shift, state_rwkv_wkv, state_gdn_conv, state_gdn_S, state_hgrn_S, state_ffn_conv, norm_mix_pre, norm_mix_post, norm_ffn_pre, norm_ffn_post, rwkv_mix, rwkv_w_rkv, rwkv_w0, rwkv_w1, rwkv_w2, rwkv_a0, rwkv_a1, rwkv_a2, rwkv_v0, rwkv_v1, rwkv_v2, rwkv_g1, rwkv_g2, rwkv_k_k, rwkv_k_a, rwkv_r_k, rwkv_lnx_w, rwkv_lnx_b, rwkv_w_o, gdn_w_in, gdn_conv_w, gdn_a_log, gdn_dt_bias, gdn_norm_w, gdn_w_o, hgrn_w_in, hgrn_lb, hgrn_norm_w, hgrn_w_o, ffn_w_up, ffn_conv_w, ffn_w_down):
    P = dict(norm_mix_pre=norm_mix_pre, norm_mix_post=norm_mix_post, norm_ffn_pre=norm_ffn_pre,
             norm_ffn_post=norm_ffn_post, rwkv_mix=rwkv_mix, rwkv_w_rkv=rwkv_w_rkv, rwkv_w0=rwkv_w0,
             rwkv_w1=rwkv_w1, rwkv_w2=rwkv_w2, rwkv_a0=rwkv_a0, rwkv_a1=rwkv_a1, rwkv_a2=rwkv_a2,
             rwkv_v0=rwkv_v0, rwkv_v1=rwkv_v1, rwkv_v2=rwkv_v2, rwkv_g1=rwkv_g1, rwkv_g2=rwkv_g2,
             rwkv_k_k=rwkv_k_k, rwkv_k_a=rwkv_k_a, rwkv_r_k=rwkv_r_k, rwkv_lnx_w=rwkv_lnx_w,
             rwkv_lnx_b=rwkv_lnx_b, rwkv_w_o=rwkv_w_o, gdn_w_in=gdn_w_in, gdn_conv_w=gdn_conv_w,
             gdn_a_log=gdn_a_log, gdn_dt_bias=gdn_dt_bias, gdn_norm_w=gdn_norm_w, gdn_w_o=gdn_w_o,
             hgrn_w_in=hgrn_w_in, hgrn_lb=hgrn_lb, hgrn_norm_w=hgrn_norm_w, hgrn_w_o=hgrn_w_o,
             ffn_w_up=ffn_w_up, ffn_conv_w=ffn_conv_w, ffn_w_down=ffn_w_down)
    for name in ('rwkv_w_rkv', 'rwkv_w_o', 'gdn_w_o', 'hgrn_w_in', 'hgrn_w_o', 'ffn_w_up', 'ffn_w_down'):
        P[name + '_bf'] = P[name].astype(BF16)
    Bp, Tp, _ = x_prompt.shape
    Bs, Ts, _ = x_sample.shape
    zero_like = lambda st: jnp.zeros((st.shape[0], Bp) + st.shape[2:], st.dtype)
    y_p, (p_shift, p_wkv, p_gconv, p_gS, p_hS, p_fconv) = _trunk(
        _Group(Bp, Tp, False), x_prompt, zero_like(state_rwkv_shift), zero_like(state_rwkv_wkv),
        zero_like(state_gdn_conv), zero_like(state_gdn_S), zero_like(state_hgrn_S), zero_like(state_ffn_conv), P)
    y_s, (s_shift, s_wkv, s_gconv, s_gS, s_hS, s_fconv) = _trunk(
        _Group(Bs, Ts, True), x_sample, state_rwkv_shift, state_rwkv_wkv, state_gdn_conv, state_gdn_S,
        state_hgrn_S, state_ffn_conv, P)
    return (y_p, y_s, p_shift, s_shift, p_wkv, s_wkv, p_gconv, s_gconv,
            p_gS, s_gS, p_hS, s_hS, p_fconv, s_fconv)
```

```python
import functools
import math

import jax
import jax.numpy as jnp
from jax import lax
from jax.experimental import pallas as pl
from jax.experimental.pallas import tpu as pltpu

F32 = jnp.float32
BF16 = jnp.bfloat16

NORM_EPS = 1e-6
L2_EPS = 1e-6
RWKV_LNX_EPS = 64e-5
RWKV_N = 64
LANES = 128
SUBLANES = 8
VMEM_LIMIT = 56 * 1024 * 1024
ROW_TILE = 256
WIDE_TILE = 512
FFN_TILE = 256
SCAN_ROWS = 256
SCAN_ROWS_CHAINED = 512
SCAN_HEADS = 4
SCAN_CHUNK = 64
GLA_SUB = 8


def _dg(a, b, ca, cb):
    return lax.dot_general(a, b, (((ca,), (cb,)), ((), ())), preferred_element_type=F32)


def _mm(a, b):
    return _dg(a.astype(BF16), b.astype(BF16), 1, 0)


def _mm_nt(a, b):
    return _dg(a.astype(BF16), b.astype(BF16), 1, 1)


def _mm_tn(a, b):
    return _dg(a.astype(BF16), b.astype(BF16), 0, 0)


def _split3(x):
    hi = x.astype(BF16)
    r1 = x - hi.astype(F32)
    mid = r1.astype(BF16)
    lo = (r1 - mid.astype(F32)).astype(BF16)
    return hi, mid, lo


def _mm_xl(m, x):
    h, mi, lo = _split3(x)
    m = m.astype(BF16)
    return _dg(m, h, 1, 0) + (_dg(m, mi, 1, 0) + _dg(m, lo, 1, 0))


def _iota2(n, m):
    return (lax.broadcasted_iota(jnp.int32, (n, m), 0), lax.broadcasted_iota(jnp.int32, (n, m), 1))


def _sigmoid(x):
    return 1.0 / (1.0 + jnp.exp(-x))


def _silu(x):
    return x * _sigmoid(x)


def _softplus(x):
    return jnp.maximum(x, 0.0) + jnp.log(1.0 + jnp.exp(-jnp.abs(x)))


def _rmsnorm(x, w):
    return x * lax.rsqrt(jnp.mean(x * x, axis=-1, keepdims=True) + NORM_EPS) * w


def _head_sum(x, e):
    return _mm(x, e)


def _unit_lower_inv(Ls, C):
    n = Ls[0].shape[0]
    ri, ci = _iota2(n, n)
    eye = (ri == ci).astype(F32)
    Xs = [eye - L for L in Ls]
    Ps = list(Ls)
    m = 2
    while m < C:
        Ps = [_mm(P, P) for P in Ps]
        Xs = [X + _mm(X, P) for X, P in zip(Xs, Ps)]
        m *= 2
    return Xs


def _tile_masks(C):
    sh = C.bit_length() - 1
    ri, ci = _iota2(LANES, LANES)
    same = jnp.right_shift(ri, sh) == jnp.right_shift(ci, sh)
    return ri, ci, same, same & (ri >= ci), same & (ri > ci)


def _place_rows(x, r0, n):
    parts = []
    if r0:
        parts.append(jnp.zeros((r0, x.shape[1]), x.dtype))
    parts.append(x)
    if n - r0 - x.shape[0]:
        parts.append(jnp.zeros((n - r0 - x.shape[0], x.shape[1]), x.dtype))
    return jnp.concatenate(parts, axis=0) if len(parts) > 1 else x


def _rwkv_tile(r, lw, k, v, kk, bb, C):
    ri, ci, same, tril, stril = _tile_masks(C)
    lane = lax.broadcasted_iota(jnp.int32, (1, LANES), 1)
    m = [(lane < RWKV_N).astype(F32), (lane >= RWKV_N).astype(F32)]
    trilf = tril.astype(F32)
    n = len(r)
    gc = [_mm_xl(trilf, x) for x in lw]
    at = [-kk[g] * jnp.exp(gc[g] - lw[g]) for g in range(n)]
    rt = [r[g] * jnp.exp(gc[g]) for g in range(n)]
    sc = []
    for g in range(n):
        e_neg = jnp.exp(-gc[g])
        lhs = jnp.concatenate([at[g] * m[0], at[g] * m[1], rt[g] * m[0], rt[g] * m[1]], axis=0)
        sc.append(_mm_nt(lhs, jnp.concatenate([bb[g] * e_neg, k[g] * e_neg], axis=0)))
    gh = [(g, hd) for g in range(n) for hd in range(2)]
    a_blk = [sc[g][LANES * hd:LANES * (hd + 1)] for g, hd in gh]
    r_blk = [sc[g][LANES * (2 + hd):LANES * (3 + hd)] for g, hd in gh]
    A_rb = [jnp.where(tril, x[:, :LANES], 0.0) for x in r_blk]
    A_rk = [jnp.where(tril, x[:, LANES:], 0.0) for x in r_blk]
    Tinv = _unit_lower_inv([jnp.where(stril, -x[:, :LANES], 0.0) for x in a_blk], C)
    akv = [_mm(jnp.where(stril, a_blk[p][:, LANES:], 0.0), v[g]) for p, (g, hd) in enumerate(gh)]
    y = [_mm(Tinv[p], jnp.concatenate([at[g] * m[hd], akv[p]], axis=1)) for p, (g, hd) in enumerate(gh)]
    o0 = [_mm(jnp.concatenate([A_rk[p], A_rb[p]], axis=1), jnp.concatenate([v[g], y[p][:, LANES:]], axis=0))
          for p, (g, hd) in enumerate(gh)]
    out = []
    for g in range(n):
        p0, p1 = 2 * g, 2 * g + 1
        out.append(dict(gc=gc[g], rt=rt[g], m=m, Wa=y[p0][:, :LANES] + y[p1][:, :LANES],
                        U0=y[p0][:, LANES:] * m[0] + y[p1][:, LANES:] * m[1],
                        O0=o0[p0] * m[0] + o0[p1] * m[1],
                        A_rb=jnp.concatenate([A_rb[p0], A_rb[p1]], axis=0), k=k[g], v=v[g], bb=bb[g]))
    return out


def _rwkv_block(i, C, ts, Ss):
    sl = slice(C * i, C * (i + 1))
    sl1 = slice(LANES + C * i, LANES + C * (i + 1))
    r2, c2 = _iota2(LANES, LANES)
    blk = ((r2 >= RWKV_N) == (c2 >= RWKV_N)).astype(F32)
    xs = [_mm_nt(jnp.concatenate([t['Wa'][sl], t['rt'][sl]], axis=0), S) for t, S in zip(ts, Ss)]
    S_new, ax = [], []
    for t, S, x in zip(ts, Ss, xs):
        X = x[:C]
        U = t['U0'][sl] + X
        gl = t['gc'][C * (i + 1) - 1:C * (i + 1), :]
        e_last = jnp.exp(gl - t['gc'][sl])
        S_new.append(S * jnp.exp(gl) + blk * _mm_tn(
            jnp.concatenate([U, t['v'][sl]], axis=0),
            jnp.concatenate([t['bb'][sl] * e_last, t['k'][sl] * e_last], axis=0)))
        ax.append(_mm(jnp.concatenate([t['A_rb'][sl], t['A_rb'][sl1]], axis=0), _place_rows(X, C * i, LANES)))
    o = [t['O0'][sl] + x[C:] + a[:C] * t['m'][0] + a[C:] * t['m'][1] for t, x, a in zip(ts, xs, ax)]
    return o, S_new


def _gdn_tile(q, k, v, g_row, beta_row, C):
    ri, ci, same, tril, stril = _tile_masks(C)
    eye = ri == ci
    lsum = lambda msk, x: jnp.sum(jnp.where(msk, x, 0.0), axis=1, keepdims=True)
    n = len(q)
    pre = []
    for g in range(n):
        g_b = jnp.broadcast_to(g_row[g], (LANES, LANES))
        beta_col = lsum(eye, jnp.broadcast_to(beta_row[g], (LANES, LANES)))
        g_col = lsum(eye, g_b)
        gc_col = lsum(tril, g_b)
        gl_col = lsum(same, g_b)
        gc_row = jnp.sum(jnp.where(same & (ri <= ci), jnp.broadcast_to(g_col, (LANES, LANES)), 0.0),
                         axis=0, keepdims=True)
        decay = jnp.where(tril, jnp.exp(jnp.minimum(gc_col - gc_row, 0.0)), 0.0)
        pre.append(dict(beta=beta_col, gc=gc_col, gl=gl_col, decay=decay, kb=k[g] * beta_col))
    sc = [_mm_nt(jnp.concatenate([pre[g]['kb'], q[g]], axis=0), k[g]) for g in range(n)]
    Tinv = _unit_lower_inv([jnp.where(stril, sc[g][:LANES] * pre[g]['decay'], 0.0) for g in range(n)], C)
    out = []
    for g in range(n):
        p = pre[g]
        e_gc = jnp.exp(p['gc'])
        uw = _mm(Tinv[g], jnp.concatenate([v[g] * p['beta'], p['kb'] * e_gc], axis=1))
        out.append(dict(u=uw[:, :LANES], wk=uw[:, LANES:], qk=sc[g][LANES:] * p['decay'], qg=q[g] * e_gc,
                        kd=k[g] * jnp.exp(p['gl'] - p['gc']), egl=jnp.exp(p['gl'])))
    return out


def _gdn_block(i, C, ts, Ss):
    sl = slice(C * i, C * (i + 1))
    ws = [_mm(jnp.concatenate([t['wk'][sl], t['qg'][sl]], axis=0), S) for t, S in zip(ts, Ss)]
    v_new = [t['u'][sl] - w[:C] for t, w in zip(ts, ws)]
    S_new = [S * t['egl'][C * i:C * i + 1, :] + _mm_tn(t['kd'][sl], vn) for t, S, vn in zip(ts, Ss, v_new)]
    o = [w[C:] + _mm(t['qk'][sl], _place_rows(vn, C * i, LANES)) for t, w, vn in zip(ts, ws, v_new)]
    return o, S_new


def _gla_tile(q_ref, k_ref, lf_ref, v_ref, probs, C, gc_scr, prod_scr):
    n = len(probs)
    nsb = LANES // GLA_SUB
    ri, ci, same, tril, _ = _tile_masks(C)
    sub = GLA_SUB.bit_length() - 1
    rsub, csub = jnp.right_shift(ri, sub), jnp.right_shift(ci, sub)
    trilf = tril.astype(F32)
    get = lambda ref: [_tile(ref, p) for p in probs]
    q, k, v = get(q_ref), get(k_ref), get(v_ref)
    gc = [_mm_xl(trilf, x) for x in get(lf_ref)]
    rk, ck = _iota2(GLA_SUB * LANES, LANES)
    sel = (jnp.right_shift(rk, LANES.bit_length() - 1) == ck).astype(BF16)
    for g, (h, j) in enumerate(probs):
        gc_scr[g] = gc[g]
        for I in range(nsb):
            s0 = GLA_SUB * I
            ksub = k[g][s0:s0 + GLA_SUB]
            gsub = gc[g][s0:s0 + GLA_SUB]
            for ii in range(GLA_SUB):
                i = s0 + ii
                gi = gc_scr[g, i:i + 1, :]
                qi = q_ref[LANES * j + i:LANES * j + i + 1, LANES * h:LANES * (h + 1)]
                prod_scr[g, s0:s0 + GLA_SUB, LANES * ii:LANES * (ii + 1)] = (
                    qi * ksub * jnp.exp(jnp.minimum(gi - gsub, 0.0)))
    d = [_mm(prod_scr[g], sel) for g in range(n)]
    AT = []
    for g in range(n):
        strips = [d[g][:GLA_SUB]] + [pltpu.roll(d[g][GLA_SUB * I:GLA_SUB * (I + 1)], GLA_SUB * I, axis=1)
                                     for I in range(1, nsb)]
        AT.append(jnp.where((rsub == csub) & (ri <= ci), jnp.concatenate(strips, axis=0), 0.0))
    nsub = C // GLA_SUB
    if nsub > 1:
        for g in range(n):
            off = []
            for b in range(LANES // C):
                b0 = C * b
                kblk = k[g][b0:b0 + C]
                gblk = gc[g][b0:b0 + C]
                acc = jnp.zeros((C, LANES), F32)
                for J in range(1, nsub):
                    s0 = b0 + GLA_SUB * J
                    ref = gc_scr[g, s0 - 1:s0, :]
                    khat = kblk * jnp.exp(jnp.minimum(ref - gblk, 0.0))
                    qhat = q[g][s0:s0 + GLA_SUB] * jnp.exp(gc[g][s0:s0 + GLA_SUB] - ref)
                    acc = acc + _mm_nt(khat, _place_rows(qhat, s0, LANES))
                off.append(acc)
            AT[g] = AT[g] + jnp.where(same & (rsub < csub), jnp.concatenate(off, axis=0), 0.0)
    o_intra = [_mm_tn(AT[g], v[g]) for g in range(n)]
    return [dict(gc=gc[g], k=k[g], v=v[g], qg=q[g] * jnp.exp(gc[g]), o_intra=o_intra[g]) for g in range(n)]


def _gla_block(i, C, ts, STs):
    sl = slice(C * i, C * (i + 1))
    o, ST_new = [], []
    for t, ST in zip(ts, STs):
        gl = t['gc'][C * (i + 1) - 1:C * (i + 1), :]
        o.append(t['o_intra'][sl] + _mm_nt(t['qg'][sl], ST))
        ST_new.append(ST * jnp.exp(gl) + _mm_tn(t['v'][sl], t['k'][sl] * jnp.exp(gl - t['gc'][sl])))
    return o, ST_new


def _shifted(ext, hp, s, tm, back):
    return ext[hp - back * s:hp - back * s + tm, :]


def _rwkv_pre_kernel(has_vres, s, *refs):
    if has_vres:
        (x_ref, vf_ref, sh0_ref, nw_ref, mix_ref, wrkv_ref, w0_ref, w1_ref, w2_ref, a0_ref, a1_ref, a2_ref,
         g1_ref, g2_ref, kkw_ref, ka_ref, e_ref, v0_ref, v1_ref, v2_ref,
         r_out, lw_out, k_out, v_out, kk_out, bb_out, g_out, sh_out, ext) = refs
    else:
        (x_ref, sh0_ref, nw_ref, mix_ref, wrkv_ref, w0_ref, w1_ref, w2_ref, a0_ref, a1_ref, a2_ref,
         g1_ref, g2_ref, kkw_ref, ka_ref, e_ref,
         r_out, lw_out, k_out, v_out, kk_out, bb_out, g_out, sh_out, ext) = refs
    t = pl.program_id(1)
    tm, D = x_ref.shape
    hp = ext.shape[0] - tm
    h = _rmsnorm(x_ref[...], nw_ref[...])

    @pl.when(t == 0)
    def _():
        ext[hp - s:hp, :] = sh0_ref[...]

    ext[hp:, :] = h
    d = _shifted(ext, hp, s, tm, 1) - h
    ext[hp - s:hp, :] = h[tm - s:, :]

    @pl.when(t == pl.num_programs(1) - 1)
    def _():
        sh_out[...] = h[tm - s:, :]

    mixed = lambda i: h + d * mix_ref[i:i + 1, :]
    r = _mm(mixed(0), wrkv_ref[0])
    z = w0_ref[...] + _mm(jnp.tanh(_mm(mixed(1), w1_ref[...])), w2_ref[...])
    lw_out[...] = -math.exp(-0.5) * _sigmoid(z)
    k = _mm(mixed(2), wrkv_ref[1])
    xv = mixed(3)
    v = _mm(xv, wrkv_ref[2])
    a = _sigmoid(a0_ref[...] + _mm(_mm(mixed(4), a1_ref[...]), a2_ref[...]))
    if has_vres:
        gate_v = _sigmoid(v0_ref[...] + _mm(_mm(xv, v1_ref[...]), v2_ref[...]))
        v = v + (vf_ref[...] - v) * gate_v
    g_out[...] = _mm(_sigmoid(_mm(mixed(5), g1_ref[...])), g2_ref[...]).astype(g_out.dtype)
    r_out[...] = r.astype(r_out.dtype)
    v_out[...] = v
    kkraw = k * kkw_ref[...]
    e = e_ref[...]
    for c in range(D // LANES):
        sl = slice(c * LANES, (c + 1) * LANES)
        kc = kkraw[:, sl]
        kkn = kc * lax.rsqrt(_head_sum(kc * kc, e) + L2_EPS)
        kk_out[:, sl] = kkn.astype(kk_out.dtype)
        bb_out[:, sl] = (kkn * a[:, sl]).astype(bb_out.dtype)
    k_out[...] = (k * (1.0 + (a - 1.0) * ka_ref[...])).astype(k_out.dtype)


def _gdn_pre_kernel(s, nheads, x_ref, c0_ref, nw_ref, wqkv_ref, wgate_ref, wbg_ref, cw_ref, alog_ref, dtb_ref,
                    q_out, k_out, v_out, gate_out, bg_out, cnew_out, ext):
    t = pl.program_id(1)
    tm, D = x_ref.shape
    hp = ext.shape[0] - tm
    nh = cw_ref.shape[0] - 1
    h = _rmsnorm(x_ref[...], nw_ref[...])

    @pl.when(t == 0)
    def _():
        ext[hp - nh * s:hp, :] = c0_ref[...]

    ext[hp:, :] = _mm(h, wqkv_ref[...])
    y = _shifted(ext, hp, s, tm, 0) * cw_ref[nh:nh + 1, :]
    for j in range(nh):
        y = y + _shifted(ext, hp, s, tm, nh - j) * cw_ref[j:j + 1, :]
    hist = ext[hp + tm - nh * s:hp + tm, :]

    @pl.when(t == pl.num_programs(1) - 1)
    def _():
        cnew_out[...] = hist

    ext[hp - nh * s:hp, :] = hist
    y = _silu(y)
    qk_w = D
    for c in range(qk_w // LANES):
        sl = slice(c * LANES, (c + 1) * LANES)
        qc = y[:, sl]
        q_out[:, sl] = (qc * (lax.rsqrt(jnp.sum(qc * qc, axis=-1, keepdims=True) + L2_EPS) * LANES ** -0.5)
                        ).astype(q_out.dtype)
        sl2 = slice(qk_w + c * LANES, qk_w + (c + 1) * LANES)
        kc = y[:, sl2]
        k_out[:, sl] = (kc * lax.rsqrt(jnp.sum(kc * kc, axis=-1, keepdims=True) + L2_EPS)).astype(k_out.dtype)
    v_out[...] = y[:, 2 * qk_w:].astype(v_out.dtype)
    gate_out[...] = _mm(h, wgate_ref[...]).astype(gate_out.dtype)
    rest = _mm(h, wbg_ref[...])
    lane = lax.broadcasted_iota(jnp.int32, rest.shape, 1)
    bg_out[...] = jnp.where(lane < nheads, _sigmoid(rest),
                            -jnp.exp(alog_ref[...]) * _softplus(rest + dtb_ref[...]))


def _hgrn_pre_kernel(layer, x_ref, nw_ref, win_ref, lb_ref, q_out, k_out, lf_out, v_out, gate_out):
    D = x_ref.shape[1]
    h = _rmsnorm(x_ref[...], nw_ref[...])
    lbp = lb_ref[...]
    ex = jnp.exp(lbp - jnp.max(lbp, axis=0, keepdims=True))
    soft = ex / jnp.sum(ex, axis=0, keepdims=True)
    row = lax.broadcasted_iota(jnp.int32, soft.shape, 0)
    lb = jnp.sum(jnp.where((row >= 1) & (row <= layer), soft, 0.0), axis=0, keepdims=True)
    p = _mm(h, win_ref[...])
    q_out[...] = _silu(p[:, :D])
    f = lb + (1.0 - lb) * _sigmoid(p[:, D:2 * D])
    k_out[...] = 1.0 - f
    lf_out[...] = jnp.log(f)
    v_out[...] = p[:, 2 * D:3 * D].astype(v_out.dtype)
    gate_out[...] = p[:, 3 * D:].astype(gate_out.dtype)


def _ffn_kernel(s, y_ref, x_ref, c0_ref, wo_ref, mpw_ref, nw_ref, wup_ref, cw_ref, wdn_ref, pw_ref, x_out, cnew_out,
                ext):
    t = pl.program_id(1)
    tm, D = x_ref.shape
    hp = ext.shape[0] - tm
    nh = cw_ref.shape[0] - 1
    dff = wdn_ref.shape[0]
    x = x_ref[...] + _rmsnorm(_mm(y_ref[...], wo_ref[...]), mpw_ref[...])
    h = _rmsnorm(x, nw_ref[...])

    @pl.when(t == 0)
    def _():
        ext[hp - nh * s:hp, :] = c0_ref[...]

    ext[hp:, :] = _mm(h, wup_ref[...])
    y = _shifted(ext, hp, s, tm, 0) * cw_ref[nh:nh + 1, :]
    for j in range(nh):
        y = y + _shifted(ext, hp, s, tm, nh - j) * cw_ref[j:j + 1, :]
    hist = ext[hp + tm - nh * s:hp + tm, :]

    @pl.when(t == pl.num_programs(1) - 1)
    def _():
        cnew_out[...] = hist

    ext[hp - nh * s:hp, :] = hist
    act = _silu(y[:, dff:]) * y[:, :dff]
    x_out[...] = x + _rmsnorm(_mm(act, wdn_ref[...]), pw_ref[...])


def _const_spec(a):
    if isinstance(a, tuple):
        a, i = a
        nd = a.ndim - 1
        return pl.BlockSpec((None,) + a.shape[1:], lambda g, t: (i,) + (0,) * nd, pipeline_mode=pl.Buffered(1))
    nd = a.ndim
    return pl.BlockSpec(a.shape, lambda g, t: (0,) * nd, pipeline_mode=pl.Buffered(1))


def _row_call(body, name, tm, tiled_ins, group_ins, const_ins, tiled_out_widths, group_out_shapes, scratch):
    G, R, _ = tiled_ins[0].shape
    assert R % tm == 0
    tile_spec = lambda c: pl.BlockSpec((None, tm, c), lambda g, t: (g, t, 0))
    group_spec = lambda n, c: pl.BlockSpec((None, n, c), lambda g, t: (g, 0, 0))
    group_in_spec = lambda n, c: pl.BlockSpec((None, n, c), lambda g, t: (g, 0, 0), pipeline_mode=pl.Buffered(1))
    in_specs = ([tile_spec(a.shape[2]) for a in tiled_ins] + [group_in_spec(*a.shape[1:]) for a in group_ins]
                + [_const_spec(a) for a in const_ins])
    tiled_outs = [c if isinstance(c, tuple) else (c, F32) for c in tiled_out_widths]
    out_specs = [tile_spec(c) for c, _ in tiled_outs] + [group_spec(n, c) for n, c in group_out_shapes]
    out_shape = ([jax.ShapeDtypeStruct((G, R, c), dt) for c, dt in tiled_outs]
                 + [jax.ShapeDtypeStruct((G, n, c), F32) for n, c in group_out_shapes])
    return pl.pallas_call(
        body, name=name, grid=(G, R // tm), in_specs=in_specs, out_specs=out_specs, out_shape=out_shape,
        scratch_shapes=scratch,
        compiler_params=pltpu.CompilerParams(dimension_semantics=("parallel", "arbitrary"),
                                             vmem_limit_bytes=VMEM_LIMIT),
    )(*tiled_ins, *group_ins, *[a[0] if isinstance(a, tuple) else a for a in const_ins])


def _hist_pad(n):
    return -(-n // SUBLANES) * SUBLANES


def _tile(ref, p):
    h, j = p
    return ref[LANES * j:LANES * (j + 1), LANES * h:LANES * (h + 1)].astype(F32)


def _scan_states(chained, C, probs, tiles, scr, load, store, block_fn):
    nb = LANES // C
    tt = pl.program_id(2)
    heads = sorted({h for h, _ in probs})
    nt = len(probs) // len(heads)
    outs = {}
    if chained:
        @pl.when(tt == 0)
        def _():
            for h in heads:
                scr[h] = load((h,))

        Ss = [scr[h] for h in heads]
        for j in range(nt):
            sel = [tiles[probs.index((h, j))] for h in heads]
            for i in range(nb):
                os, Ss = block_fn(i, sel, Ss)
                for h, o in zip(heads, os):
                    outs[(h, j, i)] = o
        for h, S in zip(heads, Ss):
            scr[h] = S

        @pl.when(tt == pl.num_programs(2) - 1)
        def _():
            for h, S in zip(heads, Ss):
                store((h,), S)
    else:
        for i in range(nb):
            Ss = [load((j * nb + i, h)) for h, j in probs]
            os, Ss = block_fn(i, tiles, Ss)
            for (h, j), o, S in zip(probs, os, Ss):
                outs[(h, j, i)] = o
                store((j * nb + i, h), S)
    return [jnp.concatenate([outs[(h, j, i)] for i in range(nb)], axis=0) if nb > 1 else outs[(h, j, 0)]
            for h, j in probs]


def _scan_probs(ref):
    return [(h, j) for h in range(ref.shape[1] // LANES) for j in range(ref.shape[0] // LANES)]


def _rwkv_scan_kernel(C, chained, r_ref, lw_ref, k_ref, v_ref, kk_ref, bb_ref, g_ref, rk_ref, lnw_ref, lnb_ref,
                      e_ref, s0_ref, y_out, s_out, scr, bd):
    N = RWKV_N
    e = e_ref[...]
    probs = _scan_probs(r_ref)
    nt = r_ref.shape[0] // LANES
    nb = LANES // C
    get = lambda ref: [_tile(ref, p) for p in probs]
    r, k, v = get(r_ref), get(k_ref), get(v_ref)
    tiles = _rwkv_tile(r, get(lw_ref), k, v, get(kk_ref), get(bb_ref), C)

    bd[...] = jnp.zeros(bd.shape, F32)
    slot = lambda idx: idx[-1] * nt + (idx[0] // nb if len(idx) > 1 else 0)
    heads = lambda idx: (idx[:-1] + (2 * idx[-1],), idx[:-1] + (2 * idx[-1] + 1,))

    def load(idx):
        p = slot(idx)
        h0, h1 = heads(idx)
        bd[p, :N, :N] = s0_ref[h0]
        bd[p, N:, N:] = s0_ref[h1]
        return bd[p]

    def store(idx, S):
        p = slot(idx)
        h0, h1 = heads(idx)
        bd[p] = S
        s_out[h0] = bd[p, :N, :N]
        s_out[h1] = bd[p, N:, N:]

    O = _scan_states(chained, C, probs, tiles, scr, load, store, lambda i, ts, Ss: _rwkv_block(i, C, ts, Ss))
    np_ = len(probs)
    lanes = [slice(LANES * h, LANES * (h + 1)) for h, _ in probs]
    s1 = _head_sum(jnp.concatenate(O + [r[n] * k[n] * rk_ref[:, lanes[n]] for n in range(np_)], axis=0), e)
    d = [O[n] - s1[LANES * n:LANES * (n + 1)] * (1.0 / N) for n in range(np_)]
    s2 = _head_sum(jnp.concatenate([x * x for x in d], axis=0), e)
    for n, (h, j) in enumerate(probs):
        var = s2[LANES * n:LANES * (n + 1)] * (1.0 / N)
        on = d[n] * lax.rsqrt(var + RWKV_LNX_EPS) * lnw_ref[:, lanes[n]] + lnb_ref[:, lanes[n]]
        bonus = s1[LANES * (np_ + n):LANES * (np_ + n + 1)] * v[n]
        y_out[LANES * j:LANES * (j + 1), lanes[n]] = ((on + bonus) * _tile(g_ref, (h, j))).astype(y_out.dtype)


def _gdn_scan_kernel(C, chained, q_ref, k_ref, v_ref, gate_ref, g_ref, beta_ref, nw_ref, s0_ref, y_out, s_out,
                     scr):
    probs = _scan_probs(q_ref)
    get = lambda ref: [_tile(ref, p) for p in probs]
    tiles = _gdn_tile(get(q_ref), get(k_ref), get(v_ref), [g_ref[h, j:j + 1, :] for h, j in probs],
                      [beta_ref[h, j:j + 1, :] for h, j in probs], C)

    def store(idx, S):
        s_out[idx] = S

    O = _scan_states(chained, C, probs, tiles, scr, lambda idx: s0_ref[idx], store,
                     lambda i, ts, Ss: _gdn_block(i, C, ts, Ss))
    for n, (h, j) in enumerate(probs):
        y_out[LANES * j:LANES * (j + 1), LANES * h:LANES * (h + 1)] = (
            _rmsnorm(O[n], nw_ref[...]) * _silu(_tile(gate_ref, (h, j)))).astype(y_out.dtype)


def _gla_scan_kernel(C, chained, q_ref, k_ref, lf_ref, v_ref, gate_ref, nw_ref, s0_ref, y_out, s_out, scr, gc_scr,
                     prod_scr):
    probs = _scan_probs(q_ref)
    tiles = _gla_tile(q_ref, k_ref, lf_ref, v_ref, probs, C, gc_scr, prod_scr)

    def store(idx, ST):
        s_out[idx] = ST.T

    O = _scan_states(chained, C, probs, tiles, scr, lambda idx: s0_ref[idx].T, store,
                     lambda i, ts, Ss: _gla_block(i, C, ts, Ss))
    for n, (h, j) in enumerate(probs):
        y_out[LANES * j:LANES * (j + 1), LANES * h:LANES * (h + 1)] = (
            _rmsnorm(O[n], nw_ref[...]) * _silu(_tile(gate_ref, (h, j)))).astype(y_out.dtype)


def _scan_rows(chained, RB):
    return min(SCAN_ROWS_CHAINED if chained else SCAN_ROWS, RB)


def _scan_call(body, name, C, chained, tiles, extra_ins, extra_specs, s0, layer, nlayers, s_prev, nheads,
               extra_scratch=()):
    NB, RB, D = tiles[0].shape
    rows = _scan_rows(chained, RB)
    hg = SCAN_HEADS
    assert RB % rows == 0 and rows % LANES == 0 and nheads % hg == 0
    tile_spec = pl.BlockSpec((None, rows, hg * LANES), lambda b, h, t: (b, t, h))
    s_shape = s0.shape[-4:]
    hs = hg * (s_shape[1] // nheads)
    nt = RB // rows
    lead = (None,) if chained else (rows // C,)
    first = (lambda b, t: b) if chained else (lambda b, t: b * nt + t)

    def s_spec(l):
        if l is None:
            return pl.BlockSpec(lead + (hs,) + s_shape[2:], lambda b, h, t: (first(b, t), h, 0, 0))
        return pl.BlockSpec((None,) + lead + (hs,) + s_shape[2:], lambda b, h, t: (l, first(b, t), h, 0, 0))

    s0_layer = None if s0.ndim == 4 else layer
    n_in = len(tiles) + len(extra_ins) + 1
    kern = functools.partial(body, C, chained)
    ins = [*tiles, *extra_ins, s0]
    in_specs = [tile_spec] * len(tiles) + list(extra_specs) + [s_spec(s0_layer)]
    aliases = {}
    if s_prev is not None:
        ins.append(s_prev)
        in_specs.append(pl.BlockSpec(memory_space=pl.ANY))
        aliases = {n_in: 1}
        kern = lambda *refs: body(C, chained, *refs[:n_in], *refs[n_in + 1:])
    return pl.pallas_call(
        kern, name=name, grid=(NB, nheads // hg, RB // rows),
        in_specs=in_specs, out_specs=[tile_spec, s_spec(layer)],
        out_shape=[jax.ShapeDtypeStruct((NB, RB, D), BF16),
                   jax.ShapeDtypeStruct((nlayers,) + s_shape, F32)],
        scratch_shapes=[pltpu.VMEM((hg, LANES, LANES), F32)] + list(extra_scratch),
        input_output_aliases=aliases,
        compiler_params=pltpu.CompilerParams(dimension_semantics=("parallel", "parallel", "arbitrary"),
                                             vmem_limit_bytes=VMEM_LIMIT),
    )(*ins)


class _Group:
    def __init__(self, B, T, time_major):
        self.B, self.T, self.tm_major = B, T, time_major
        if time_major:
            self.G, self.R, self.s = 1, B * T, B
            self.C = -(-T // SUBLANES) * SUBLANES
            assert LANES % self.C == 0 and (B * self.C) % LANES == 0
        else:
            self.G, self.R, self.s = B, T, 1
            self.C = SCAN_CHUNK
            assert T % LANES == 0
        self.chained = not time_major
        self.tile = min(ROW_TILE, self.R)
        self.wide_tile = min(WIDE_TILE, self.R)
        self.ffn_tile = self.s if time_major else min(FFN_TILE, self.R)

    def to_rows(self, x):
        if self.tm_major:
            return jnp.swapaxes(x, 0, 1).reshape(1, self.R, x.shape[-1])
        return x

    def hist_to_rows(self, h):
        if self.tm_major:
            return jnp.swapaxes(h, 0, 1).reshape(1, -1, h.shape[-1])
        return h

    def hist_from_rows(self, h, n):
        if self.tm_major:
            return jnp.swapaxes(h.reshape(n, self.B, h.shape[-1]), 0, 1)
        return h

    def to_scan(self, a):
        if not self.tm_major:
            return a
        a = jnp.swapaxes(a.reshape(self.T, self.B, a.shape[-1]), 0, 1)
        a = jnp.pad(a, ((0, 0), (0, self.C - self.T), (0, 0)))
        return a.reshape(1, self.B * self.C, a.shape[-1])

    def from_scan(self, a):
        if not self.tm_major:
            return a
        return self.to_rows(a.reshape(self.B, self.C, a.shape[-1])[:, :self.T])


def _rwkv_layer(grp, x, shift0, S0, S_prev, v_first, P, j):
    D = x.shape[-1]
    s, tm = grp.s, grp.wide_tile
    N = RWKV_N
    has_vres = v_first is not None
    r2, c2 = _iota2(LANES, LANES)
    e = ((r2 // N) == (c2 // N)).astype(BF16)
    row = lambda a: a.reshape(1, -1)
    tiled = [x] + ([v_first] if has_vres else [])
    consts = [row(P['norm_mix_pre_i']), P['rwkv_mix'][j], (P['rwkv_w_rkv_bf'], j),
              row(P['rwkv_w0'][j]), P['rwkv_w1'][j].astype(BF16), P['rwkv_w2'][j].astype(BF16),
              row(P['rwkv_a0'][j]), P['rwkv_a1'][j].astype(BF16), P['rwkv_a2'][j].astype(BF16),
              P['rwkv_g1'][j].astype(BF16), P['rwkv_g2'][j].astype(BF16),
              row(P['rwkv_k_k'][j]), row(P['rwkv_k_a'][j]), e]
    if has_vres:
        consts += [row(P['rwkv_v0'][j - 1]), P['rwkv_v1'][j - 1].astype(BF16), P['rwkv_v2'][j - 1].astype(BF16)]
    r, lw, k, v, kk, bb, g, shift = _row_call(
        functools.partial(_rwkv_pre_kernel, has_vres, s), f"rwkv_pre_{j}", tm, tiled, [shift0], consts,
        [(D, BF16), D, (D, BF16), D, (D, BF16), (D, BF16), (D, BF16)], [(s, D)],
        [pltpu.VMEM((_hist_pad(s) + tm, D), F32)])
    if not has_vres:
        v_first = v
    tiles = [grp.to_scan(a) for a in (r, lw, k, v, kk, bb, g)]
    nprob = SCAN_HEADS * (_scan_rows(grp.chained, tiles[0].shape[1]) // LANES)
    vec_spec = pl.BlockSpec((1, SCAN_HEADS * LANES), lambda b, h, t: (0, h))
    e_spec = pl.BlockSpec((LANES, LANES), lambda b, h, t: (0, 0))
    y, S = _scan_call(_rwkv_scan_kernel, f"rwkv_scan_{j}", grp.C, grp.chained, tiles,
                      [row(P['rwkv_r_k'][j]), row(P['rwkv_lnx_w'][j]), row(P['rwkv_lnx_b'][j]), e],
                      [vec_spec] * 3 + [e_spec], S0[j], j, S0.shape[0], S_prev, D // LANES,
                      extra_scratch=[pltpu.VMEM((nprob, LANES, LANES), F32)])
    return grp.from_scan(y), (P['rwkv_w_o_bf'], j), shift, S, v_first


def _gdn_layer(grp, x, conv0, S0, S_prev, P, j):
    D = x.shape[-1]
    s, tm = grp.s, grp.wide_tile
    w_in = P['gdn_w_in'][j]
    cw = P['gdn_conv_w'][j]
    nh, cdim = cw.shape[0] - 1, cw.shape[1]
    H = S0.shape[2]
    assert grp.T >= nh
    row = lambda a: a.reshape(1, -1)
    lane_pad = lambda a: jnp.pad(a, ((0, 0), (0, LANES - a.shape[1])))
    zeros = jnp.zeros((1, H), F32)
    consts = [row(P['norm_mix_pre_i']), w_in[:, :cdim].astype(BF16), w_in[:, cdim:cdim + D].astype(BF16),
              lane_pad(w_in[:, cdim + D:]).astype(BF16), cw,
              lane_pad(jnp.concatenate([zeros, row(P['gdn_a_log'][j])], axis=1)),
              lane_pad(jnp.concatenate([zeros, row(P['gdn_dt_bias'][j])], axis=1))]
    q, k, v, gate, bg, conv_new = _row_call(
        functools.partial(_gdn_pre_kernel, s, H), f"gdn_pre_{j}", tm, [x], [conv0], consts,
        [(D, BF16)] * 4 + [LANES], [(nh * s, cdim)], [pltpu.VMEM((_hist_pad(nh * s) + tm, cdim), F32)])
    tiles = [grp.to_scan(a) for a in (q, k, v, gate)]
    NB, RB, _ = tiles[0].shape
    rows = _scan_rows(grp.chained, RB)
    nt = RB // rows
    bg = grp.to_scan(bg[:, :, :2 * H]).reshape(NB * RB, 2 * H).T.reshape(2 * H, NB * nt, rows // LANES, LANES)
    hg = SCAN_HEADS
    g_spec = pl.BlockSpec((hg, None, rows // LANES, LANES), lambda b, h, t: (H // hg + h, b * nt + t, 0, 0))
    beta_spec = pl.BlockSpec((hg, None, rows // LANES, LANES), lambda b, h, t: (h, b * nt + t, 0, 0))
    vec_spec = pl.BlockSpec((1, LANES), lambda b, h, t: (0, 0))
    y, S = _scan_call(_gdn_scan_kernel, f"gdn_scan_{j}", grp.C, grp.chained, tiles,
                      [bg, bg, row(P['gdn_norm_w'][j])], [g_spec, beta_spec, vec_spec], S0, j, S0.shape[0], S_prev, H)
    return grp.from_scan(y), (P['gdn_w_o_bf'], j), conv_new, S


def _hgrn_layer(grp, x, S0, S_prev, P, i, j):
    D = x.shape[-1]
    tm = grp.wide_tile
    H = S0.shape[2]
    row = lambda a: a.reshape(1, -1)
    consts = [row(P['norm_mix_pre_i']), (P['hgrn_w_in_bf'], j), P['hgrn_lb']]
    q, k, lf, v, gate = _row_call(functools.partial(_hgrn_pre_kernel, i), f"hgrn_pre_{j}", tm, [x], [], consts,
                                  [D, D, D, (D, BF16), (D, BF16)], [], [])
    tiles = [grp.to_scan(a) for a in (q, k, lf, v, gate)]
    nprob = SCAN_HEADS * (_scan_rows(grp.chained, tiles[0].shape[1]) // LANES)
    vec_spec = pl.BlockSpec((1, LANES), lambda b, h, t: (0, 0))
    y, S = _scan_call(_gla_scan_kernel, f"hgrn_scan_{j}", grp.C, grp.chained, tiles, [row(P['hgrn_norm_w'][j])],
                      [vec_spec], S0, j, S0.shape[0], S_prev, H,
                      extra_scratch=[pltpu.VMEM((nprob, LANES, LANES), F32),
                                     pltpu.VMEM((nprob, LANES, GLA_SUB * LANES), F32)])
    return grp.from_scan(y), (P['hgrn_w_o_bf'], j), S


def _trunk(grp, x, shift0, wkv0, gconv0, gS0, hS0, fconv0, P):
    D = x.shape[-1]
    depth = P['norm_mix_pre'].shape[0]
    row = lambda a: a.reshape(1, -1)
    x = grp.to_rows(x)
    v_first = None
    shift, gconv, fconv = [], [], []
    wkv = gS = hS = None
    for i in range(depth):
        kind, j = i % 3, i // 3
        P = dict(P, norm_mix_pre_i=P['norm_mix_pre'][i])
        if kind == 0:
            y, w_o, s_shift, wkv, v_first = _rwkv_layer(grp, x, grp.hist_to_rows(shift0[j][:, None]), wkv0, wkv,
                                                       v_first, P, j)
            shift.append(grp.hist_from_rows(s_shift, 1)[:, 0])
        elif kind == 1:
            y, w_o, c_new, gS = _gdn_layer(grp, x, grp.hist_to_rows(gconv0[j]), gS0, gS, P, j)
            gconv.append(grp.hist_from_rows(c_new, gconv0.shape[2]))
        else:
            y, w_o, hS = _hgrn_layer(grp, x, hS0, hS, P, i, j)
        nh = P['ffn_conv_w'].shape[1] - 1
        dff2 = P['ffn_w_up'].shape[2]
        x, c_new = _row_call(
            functools.partial(_ffn_kernel, grp.s), f"ffn_{i}", grp.ffn_tile, [y, x], [grp.hist_to_rows(fconv0[i])],
            [w_o, row(P['norm_mix_post'][i]),
             row(P['norm_ffn_pre'][i]), (P['ffn_w_up_bf'], i), P['ffn_conv_w'][i],
             (P['ffn_w_down_bf'], i), row(P['norm_ffn_post'][i])],
            [D], [(nh * grp.s, dff2)], [pltpu.VMEM((_hist_pad(nh * grp.s) + grp.ffn_tile, dff2), F32)])
        fconv.append(grp.hist_from_rows(c_new, nh))
    y = x.reshape(grp.T, grp.B, D).swapaxes(0, 1) if grp.tm_major else x
    return y, (jnp.stack(shift), wkv, jnp.stack(gconv), gS, hS, jnp.stack(fconv))


def kernel(x_prompt, x_sample, state_rwkv_shift, state_rwkv_wkv, state_gdn_conv, state_gdn_S, state_hgrn_S, state_ffn_conv, norm_mix_pre, norm_mix_post, norm_ffn_pre, norm_ffn_post, rwkv_mix, rwkv_w_rkv, rwkv_w0, rwkv_w1, rwkv_w2, rwkv_a0, rwkv_a1, rwkv_a2, rwkv_v0, rwkv_v1, rwkv_v2, rwkv_g1, rwkv_g2, rwkv_k_k, rwkv_k_a, rwkv_r_k, rwkv_lnx_w, rwkv_lnx_b, rwkv_w_o, gdn_w_in, gdn_conv_w, gdn_a_log, gdn_dt_bias, gdn_norm_w, gdn_w_o, hgrn_w_in, hgrn_lb, hgrn_norm_w, hgrn_w_o, ffn_w_up, ffn_conv_w, ffn_w_down):
    P = dict(norm_mix_pre=norm_mix_pre, norm_mix_post=norm_mix_post, norm_ffn_pre=norm_ffn_pre,
             norm_ffn_post=norm_ffn_post, rwkv_mix=rwkv_mix, rwkv_w_rkv=rwkv_w_rkv, rwkv_w0=rwkv_w0,
             rwkv_w1=rwkv_w1, rwkv_w2=rwkv_w2, rwkv_a0=rwkv_a0, rwkv_a1=rwkv_a1, rwkv_a2=rwkv_a2,
             rwkv_v0=rwkv_v0, rwkv_v1=rwkv_v1, rwkv_v2=rwkv_v2, rwkv_g1=rwkv_g1, rwkv_g2=rwkv_g2,
             rwkv_k_k=rwkv_k_k, rwkv_k_a=rwkv_k_a, rwkv_r_k=rwkv_r_k, rwkv_lnx_w=rwkv_lnx_w,
             rwkv_lnx_b=rwkv_lnx_b, rwkv_w_o=rwkv_w_o, gdn_w_in=gdn_w_in, gdn_conv_w=gdn_conv_w,
             gdn_a_log=gdn_a_log, gdn_dt_bias=gdn_dt_bias, gdn_norm_w=gdn_norm_w, gdn_w_o=gdn_w_o,
             hgrn_w_in=hgrn_w_in, hgrn_lb=hgrn_lb, hgrn_norm_w=hgrn_norm_w, hgrn_w_o=hgrn_w_o,
             ffn_w_up=ffn_w_up, ffn_conv_w=ffn_conv_w, ffn_w_down=ffn_w_down)
    for name in ('rwkv_w_rkv', 'rwkv_w_o', 'gdn_w_o', 'hgrn_w_in', 'hgrn_w_o', 'ffn_w_up', 'ffn_w_down'):
        P[name + '_bf'] = P[name].astype(BF16)
    Bp, Tp, _ = x_prompt.shape
    Bs, Ts, _ = x_sample.shape
    zero_like = lambda st: jnp.zeros((st.shape[0], Bp) + st.shape[2:], st.dtype)
    y_p, (p_shift, p_wkv, p_gconv, p_gS, p_hS, p_fconv) = _trunk(
        _Group(Bp, Tp, False), x_prompt, zero_like(state_rwkv_shift), zero_like(state_rwkv_wkv),
        zero_like(state_gdn_conv), zero_like(state_gdn_S), zero_like(state_hgrn_S), zero_like(state_ffn_conv), P)
    y_s, (s_shift, s_wkv, s_gconv, s_gS, s_hS, s_fconv) = _trunk(
        _Group(Bs, Ts, True), x_sample, state_rwkv_shift, state_rwkv_wkv, state_gdn_conv, state_gdn_S,
        state_hgrn_S, state_ffn_conv, P)
    return (y_p, y_s, p_shift, s_shift, p_wkv, s_wkv, p_gconv, s_gconv,
            p_gS, s_gS, p_hS, s_hS, p_fconv, s_fconv)
```

```python
import functools
import math

import jax
import jax.numpy as jnp
from jax import lax
from jax.experimental import pallas as pl
from jax.experimental.pallas import tpu as pltpu

F32 = jnp.float32
BF16 = jnp.bfloat16

NORM_EPS = 1e-6
L2_EPS = 1e-6
RWKV_LNX_EPS = 64e-5
RWKV_N = 64
LANES = 128
SUBLANES = 8
VMEM_LIMIT = 56 * 1024 * 1024
ROW_TILE = 256
WIDE_TILE = 512
FFN_TILE = 256
SCAN_ROWS = 256
SCAN_ROWS_CHAINED = 1024
SCAN_HEADS = 4
SCAN_CHUNK = 64
GLA_SUB = 8


def _dg(a, b, ca, cb):
    return lax.dot_general(a, b, (((ca,), (cb,)), ((), ())), preferred_element_type=F32)


def _mm(a, b):
    return _dg(a.astype(BF16), b.astype(BF16), 1, 0)


def _mm_nt(a, b):
    return _dg(a.astype(BF16), b.astype(BF16), 1, 1)


def _mm_tn(a, b):
    return _dg(a.astype(BF16), b.astype(BF16), 0, 0)


def _split3(x):
    hi = x.astype(BF16)
    r1 = x - hi.astype(F32)
    mid = r1.astype(BF16)
    lo = (r1 - mid.astype(F32)).astype(BF16)
    return hi, mid, lo


def _mm_xl(m, x):
    h, mi, lo = _split3(x)
    m = m.astype(BF16)
    return _dg(m, h, 1, 0) + (_dg(m, mi, 1, 0) + _dg(m, lo, 1, 0))


def _iota2(n, m):
    return (lax.broadcasted_iota(jnp.int32, (n, m), 0), lax.broadcasted_iota(jnp.int32, (n, m), 1))


def _sigmoid(x):
    return 1.0 / (1.0 + jnp.exp(-x))


def _silu(x):
    return x * _sigmoid(x)


def _softplus(x):
    return jnp.maximum(x, 0.0) + jnp.log(1.0 + jnp.exp(-jnp.abs(x)))


def _rmsnorm(x, w):
    return x * lax.rsqrt(jnp.mean(x * x, axis=-1, keepdims=True) + NORM_EPS) * w


def _head_sum(x, e):
    return _mm(x, e)


def _unit_lower_inv(Ls, C):
    n = Ls[0].shape[0]
    ri, ci = _iota2(n, n)
    eye = (ri == ci).astype(F32)
    Xs = [eye - L for L in Ls]
    Ps = list(Ls)
    m = 2
    while m < C:
        Ps = [_mm(P, P) for P in Ps]
        Xs = [X + _mm(X, P) for X, P in zip(Xs, Ps)]
        m *= 2
    return Xs


def _tile_masks(C):
    sh = C.bit_length() - 1
    ri, ci = _iota2(LANES, LANES)
    same = jnp.right_shift(ri, sh) == jnp.right_shift(ci, sh)
    return ri, ci, same, same & (ri >= ci), same & (ri > ci)


def _place_rows(x, r0, n):
    parts = []
    if r0:
        parts.append(jnp.zeros((r0, x.shape[1]), x.dtype))
    parts.append(x)
    if n - r0 - x.shape[0]:
        parts.append(jnp.zeros((n - r0 - x.shape[0], x.shape[1]), x.dtype))
    return jnp.concatenate(parts, axis=0) if len(parts) > 1 else x


def _rwkv_tile(r, lw, k, v, kk, bb, C):
    ri, ci, same, tril, stril = _tile_masks(C)
    lane = lax.broadcasted_iota(jnp.int32, (1, LANES), 1)
    m = [(lane < RWKV_N).astype(F32), (lane >= RWKV_N).astype(F32)]
    trilf = tril.astype(F32)
    n = len(r)
    gc = [_mm_xl(trilf, x) for x in lw]
    at = [-kk[g] * jnp.exp(gc[g] - lw[g]) for g in range(n)]
    rt = [r[g] * jnp.exp(gc[g]) for g in range(n)]
    sc = []
    for g in range(n):
        e_neg = jnp.exp(-gc[g])
        lhs = jnp.concatenate([at[g] * m[0], at[g] * m[1], rt[g] * m[0], rt[g] * m[1]], axis=0)
        sc.append(_mm_nt(lhs, jnp.concatenate([bb[g] * e_neg, k[g] * e_neg], axis=0)))
    gh = [(g, hd) for g in range(n) for hd in range(2)]
    a_blk = [sc[g][LANES * hd:LANES * (hd + 1)] for g, hd in gh]
    r_blk = [sc[g][LANES * (2 + hd):LANES * (3 + hd)] for g, hd in gh]
    A_rb = [jnp.where(tril, x[:, :LANES], 0.0) for x in r_blk]
    A_rk = [jnp.where(tril, x[:, LANES:], 0.0) for x in r_blk]
    Tinv = _unit_lower_inv([jnp.where(stril, -x[:, :LANES], 0.0) for x in a_blk], C)
    akv = [_mm(jnp.where(stril, a_blk[p][:, LANES:], 0.0), v[g]) for p, (g, hd) in enumerate(gh)]
    y = [_mm(Tinv[p], jnp.concatenate([at[g] * m[hd], akv[p]], axis=1)) for p, (g, hd) in enumerate(gh)]
    o0 = [_mm(jnp.concatenate([A_rk[p], A_rb[p]], axis=1), jnp.concatenate([v[g], y[p][:, LANES:]], axis=0))
          for p, (g, hd) in enumerate(gh)]
    out = []
    for g in range(n):
        p0, p1 = 2 * g, 2 * g + 1
        out.append(dict(gc=gc[g], rt=rt[g], m=m, Wa=y[p0][:, :LANES] + y[p1][:, :LANES],
                        U0=y[p0][:, LANES:] * m[0] + y[p1][:, LANES:] * m[1],
                        O0=o0[p0] * m[0] + o0[p1] * m[1],
                        A_rb=jnp.concatenate([A_rb[p0], A_rb[p1]], axis=0), k=k[g], v=v[g], bb=bb[g]))
    return out


def _rwkv_block(i, C, ts, Ss):
    sl = slice(C * i, C * (i + 1))
    sl1 = slice(LANES + C * i, LANES + C * (i + 1))
    r2, c2 = _iota2(LANES, LANES)
    blk = ((r2 >= RWKV_N) == (c2 >= RWKV_N)).astype(F32)
    xs = [_mm_nt(jnp.concatenate([t['Wa'][sl], t['rt'][sl]], axis=0), S) for t, S in zip(ts, Ss)]
    S_new, ax = [], []
    for t, S, x in zip(ts, Ss, xs):
        X = x[:C]
        U = t['U0'][sl] + X
        gl = t['gc'][C * (i + 1) - 1:C * (i + 1), :]
        e_last = jnp.exp(gl - t['gc'][sl])
        S_new.append(S * jnp.exp(gl) + blk * _mm_tn(
            jnp.concatenate([U, t['v'][sl]], axis=0),
            jnp.concatenate([t['bb'][sl] * e_last, t['k'][sl] * e_last], axis=0)))
        ax.append(_mm(jnp.concatenate([t['A_rb'][sl], t['A_rb'][sl1]], axis=0), _place_rows(X, C * i, LANES)))
    o = [t['O0'][sl] + x[C:] + a[:C] * t['m'][0] + a[C:] * t['m'][1] for t, x, a in zip(ts, xs, ax)]
    return o, S_new


def _gdn_tile(q, k, v, g_row, beta_row, C):
    ri, ci, same, tril, stril = _tile_masks(C)
    eye = ri == ci
    lsum = lambda msk, x: jnp.sum(jnp.where(msk, x, 0.0), axis=1, keepdims=True)
    n = len(q)
    pre = []
    for g in range(n):
        g_b = jnp.broadcast_to(g_row[g], (LANES, LANES))
        beta_col = lsum(eye, jnp.broadcast_to(beta_row[g], (LANES, LANES)))
        g_col = lsum(eye, g_b)
        gc_col = lsum(tril, g_b)
        gl_col = lsum(same, g_b)
        gc_row = jnp.sum(jnp.where(same & (ri <= ci), jnp.broadcast_to(g_col, (LANES, LANES)), 0.0),
                         axis=0, keepdims=True)
        decay = jnp.where(tril, jnp.exp(jnp.minimum(gc_col - gc_row, 0.0)), 0.0)
        pre.append(dict(beta=beta_col, gc=gc_col, gl=gl_col, decay=decay, kb=k[g] * beta_col))
    sc = [_mm_nt(jnp.concatenate([pre[g]['kb'], q[g]], axis=0), k[g]) for g in range(n)]
    Tinv = _unit_lower_inv([jnp.where(stril, sc[g][:LANES] * pre[g]['decay'], 0.0) for g in range(n)], C)
    out = []
    for g in range(n):
        p = pre[g]
        e_gc = jnp.exp(p['gc'])
        uw = _mm(Tinv[g], jnp.concatenate([v[g] * p['beta'], p['kb'] * e_gc], axis=1))
        out.append(dict(u=uw[:, :LANES], wk=uw[:, LANES:], qk=sc[g][LANES:] * p['decay'], qg=q[g] * e_gc,
                        kd=k[g] * jnp.exp(p['gl'] - p['gc']), egl=jnp.exp(p['gl'])))
    return out


def _gdn_block(i, C, ts, Ss):
    sl = slice(C * i, C * (i + 1))
    ws = [_mm(jnp.concatenate([t['wk'][sl], t['qg'][sl]], axis=0), S) for t, S in zip(ts, Ss)]
    v_new = [t['u'][sl] - w[:C] for t, w in zip(ts, ws)]
    S_new = [S * t['egl'][C * i:C * i + 1, :] + _mm_tn(t['kd'][sl], vn) for t, S, vn in zip(ts, Ss, v_new)]
    o = [w[C:] + _mm(t['qk'][sl], _place_rows(vn, C * i, LANES)) for t, w, vn in zip(ts, ws, v_new)]
    return o, S_new


def _gla_tile(q_ref, k_ref, lf_ref, v_ref, probs, C, gc_scr, prod_scr):
    n = len(probs)
    nsb = LANES // GLA_SUB
    ri, ci, same, tril, _ = _tile_masks(C)
    sub = GLA_SUB.bit_length() - 1
    rsub, csub = jnp.right_shift(ri, sub), jnp.right_shift(ci, sub)
    trilf = tril.astype(F32)
    get = lambda ref: [_tile(ref, p) for p in probs]
    q, k, v = get(q_ref), get(k_ref), get(v_ref)
    gc = [_mm_xl(trilf, x) for x in get(lf_ref)]
    rk, ck = _iota2(GLA_SUB * LANES, LANES)
    sel = (jnp.right_shift(rk, LANES.bit_length() - 1) == ck).astype(BF16)
    for g, (h, j) in enumerate(probs):
        gc_scr[g] = gc[g]
        for I in range(nsb):
            s0 = GLA_SUB * I
            ksub = k[g][s0:s0 + GLA_SUB]
            gsub = gc[g][s0:s0 + GLA_SUB]
            for ii in range(GLA_SUB):
                i = s0 + ii
                gi = gc_scr[g, i:i + 1, :]
                qi = q_ref[LANES * j + i:LANES * j + i + 1, LANES * h:LANES * (h + 1)]
                prod_scr[g, s0:s0 + GLA_SUB, LANES * ii:LANES * (ii + 1)] = (
                    qi * ksub * jnp.exp(jnp.minimum(gi - gsub, 0.0)))
    d = [_mm(prod_scr[g], sel) for g in range(n)]
    AT = []
    for g in range(n):
        strips = [d[g][:GLA_SUB]] + [pltpu.roll(d[g][GLA_SUB * I:GLA_SUB * (I + 1)], GLA_SUB * I, axis=1)
                                     for I in range(1, nsb)]
        AT.append(jnp.where((rsub == csub) & (ri <= ci), jnp.concatenate(strips, axis=0), 0.0))
    nsub = C // GLA_SUB
    if nsub > 1:
        for g in range(n):
            off = []
            for b in range(LANES // C):
                b0 = C * b
                kblk = k[g][b0:b0 + C]
                gblk = gc[g][b0:b0 + C]
                acc = jnp.zeros((C, LANES), F32)
                for J in range(1, nsub):
                    s0 = b0 + GLA_SUB * J
                    ref = gc_scr[g, s0 - 1:s0, :]
                    khat = kblk * jnp.exp(jnp.minimum(ref - gblk, 0.0))
                    qhat = q[g][s0:s0 + GLA_SUB] * jnp.exp(gc[g][s0:s0 + GLA_SUB] - ref)
                    acc = acc + _mm_nt(khat, _place_rows(qhat, s0, LANES))
                off.append(acc)
            AT[g] = AT[g] + jnp.where(same & (rsub < csub), jnp.concatenate(off, axis=0), 0.0)
    o_intra = [_mm_tn(AT[g], v[g]) for g in range(n)]
    return [dict(gc=gc[g], k=k[g], v=v[g], qg=q[g] * jnp.exp(gc[g]), o_intra=o_intra[g]) for g in range(n)]


def _gla_block(i, C, ts, STs):
    sl = slice(C * i, C * (i + 1))
    o, ST_new = [], []
    for t, ST in zip(ts, STs):
        gl = t['gc'][C * (i + 1) - 1:C * (i + 1), :]
        o.append(t['o_intra'][sl] + _mm_nt(t['qg'][sl], ST))
        ST_new.append(ST * jnp.exp(gl) + _mm_tn(t['v'][sl], t['k'][sl] * jnp.exp(gl - t['gc'][sl])))
    return o, ST_new


def _shifted(ext, hp, s, tm, back):
    return ext[hp - back * s:hp - back * s + tm, :]


def _rwkv_pre_kernel(has_vres, s, *refs):
    if has_vres:
        (x_ref, vf_ref, sh0_ref, nw_ref, mix_ref, wrkv_ref, w0_ref, w1_ref, w2_ref, a0_ref, a1_ref, a2_ref,
         g1_ref, g2_ref, kkw_ref, ka_ref, e_ref, v0_ref, v1_ref, v2_ref,
         r_out, lw_out, k_out, v_out, kk_out, bb_out, g_out, sh_out, ext) = refs
    else:
        (x_ref, sh0_ref, nw_ref, mix_ref, wrkv_ref, w0_ref, w1_ref, w2_ref, a0_ref, a1_ref, a2_ref,
         g1_ref, g2_ref, kkw_ref, ka_ref, e_ref,
         r_out, lw_out, k_out, v_out, kk_out, bb_out, g_out, sh_out, ext) = refs
    t = pl.program_id(1)
    tm, D = x_ref.shape
    hp = ext.shape[0] - tm
    h = _rmsnorm(x_ref[...], nw_ref[...])

    @pl.when(t == 0)
    def _():
        ext[hp - s:hp, :] = sh0_ref[...]

    ext[hp:, :] = h
    d = _shifted(ext, hp, s, tm, 1) - h
    ext[hp - s:hp, :] = h[tm - s:, :]

    @pl.when(t == pl.num_programs(1) - 1)
    def _():
        sh_out[...] = h[tm - s:, :]

    mixed = lambda i: h + d * mix_ref[i:i + 1, :]
    r = _mm(mixed(0), wrkv_ref[0])
    z = w0_ref[...] + _mm(jnp.tanh(_mm(mixed(1), w1_ref[...])), w2_ref[...])
    lw_out[...] = -math.exp(-0.5) * _sigmoid(z)
    k = _mm(mixed(2), wrkv_ref[1])
    xv = mixed(3)
    v = _mm(xv, wrkv_ref[2])
    a = _sigmoid(a0_ref[...] + _mm(_mm(mixed(4), a1_ref[...]), a2_ref[...]))
    if has_vres:
        gate_v = _sigmoid(v0_ref[...] + _mm(_mm(xv, v1_ref[...]), v2_ref[...]))
        v = v + (vf_ref[...] - v) * gate_v
    g_out[...] = _mm(_sigmoid(_mm(mixed(5), g1_ref[...])), g2_ref[...]).astype(g_out.dtype)
    r_out[...] = r.astype(r_out.dtype)
    v_out[...] = v
    kkraw = k * kkw_ref[...]
    e = e_ref[...]
    for c in range(D // LANES):
        sl = slice(c * LANES, (c + 1) * LANES)
        kc = kkraw[:, sl]
        kkn = kc * lax.rsqrt(_head_sum(kc * kc, e) + L2_EPS)
        kk_out[:, sl] = kkn.astype(kk_out.dtype)
        bb_out[:, sl] = (kkn * a[:, sl]).astype(bb_out.dtype)
    k_out[...] = (k * (1.0 + (a - 1.0) * ka_ref[...])).astype(k_out.dtype)


def _gdn_pre_kernel(s, nheads, x_ref, c0_ref, nw_ref, wqkv_ref, wgate_ref, wbg_ref, cw_ref, alog_ref, dtb_ref,
                    q_out, k_out, v_out, gate_out, bg_out, cnew_out, ext):
    t = pl.program_id(1)
    tm, D = x_ref.shape
    hp = ext.shape[0] - tm
    nh = cw_ref.shape[0] - 1
    h = _rmsnorm(x_ref[...], nw_ref[...])

    @pl.when(t == 0)
    def _():
        ext[hp - nh * s:hp, :] = c0_ref[...]

    ext[hp:, :] = _mm(h, wqkv_ref[...])
    y = _shifted(ext, hp, s, tm, 0) * cw_ref[nh:nh + 1, :]
    for j in range(nh):
        y = y + _shifted(ext, hp, s, tm, nh - j) * cw_ref[j:j + 1, :]
    hist = ext[hp + tm - nh * s:hp + tm, :]

    @pl.when(t == pl.num_programs(1) - 1)
    def _():
        cnew_out[...] = hist

    ext[hp - nh * s:hp, :] = hist
    y = _silu(y)
    qk_w = D
    for c in range(qk_w // LANES):
        sl = slice(c * LANES, (c + 1) * LANES)
        qc = y[:, sl]
        q_out[:, sl] = (qc * (lax.rsqrt(jnp.sum(qc * qc, axis=-1, keepdims=True) + L2_EPS) * LANES ** -0.5)
                        ).astype(q_out.dtype)
        sl2 = slice(qk_w + c * LANES, qk_w + (c + 1) * LANES)
        kc = y[:, sl2]
        k_out[:, sl] = (kc * lax.rsqrt(jnp.sum(kc * kc, axis=-1, keepdims=True) + L2_EPS)).astype(k_out.dtype)
    v_out[...] = y[:, 2 * qk_w:].astype(v_out.dtype)
    gate_out[...] = _mm(h, wgate_ref[...]).astype(gate_out.dtype)
    rest = _mm(h, wbg_ref[...])
    lane = lax.broadcasted_iota(jnp.int32, rest.shape, 1)
    bg_out[...] = jnp.where(lane < nheads, _sigmoid(rest),
                            -jnp.exp(alog_ref[...]) * _softplus(rest + dtb_ref[...]))


def _hgrn_pre_kernel(layer, x_ref, nw_ref, win_ref, lb_ref, q_out, k_out, lf_out, v_out, gate_out):
    D = x_ref.shape[1]
    h = _rmsnorm(x_ref[...], nw_ref[...])
    lbp = lb_ref[...]
    ex = jnp.exp(lbp - jnp.max(lbp, axis=0, keepdims=True))
    soft = ex / jnp.sum(ex, axis=0, keepdims=True)
    row = lax.broadcasted_iota(jnp.int32, soft.shape, 0)
    lb = jnp.sum(jnp.where((row >= 1) & (row <= layer), soft, 0.0), axis=0, keepdims=True)
    p = _mm(h, win_ref[...])
    q_out[...] = _silu(p[:, :D])
    f = lb + (1.0 - lb) * _sigmoid(p[:, D:2 * D])
    k_out[...] = 1.0 - f
    lf_out[...] = jnp.log(f)
    v_out[...] = p[:, 2 * D:3 * D].astype(v_out.dtype)
    gate_out[...] = p[:, 3 * D:].astype(gate_out.dtype)


def _ffn_kernel(s, y_ref, x_ref, c0_ref, wo_ref, mpw_ref, nw_ref, wup_ref, cw_ref, wdn_ref, pw_ref, x_out, cnew_out,
                ext):
    t = pl.program_id(1)
    tm, D = x_ref.shape
    hp = ext.shape[0] - tm
    nh = cw_ref.shape[0] - 1
    dff = wdn_ref.shape[0]
    x = x_ref[...] + _rmsnorm(_mm(y_ref[...], wo_ref[...]), mpw_ref[...])
    h = _rmsnorm(x, nw_ref[...])

    @pl.when(t == 0)
    def _():
        ext[hp - nh * s:hp, :] = c0_ref[...]

    ext[hp:, :] = _mm(h, wup_ref[...])
    y = _shifted(ext, hp, s, tm, 0) * cw_ref[nh:nh + 1, :]
    for j in range(nh):
        y = y + _shifted(ext, hp, s, tm, nh - j) * cw_ref[j:j + 1, :]
    hist = ext[hp + tm - nh * s:hp + tm, :]

    @pl.when(t == pl.num_programs(1) - 1)
    def _():
        cnew_out[...] = hist

    ext[hp - nh * s:hp, :] = hist
    act = _silu(y[:, dff:]) * y[:, :dff]
    x_out[...] = x + _rmsnorm(_mm(act, wdn_ref[...]), pw_ref[...])


def _const_spec(a):
    if isinstance(a, tuple):
        a, i = a
        nd = a.ndim - 1
        return pl.BlockSpec((None,) + a.shape[1:], lambda g, t: (i,) + (0,) * nd, pipeline_mode=pl.Buffered(1))
    nd = a.ndim
    return pl.BlockSpec(a.shape, lambda g, t: (0,) * nd, pipeline_mode=pl.Buffered(1))


def _row_call(body, name, tm, tiled_ins, group_ins, const_ins, tiled_out_widths, group_out_shapes, scratch):
    G, R, _ = tiled_ins[0].shape
    assert R % tm == 0
    tile_spec = lambda c: pl.BlockSpec((None, tm, c), lambda g, t: (g, t, 0))
    group_spec = lambda n, c: pl.BlockSpec((None, n, c), lambda g, t: (g, 0, 0))
    group_in_spec = lambda n, c: pl.BlockSpec((None, n, c), lambda g, t: (g, 0, 0), pipeline_mode=pl.Buffered(1))
    in_specs = ([tile_spec(a.shape[2]) for a in tiled_ins] + [group_in_spec(*a.shape[1:]) for a in group_ins]
                + [_const_spec(a) for a in const_ins])
    tiled_outs = [c if isinstance(c, tuple) else (c, F32) for c in tiled_out_widths]
    out_specs = [tile_spec(c) for c, _ in tiled_outs] + [group_spec(n, c) for n, c in group_out_shapes]
    out_shape = ([jax.ShapeDtypeStruct((G, R, c), dt) for c, dt in tiled_outs]
                 + [jax.ShapeDtypeStruct((G, n, c), F32) for n, c in group_out_shapes])
    return pl.pallas_call(
        body, name=name, grid=(G, R // tm), in_specs=in_specs, out_specs=out_specs, out_shape=out_shape,
        scratch_shapes=scratch,
        compiler_params=pltpu.CompilerParams(dimension_semantics=("parallel", "arbitrary"),
                                             vmem_limit_bytes=VMEM_LIMIT),
    )(*tiled_ins, *group_ins, *[a[0] if isinstance(a, tuple) else a for a in const_ins])


def _hist_pad(n):
    return -(-n // SUBLANES) * SUBLANES


def _tile(ref, p):
    h, j = p
    return ref[LANES * j:LANES * (j + 1), LANES * h:LANES * (h + 1)].astype(F32)


def _scan_states(chained, C, probs, tiles, scr, load, store, block_fn):
    nb = LANES // C
    tt = pl.program_id(2)
    heads = sorted({h for h, _ in probs})
    nt = len(probs) // len(heads)
    outs = {}
    if chained:
        @pl.when(tt == 0)
        def _():
            for h in heads:
                scr[h] = load((h,))

        Ss = [scr[h] for h in heads]
        for j in range(nt):
            sel = [tiles[probs.index((h, j))] for h in heads]
            for i in range(nb):
                os, Ss = block_fn(i, sel, Ss)
                for h, o in zip(heads, os):
                    outs[(h, j, i)] = o
        for h, S in zip(heads, Ss):
            scr[h] = S

        @pl.when(tt == pl.num_programs(2) - 1)
        def _():
            for h, S in zip(heads, Ss):
                store((h,), S)
    else:
        for i in range(nb):
            Ss = [load((j * nb + i, h)) for h, j in probs]
            os, Ss = block_fn(i, tiles, Ss)
            for (h, j), o, S in zip(probs, os, Ss):
                outs[(h, j, i)] = o
                store((j * nb + i, h), S)
    return [jnp.concatenate([outs[(h, j, i)] for i in range(nb)], axis=0) if nb > 1 else outs[(h, j, 0)]
            for h, j in probs]


def _scan_probs(ref):
    return [(h, j) for h in range(ref.shape[1] // LANES) for j in range(ref.shape[0] // LANES)]


def _rwkv_scan_kernel(C, chained, r_ref, lw_ref, k_ref, v_ref, kk_ref, bb_ref, g_ref, rk_ref, lnw_ref, lnb_ref,
                      e_ref, s0_ref, y_out, s_out, scr, bd):
    N = RWKV_N
    e = e_ref[...]
    probs = _scan_probs(r_ref)
    nt = r_ref.shape[0] // LANES
    nb = LANES // C
    get = lambda ref: [_tile(ref, p) for p in probs]
    r, k, v = get(r_ref), get(k_ref), get(v_ref)
    tiles = _rwkv_tile(r, get(lw_ref), k, v, get(kk_ref), get(bb_ref), C)

    bd[...] = jnp.zeros(bd.shape, F32)
    slot = lambda idx: idx[-1] * nt + (idx[0] // nb if len(idx) > 1 else 0)
    heads = lambda idx: (idx[:-1] + (2 * idx[-1],), idx[:-1] + (2 * idx[-1] + 1,))

    def load(idx):
        p = slot(idx)
        h0, h1 = heads(idx)
        bd[p, :N, :N] = s0_ref[h0]
        bd[p, N:, N:] = s0_ref[h1]
        return bd[p]

    def store(idx, S):
        p = slot(idx)
        h0, h1 = heads(idx)
        bd[p] = S
        s_out[h0] = bd[p, :N, :N]
        s_out[h1] = bd[p, N:, N:]

    O = _scan_states(chained, C, probs, tiles, scr, load, store, lambda i, ts, Ss: _rwkv_block(i, C, ts, Ss))
    np_ = len(probs)
    lanes = [slice(LANES * h, LANES * (h + 1)) for h, _ in probs]
    s1 = _head_sum(jnp.concatenate(O + [r[n] * k[n] * rk_ref[:, lanes[n]] for n in range(np_)], axis=0), e)
    d = [O[n] - s1[LANES * n:LANES * (n + 1)] * (1.0 / N) for n in range(np_)]
    s2 = _head_sum(jnp.concatenate([x * x for x in d], axis=0), e)
    for n, (h, j) in enumerate(probs):
        var = s2[LANES * n:LANES * (n + 1)] * (1.0 / N)
        on = d[n] * lax.rsqrt(var + RWKV_LNX_EPS) * lnw_ref[:, lanes[n]] + lnb_ref[:, lanes[n]]
        bonus = s1[LANES * (np_ + n):LANES * (np_ + n + 1)] * v[n]
        y_out[LANES * j:LANES * (j + 1), lanes[n]] = ((on + bonus) * _tile(g_ref, (h, j))).astype(y_out.dtype)


def _gdn_scan_kernel(C, chained, q_ref, k_ref, v_ref, gate_ref, g_ref, beta_ref, nw_ref, s0_ref, y_out, s_out,
                     scr):
    probs = _scan_probs(q_ref)
    get = lambda ref: [_tile(ref, p) for p in probs]
    tiles = _gdn_tile(get(q_ref), get(k_ref), get(v_ref), [g_ref[h, j:j + 1, :] for h, j in probs],
                      [beta_ref[h, j:j + 1, :] for h, j in probs], C)

    def store(idx, S):
        s_out[idx] = S

    O = _scan_states(chained, C, probs, tiles, scr, lambda idx: s0_ref[idx], store,
                     lambda i, ts, Ss: _gdn_block(i, C, ts, Ss))
    for n, (h, j) in enumerate(probs):
        y_out[LANES * j:LANES * (j + 1), LANES * h:LANES * (h + 1)] = (
            _rmsnorm(O[n], nw_ref[...]) * _silu(_tile(gate_ref, (h, j)))).astype(y_out.dtype)


def _gla_scan_kernel(C, chained, q_ref, k_ref, lf_ref, v_ref, gate_ref, nw_ref, s0_ref, y_out, s_out, scr, gc_scr,
                     prod_scr):
    probs = _scan_probs(q_ref)
    tiles = _gla_tile(q_ref, k_ref, lf_ref, v_ref, probs, C, gc_scr, prod_scr)

    def store(idx, ST):
        s_out[idx] = ST.T

    O = _scan_states(chained, C, probs, tiles, scr, lambda idx: s0_ref[idx].T, store,
                     lambda i, ts, Ss: _gla_block(i, C, ts, Ss))
    for n, (h, j) in enumerate(probs):
        y_out[LANES * j:LANES * (j + 1), LANES * h:LANES * (h + 1)] = (
            _rmsnorm(O[n], nw_ref[...]) * _silu(_tile(gate_ref, (h, j)))).astype(y_out.dtype)


def _scan_rows(chained, RB):
    return min(SCAN_ROWS_CHAINED if chained else SCAN_ROWS, RB)


def _scan_call(body, name, C, chained, tiles, extra_ins, extra_specs, s0, layer, nlayers, s_prev, nheads,
               extra_scratch=()):
    NB, RB, D = tiles[0].shape
    rows = _scan_rows(chained, RB)
    hg = SCAN_HEADS
    assert RB % rows == 0 and rows % LANES == 0 and nheads % hg == 0
    tile_spec = pl.BlockSpec((None, rows, hg * LANES), lambda b, h, t: (b, t, h))
    s_shape = s0.shape[-4:]
    hs = hg * (s_shape[1] // nheads)
    nt = RB // rows
    lead = (None,) if chained else (rows // C,)
    first = (lambda b, t: b) if chained else (lambda b, t: b * nt + t)

    def s_spec(l):
        if l is None:
            return pl.BlockSpec(lead + (hs,) + s_shape[2:], lambda b, h, t: (first(b, t), h, 0, 0))
        return pl.BlockSpec((None,) + lead + (hs,) + s_shape[2:], lambda b, h, t: (l, first(b, t), h, 0, 0))

    s0_layer = None if s0.ndim == 4 else layer
    n_in = len(tiles) + len(extra_ins) + 1
    kern = functools.partial(body, C, chained)
    ins = [*tiles, *extra_ins, s0]
    in_specs = [tile_spec] * len(tiles) + list(extra_specs) + [s_spec(s0_layer)]
    aliases = {}
    if s_prev is not None:
        ins.append(s_prev)
        in_specs.append(pl.BlockSpec(memory_space=pl.ANY))
        aliases = {n_in: 1}
        kern = lambda *refs: body(C, chained, *refs[:n_in], *refs[n_in + 1:])
    return pl.pallas_call(
        kern, name=name, grid=(NB, nheads // hg, RB // rows),
        in_specs=in_specs, out_specs=[tile_spec, s_spec(layer)],
        out_shape=[jax.ShapeDtypeStruct((NB, RB, D), BF16),
                   jax.ShapeDtypeStruct((nlayers,) + s_shape, F32)],
        scratch_shapes=[pltpu.VMEM((hg, LANES, LANES), F32)] + list(extra_scratch),
        input_output_aliases=aliases,
        compiler_params=pltpu.CompilerParams(dimension_semantics=("parallel", "parallel", "arbitrary"),
                                             vmem_limit_bytes=VMEM_LIMIT),
    )(*ins)


class _Group:
    def __init__(self, B, T, time_major):
        self.B, self.T, self.tm_major = B, T, time_major
        if time_major:
            self.G, self.R, self.s = 1, B * T, B
            self.C = -(-T // SUBLANES) * SUBLANES
            assert LANES % self.C == 0 and (B * self.C) % LANES == 0
        else:
            self.G, self.R, self.s = B, T, 1
            self.C = SCAN_CHUNK
            assert T % LANES == 0
        self.chained = not time_major
        self.tile = min(ROW_TILE, self.R)
        self.wide_tile = min(WIDE_TILE, self.R)
        self.ffn_tile = self.s if time_major else min(FFN_TILE, self.R)

    def to_rows(self, x):
        if self.tm_major:
            return jnp.swapaxes(x, 0, 1).reshape(1, self.R, x.shape[-1])
        return x

    def hist_to_rows(self, h):
        if self.tm_major:
            return jnp.swapaxes(h, 0, 1).reshape(1, -1, h.shape[-1])
        return h

    def hist_from_rows(self, h, n):
        if self.tm_major:
            return jnp.swapaxes(h.reshape(n, self.B, h.shape[-1]), 0, 1)
        return h

    def to_scan(self, a):
        if not self.tm_major:
            return a
        a = jnp.swapaxes(a.reshape(self.T, self.B, a.shape[-1]), 0, 1)
        a = jnp.pad(a, ((0, 0), (0, self.C - self.T), (0, 0)))
        return a.reshape(1, self.B * self.C, a.shape[-1])

    def from_scan(self, a):
        if not self.tm_major:
            return a
        return self.to_rows(a.reshape(self.B, self.C, a.shape[-1])[:, :self.T])


def _rwkv_layer(grp, x, shift0, S0, S_prev, v_first, P, j):
    D = x.shape[-1]
    s, tm = grp.s, grp.wide_tile
    N = RWKV_N
    has_vres = v_first is not None
    r2, c2 = _iota2(LANES, LANES)
    e = ((r2 // N) == (c2 // N)).astype(BF16)
    row = lambda a: a.reshape(1, -1)
    tiled = [x] + ([v_first] if has_vres else [])
    consts = [row(P['norm_mix_pre_i']), P['rwkv_mix'][j], (P['rwkv_w_rkv_bf'], j),
              row(P['rwkv_w0'][j]), P['rwkv_w1'][j].astype(BF16), P['rwkv_w2'][j].astype(BF16),
              row(P['rwkv_a0'][j]), P['rwkv_a1'][j].astype(BF16), P['rwkv_a2'][j].astype(BF16),
              P['rwkv_g1'][j].astype(BF16), P['rwkv_g2'][j].astype(BF16),
              row(P['rwkv_k_k'][j]), row(P['rwkv_k_a'][j]), e]
    if has_vres:
        consts += [row(P['rwkv_v0'][j - 1]), P['rwkv_v1'][j - 1].astype(BF16), P['rwkv_v2'][j - 1].astype(BF16)]
    r, lw, k, v, kk, bb, g, shift = _row_call(
        functools.partial(_rwkv_pre_kernel, has_vres, s), f"rwkv_pre_{j}", tm, tiled, [shift0], consts,
        [(D, BF16), D, (D, BF16), D, (D, BF16), (D, BF16), (D, BF16)], [(s, D)],
        [pltpu.VMEM((_hist_pad(s) + tm, D), F32)])
    if not has_vres:
        v_first = v
    tiles = [grp.to_scan(a) for a in (r, lw, k, v, kk, bb, g)]
    nprob = SCAN_HEADS * (_scan_rows(grp.chained, tiles[0].shape[1]) // LANES)
    vec_spec = pl.BlockSpec((1, SCAN_HEADS * LANES), lambda b, h, t: (0, h))
    e_spec = pl.BlockSpec((LANES, LANES), lambda b, h, t: (0, 0))
    y, S = _scan_call(_rwkv_scan_kernel, f"rwkv_scan_{j}", grp.C, grp.chained, tiles,
                      [row(P['rwkv_r_k'][j]), row(P['rwkv_lnx_w'][j]), row(P['rwkv_lnx_b'][j]), e],
                      [vec_spec] * 3 + [e_spec], S0[j], j, S0.shape[0], S_prev, D // LANES,
                      extra_scratch=[pltpu.VMEM((nprob, LANES, LANES), F32)])
    return grp.from_scan(y), (P['rwkv_w_o_bf'], j), shift, S, v_first


def _gdn_layer(grp, x, conv0, S0, S_prev, P, j):
    D = x.shape[-1]
    s, tm = grp.s, grp.wide_tile
    w_in = P['gdn_w_in'][j]
    cw = P['gdn_conv_w'][j]
    nh, cdim = cw.shape[0] - 1, cw.shape[1]
    H = S0.shape[2]
    assert grp.T >= nh
    row = lambda a: a.reshape(1, -1)
    lane_pad = lambda a: jnp.pad(a, ((0, 0), (0, LANES - a.shape[1])))
    zeros = jnp.zeros((1, H), F32)
    consts = [row(P['norm_mix_pre_i']), w_in[:, :cdim].astype(BF16), w_in[:, cdim:cdim + D].astype(BF16),
              lane_pad(w_in[:, cdim + D:]).astype(BF16), cw,
              lane_pad(jnp.concatenate([zeros, row(P['gdn_a_log'][j])], axis=1)),
              lane_pad(jnp.concatenate([zeros, row(P['gdn_dt_bias'][j])], axis=1))]
    q, k, v, gate, bg, conv_new = _row_call(
        functools.partial(_gdn_pre_kernel, s, H), f"gdn_pre_{j}", tm, [x], [conv0], consts,
        [(D, BF16)] * 4 + [LANES], [(nh * s, cdim)], [pltpu.VMEM((_hist_pad(nh * s) + tm, cdim), F32)])
    tiles = [grp.to_scan(a) for a in (q, k, v, gate)]
    NB, RB, _ = tiles[0].shape
    rows = _scan_rows(grp.chained, RB)
    nt = RB // rows
    bg = grp.to_scan(bg[:, :, :2 * H]).reshape(NB * RB, 2 * H).T.reshape(2 * H, NB * nt, rows // LANES, LANES)
    hg = SCAN_HEADS
    g_spec = pl.BlockSpec((hg, None, rows // LANES, LANES), lambda b, h, t: (H // hg + h, b * nt + t, 0, 0))
    beta_spec = pl.BlockSpec((hg, None, rows // LANES, LANES), lambda b, h, t: (h, b * nt + t, 0, 0))
    vec_spec = pl.BlockSpec((1, LANES), lambda b, h, t: (0, 0))
    y, S = _scan_call(_gdn_scan_kernel, f"gdn_scan_{j}", grp.C, grp.chained, tiles,
                      [bg, bg, row(P['gdn_norm_w'][j])], [g_spec, beta_spec, vec_spec], S0, j, S0.shape[0], S_prev, H)
    return grp.from_scan(y), (P['gdn_w_o_bf'], j), conv_new, S


def _hgrn_layer(grp, x, S0, S_prev, P, i, j):
    D = x.shape[-1]
    tm = grp.wide_tile
    H = S0.shape[2]
    row = lambda a: a.reshape(1, -1)
    consts = [row(P['norm_mix_pre_i']), (P['hgrn_w_in_bf'], j), P['hgrn_lb']]
    q, k, lf, v, gate = _row_call(functools.partial(_hgrn_pre_kernel, i), f"hgrn_pre_{j}", tm, [x], [], consts,
                                  [D, D, D, (D, BF16), (D, BF16)], [], [])
    tiles = [grp.to_scan(a) for a in (q, k, lf, v, gate)]
    nprob = SCAN_HEADS * (_scan_rows(grp.chained, tiles[0].shape[1]) // LANES)
    vec_spec = pl.BlockSpec((1, LANES), lambda b, h, t: (0, 0))
    y, S = _scan_call(_gla_scan_kernel, f"hgrn_scan_{j}", grp.C, grp.chained, tiles, [row(P['hgrn_norm_w'][j])],
                      [vec_spec], S0, j, S0.shape[0], S_prev, H,
                      extra_scratch=[pltpu.VMEM((nprob, LANES, LANES), F32),
                                     pltpu.VMEM((nprob, LANES, GLA_SUB * LANES), F32)])
    return grp.from_scan(y), (P['hgrn_w_o_bf'], j), S


def _trunk(grp, x, shift0, wkv0, gconv0, gS0, hS0, fconv0, P):
    D = x.shape[-1]
    depth = P['norm_mix_pre'].shape[0]
    row = lambda a: a.reshape(1, -1)
    x = grp.to_rows(x)
    v_first = None
    shift, gconv, fconv = [], [], []
    wkv = gS = hS = None
    for i in range(depth):
        kind, j = i % 3, i // 3
        P = dict(P, norm_mix_pre_i=P['norm_mix_pre'][i])
        if kind == 0:
            y, w_o, s_shift, wkv, v_first = _rwkv_layer(grp, x, grp.hist_to_rows(shift0[j][:, None]), wkv0, wkv,
                                                       v_first, P, j)
            shift.append(grp.hist_from_rows(s_shift, 1)[:, 0])
        elif kind == 1:
            y, w_o, c_new, gS = _gdn_layer(grp, x, grp.hist_to_rows(gconv0[j]), gS0, gS, P, j)
            gconv.append(grp.hist_from_rows(c_new, gconv0.shape[2]))
        else:
            y, w_o, hS = _hgrn_layer(grp, x, hS0, hS, P, i, j)
        nh = P['ffn_conv_w'].shape[1] - 1
        dff2 = P['ffn_w_up'].shape[2]
        x, c_new = _row_call(
            functools.partial(_ffn_kernel, grp.s), f"ffn_{i}", grp.ffn_tile, [y, x], [grp.hist_to_rows(fconv0[i])],
            [w_o, row(P['norm_mix_post'][i]),
             row(P['norm_ffn_pre'][i]), (P['ffn_w_up_bf'], i), P['ffn_conv_w'][i],
             (P['ffn_w_down_bf'], i), row(P['norm_ffn_post'][i])],
            [D], [(nh * grp.s, dff2)], [pltpu.VMEM((_hist_pad(nh * grp.s) + grp.ffn_tile, dff2), F32)])
        fconv.append(grp.hist_from_rows(c_new, nh))
    y = x.reshape(grp.T, grp.B, D).swapaxes(0, 1) if grp.tm_major else x
    return y, (jnp.stack(shift), wkv, jnp.stack(gconv), gS, hS, jnp.stack(fconv))


def kernel(x_prompt, x_sample, state_rwkv_shift, state_rwkv_wkv, state_gdn_conv, state_gdn_S, state_hgrn_S, state_ffn_conv, norm_mix_pre, norm_mix_post, norm_ffn_pre, norm_ffn_post, rwkv_mix, rwkv_w_rkv, rwkv_w0, rwkv_w1, rwkv_w2, rwkv_a0, rwkv_a1, rwkv_a2, rwkv_v0, rwkv_v1, rwkv_v2, rwkv_g1, rwkv_g2, rwkv_k_k, rwkv_k_a, rwkv_r_k, rwkv_lnx_w, rwkv_lnx_b, rwkv_w_o, gdn_w_in, gdn_conv_w, gdn_a_log, gdn_dt_bias, gdn_norm_w, gdn_w_o, hgrn_w_in, hgrn_lb, hgrn_norm_w, hgrn_w_o, ffn_w_up, ffn_conv_w, ffn_w_down):
    P = dict(norm_mix_pre=norm_mix_pre, norm_mix_post=norm_mix_post, norm_ffn_pre=norm_ffn_pre,
             norm_ffn_post=norm_ffn_post, rwkv_mix=rwkv_mix, rwkv_w_rkv=rwkv_w_rkv, rwkv_w0=rwkv_w0,
             rwkv_w1=rwkv_w1, rwkv_w2=rwkv_w2, rwkv_a0=rwkv_a0, rwkv_a1=rwkv_a1, rwkv_a2=rwkv_a2,
             rwkv_v0=rwkv_v0, rwkv_v1=rwkv_v1, rwkv_v2=rwkv_v2, rwkv_g1=rwkv_g1, rwkv_g2=rwkv_g2,
             rwkv_k_k=rwkv_k_k, rwkv_k_a=rwkv_k_a, rwkv_r_k=rwkv_r_k, rwkv_lnx_w=rwkv_lnx_w,
             rwkv_lnx_b=rwkv_lnx_b, rwkv_w_o=rwkv_w_o, gdn_w_in=gdn_w_in, gdn_conv_w=gdn_conv_w,
             gdn_a_log=gdn_a_log, gdn_dt_bias=gdn_dt_bias, gdn_norm_w=gdn_norm_w, gdn_w_o=gdn_w_o,
             hgrn_w_in=hgrn_w_in, hgrn_lb=hgrn_lb, hgrn_norm_w=hgrn_norm_w, hgrn_w_o=hgrn_w_o,
             ffn_w_up=ffn_w_up, ffn_conv_w=ffn_conv_w, ffn_w_down=ffn_w_down)
    for name in ('rwkv_w_rkv', 'rwkv_w_o', 'gdn_w_o', 'hgrn_w_in', 'hgrn_w_o', 'ffn_w_up', 'ffn_w_down'):
        P[name + '_bf'] = P[name].astype(BF16)
    Bp, Tp, _ = x_prompt.shape
    Bs, Ts, _ = x_sample.shape
    zero_like = lambda st: jnp.zeros((st.shape[0], Bp) + st.shape[2:], st.dtype)
    y_p, (p_shift, p_wkv, p_gconv, p_gS, p_hS, p_fconv) = _trunk(
        _Group(Bp, Tp, False), x_prompt, zero_like(state_rwkv_shift), zero_like(state_rwkv_wkv),
        zero_like(state_gdn_conv), zero_like(state_gdn_S), zero_like(state_hgrn_S), zero_like(state_ffn_conv), P)
    y_s, (s_shift, s_wkv, s_gconv, s_gS, s_hS, s_fconv) = _trunk(
        _Group(Bs, Ts, True), x_sample, state_rwkv_shift, state_rwkv_wkv, state_gdn_conv, state_gdn_S,
        state_hgrn_S, state_ffn_conv, P)
    return (y_p, y_s, p_shift, s_shift, p_wkv, s_wkv, p_gconv, s_gconv,
            p_gS, s_gS, p_hS, s_hS, p_fconv, s_fconv)
```

```python
import functools
import math

import jax
import jax.numpy as jnp
from jax import lax
from jax.experimental import pallas as pl
from jax.experimental.pallas import tpu as pltpu

F32 = jnp.float32
BF16 = jnp.bfloat16

NORM_EPS = 1e-6
L2_EPS = 1e-6
RWKV_LNX_EPS = 64e-5
RWKV_N = 64
LANES = 128
SUBLANES = 8
VMEM_LIMIT = 56 * 1024 * 1024
ROW_TILE = 256
WIDE_TILE = 512
FFN_TILE = 256
SCAN_ROWS = 256
SCAN_ROWS_CHAINED = 1024
SCAN_HEADS = 4
SCAN_CHUNK = 64
GLA_SUB = 8


def _dg(a, b, ca, cb):
    return lax.dot_general(a, b, (((ca,), (cb,)), ((), ())), preferred_element_type=F32)


def _mm(a, b):
    return _dg(a.astype(BF16), b.astype(BF16), 1, 0)


def _mm_nt(a, b):
    return _dg(a.astype(BF16), b.astype(BF16), 1, 1)


def _mm_tn(a, b):
    return _dg(a.astype(BF16), b.astype(BF16), 0, 0)


def _split3(x):
    hi = x.astype(BF16)
    r1 = x - hi.astype(F32)
    mid = r1.astype(BF16)
    lo = (r1 - mid.astype(F32)).astype(BF16)
    return hi, mid, lo


def _mm_xl(m, x):
    h, mi, lo = _split3(x)
    m = m.astype(BF16)
    return _dg(m, h, 1, 0) + (_dg(m, mi, 1, 0) + _dg(m, lo, 1, 0))


def _iota2(n, m):
    return (lax.broadcasted_iota(jnp.int32, (n, m), 0), lax.broadcasted_iota(jnp.int32, (n, m), 1))


def _sigmoid(x):
    return 1.0 / (1.0 + jnp.exp(-x))


def _silu(x):
    return x * _sigmoid(x)


def _softplus(x):
    return jnp.maximum(x, 0.0) + jnp.log(1.0 + jnp.exp(-jnp.abs(x)))


def _rmsnorm(x, w):
    return x * lax.rsqrt(jnp.mean(x * x, axis=-1, keepdims=True) + NORM_EPS) * w


def _head_sum(x, e):
    return _mm(x, e)


def _unit_lower_inv(Ls, C):
    n = Ls[0].shape[0]
    ri, ci = _iota2(n, n)
    eye = (ri == ci).astype(F32)
    Xs = [eye - L for L in Ls]
    Ps = list(Ls)
    m = 2
    while m < C:
        Ps = [_mm(P, P) for P in Ps]
        Xs = [X + _mm(X, P) for X, P in zip(Xs, Ps)]
        m *= 2
    return Xs


def _tile_masks(C):
    sh = C.bit_length() - 1
    ri, ci = _iota2(LANES, LANES)
    same = jnp.right_shift(ri, sh) == jnp.right_shift(ci, sh)
    return ri, ci, same, same & (ri >= ci), same & (ri > ci)


def _place_rows(x, r0, n):
    parts = []
    if r0:
        parts.append(jnp.zeros((r0, x.shape[1]), x.dtype))
    parts.append(x)
    if n - r0 - x.shape[0]:
        parts.append(jnp.zeros((n - r0 - x.shape[0], x.shape[1]), x.dtype))
    return jnp.concatenate(parts, axis=0) if len(parts) > 1 else x


def _rwkv_tile(r, lw, k, v, kk, bb, C):
    ri, ci, same, tril, stril = _tile_masks(C)
    lane = lax.broadcasted_iota(jnp.int32, (1, LANES), 1)
    m = [(lane < RWKV_N).astype(F32), (lane >= RWKV_N).astype(F32)]
    trilf = tril.astype(F32)
    n = len(r)
    gc = [_mm_xl(trilf, x) for x in lw]
    at = [-kk[g] * jnp.exp(gc[g] - lw[g]) for g in range(n)]
    rt = [r[g] * jnp.exp(gc[g]) for g in range(n)]
    sc = []
    for g in range(n):
        e_neg = jnp.exp(-gc[g])
        lhs = jnp.concatenate([at[g] * m[0], at[g] * m[1], rt[g] * m[0], rt[g] * m[1]], axis=0)
        sc.append(_mm_nt(lhs, jnp.concatenate([bb[g] * e_neg, k[g] * e_neg], axis=0)))
    gh = [(g, hd) for g in range(n) for hd in range(2)]
    a_blk = [sc[g][LANES * hd:LANES * (hd + 1)] for g, hd in gh]
    r_blk = [sc[g][LANES * (2 + hd):LANES * (3 + hd)] for g, hd in gh]
    A_rb = [jnp.where(tril, x[:, :LANES], 0.0) for x in r_blk]
    A_rk = [jnp.where(tril, x[:, LANES:], 0.0) for x in r_blk]
    Tinv = _unit_lower_inv([jnp.where(stril, -x[:, :LANES], 0.0) for x in a_blk], C)
    akv = [_mm(jnp.where(stril, a_blk[p][:, LANES:], 0.0), v[g]) for p, (g, hd) in enumerate(gh)]
    y = [_mm(Tinv[p], jnp.concatenate([at[g] * m[hd], akv[p]], axis=1)) for p, (g, hd) in enumerate(gh)]
    o0 = [_mm(jnp.concatenate([A_rk[p], A_rb[p]], axis=1), jnp.concatenate([v[g], y[p][:, LANES:]], axis=0))
          for p, (g, hd) in enumerate(gh)]
    out = []
    for g in range(n):
        p0, p1 = 2 * g, 2 * g + 1
        out.append(dict(gc=gc[g], rt=rt[g], m=m, Wa=y[p0][:, :LANES] + y[p1][:, :LANES],
                        U0=y[p0][:, LANES:] * m[0] + y[p1][:, LANES:] * m[1],
                        O0=o0[p0] * m[0] + o0[p1] * m[1],
                        A_rb=jnp.concatenate([A_rb[p0], A_rb[p1]], axis=0), k=k[g], v=v[g], bb=bb[g]))
    return out


def _rwkv_block(i, C, ts, Ss):
    sl = slice(C * i, C * (i + 1))
    sl1 = slice(LANES + C * i, LANES + C * (i + 1))
    r2, c2 = _iota2(LANES, LANES)
    blk = ((r2 >= RWKV_N) == (c2 >= RWKV_N)).astype(F32)
    xs = [_mm_nt(jnp.concatenate([t['Wa'][sl], t['rt'][sl]], axis=0), S) for t, S in zip(ts, Ss)]
    S_new, ax = [], []
    for t, S, x in zip(ts, Ss, xs):
        X = x[:C]
        U = t['U0'][sl] + X
        gl = t['gc'][C * (i + 1) - 1:C * (i + 1), :]
        e_last = jnp.exp(gl - t['gc'][sl])
        S_new.append(S * jnp.exp(gl) + blk * _mm_tn(
            jnp.concatenate([U, t['v'][sl]], axis=0),
            jnp.concatenate([t['bb'][sl] * e_last, t['k'][sl] * e_last], axis=0)))
        ax.append(_mm(jnp.concatenate([t['A_rb'][sl], t['A_rb'][sl1]], axis=0), _place_rows(X, C * i, LANES)))
    o = [t['O0'][sl] + x[C:] + a[:C] * t['m'][0] + a[C:] * t['m'][1] for t, x, a in zip(ts, xs, ax)]
    return o, S_new


def _gdn_tile(q, k, v, g_row, beta_row, C):
    ri, ci, same, tril, stril = _tile_masks(C)
    eye = ri == ci
    lsum = lambda msk, x: jnp.sum(jnp.where(msk, x, 0.0), axis=1, keepdims=True)
    n = len(q)
    pre = []
    for g in range(n):
        g_b = jnp.broadcast_to(g_row[g], (LANES, LANES))
        beta_col = lsum(eye, jnp.broadcast_to(beta_row[g], (LANES, LANES)))
        g_col = lsum(eye, g_b)
        gc_col = lsum(tril, g_b)
        gl_col = lsum(same, g_b)
        gc_row = jnp.sum(jnp.where(same & (ri <= ci), jnp.broadcast_to(g_col, (LANES, LANES)), 0.0),
                         axis=0, keepdims=True)
        decay = jnp.where(tril, jnp.exp(jnp.minimum(gc_col - gc_row, 0.0)), 0.0)
        pre.append(dict(beta=beta_col, gc=gc_col, gl=gl_col, decay=decay, kb=k[g] * beta_col))
    sc = [_mm_nt(jnp.concatenate([pre[g]['kb'], q[g]], axis=0), k[g]) for g in range(n)]
    Tinv = _unit_lower_inv([jnp.where(stril, sc[g][:LANES] * pre[g]['decay'], 0.0) for g in range(n)], C)
    out = []
    for g in range(n):
        p = pre[g]
        e_gc = jnp.exp(p['gc'])
        uw = _mm(Tinv[g], jnp.concatenate([v[g] * p['beta'], p['kb'] * e_gc], axis=1))
        out.append(dict(u=uw[:, :LANES], wk=uw[:, LANES:], qk=sc[g][LANES:] * p['decay'], qg=q[g] * e_gc,
                        kd=k[g] * jnp.exp(p['gl'] - p['gc']), egl=jnp.exp(p['gl'])))
    return out


def _gdn_block(i, C, ts, Ss):
    sl = slice(C * i, C * (i + 1))
    ws = [_mm(jnp.concatenate([t['wk'][sl], t['qg'][sl]], axis=0), S) for t, S in zip(ts, Ss)]
    v_new = [t['u'][sl] - w[:C] for t, w in zip(ts, ws)]
    S_new = [S * t['egl'][C * i:C * i + 1, :] + _mm_tn(t['kd'][sl], vn) for t, S, vn in zip(ts, Ss, v_new)]
    o = [w[C:] + _mm(t['qk'][sl], _place_rows(vn, C * i, LANES)) for t, w, vn in zip(ts, ws, v_new)]
    return o, S_new


def _gla_tile(q_ref, k_ref, lf_ref, v_ref, probs, C, gc_scr, prod_scr):
    n = len(probs)
    nsb = LANES // GLA_SUB
    ri, ci, same, tril, _ = _tile_masks(C)
    sub = GLA_SUB.bit_length() - 1
    rsub, csub = jnp.right_shift(ri, sub), jnp.right_shift(ci, sub)
    trilf = tril.astype(F32)
    get = lambda ref: [_tile(ref, p) for p in probs]
    q, k, v = get(q_ref), get(k_ref), get(v_ref)
    gc = [_mm_xl(trilf, x) for x in get(lf_ref)]
    rk, ck = _iota2(GLA_SUB * LANES, LANES)
    sel = (jnp.right_shift(rk, LANES.bit_length() - 1) == ck).astype(BF16)
    g2 = [x * math.log2(math.e) for x in gc]
    for g, (h, j) in enumerate(probs):
        gc_scr[g] = g2[g]
        for I in range(nsb):
            s0 = GLA_SUB * I
            ksub = k[g][s0:s0 + GLA_SUB]
            gsub = g2[g][s0:s0 + GLA_SUB]
            for ii in range(GLA_SUB):
                i = s0 + ii
                gi = gc_scr[g, i:i + 1, :]
                qi = q_ref[LANES * j + i:LANES * j + i + 1, LANES * h:LANES * (h + 1)]
                prod_scr[g, s0:s0 + GLA_SUB, LANES * ii:LANES * (ii + 1)] = (
                    qi * ksub * jnp.exp2(jnp.minimum(gi - gsub, 0.0)))
    d = [_mm(prod_scr[g], sel) for g in range(n)]
    AT = []
    for g in range(n):
        strips = [d[g][:GLA_SUB]] + [pltpu.roll(d[g][GLA_SUB * I:GLA_SUB * (I + 1)], GLA_SUB * I, axis=1)
                                     for I in range(1, nsb)]
        AT.append(jnp.where((rsub == csub) & (ri <= ci), jnp.concatenate(strips, axis=0), 0.0))
    nsub = C // GLA_SUB
    if nsub > 1:
        for g in range(n):
            off = []
            for b in range(LANES // C):
                b0 = C * b
                kblk = k[g][b0:b0 + C]
                gblk = g2[g][b0:b0 + C]
                acc = jnp.zeros((C, LANES), F32)
                for J in range(1, nsub):
                    s0 = b0 + GLA_SUB * J
                    ref = gc_scr[g, s0 - 1:s0, :]
                    khat = kblk * jnp.exp2(jnp.minimum(ref - gblk, 0.0))
                    qhat = q[g][s0:s0 + GLA_SUB] * jnp.exp2(g2[g][s0:s0 + GLA_SUB] - ref)
                    acc = acc + _mm_nt(khat, _place_rows(qhat, s0, LANES))
                off.append(acc)
            AT[g] = AT[g] + jnp.where(same & (rsub < csub), jnp.concatenate(off, axis=0), 0.0)
    o_intra = [_mm_tn(AT[g], v[g]) for g in range(n)]
    return [dict(gc=gc[g], k=k[g], v=v[g], qg=q[g] * jnp.exp(gc[g]), o_intra=o_intra[g]) for g in range(n)]


def _gla_block(i, C, ts, STs):
    sl = slice(C * i, C * (i + 1))
    o, ST_new = [], []
    for t, ST in zip(ts, STs):
        gl = t['gc'][C * (i + 1) - 1:C * (i + 1), :]
        o.append(t['o_intra'][sl] + _mm_nt(t['qg'][sl], ST))
        ST_new.append(ST * jnp.exp(gl) + _mm_tn(t['v'][sl], t['k'][sl] * jnp.exp(gl - t['gc'][sl])))
    return o, ST_new


def _shifted(ext, hp, s, tm, back):
    return ext[hp - back * s:hp - back * s + tm, :]


def _rwkv_pre_kernel(has_vres, s, *refs):
    if has_vres:
        (x_ref, vf_ref, sh0_ref, nw_ref, mix_ref, wrkv_ref, w0_ref, w1_ref, w2_ref, a0_ref, a1_ref, a2_ref,
         g1_ref, g2_ref, kkw_ref, ka_ref, e_ref, v0_ref, v1_ref, v2_ref,
         r_out, lw_out, k_out, v_out, kk_out, bb_out, g_out, sh_out, ext) = refs
    else:
        (x_ref, sh0_ref, nw_ref, mix_ref, wrkv_ref, w0_ref, w1_ref, w2_ref, a0_ref, a1_ref, a2_ref,
         g1_ref, g2_ref, kkw_ref, ka_ref, e_ref,
         r_out, lw_out, k_out, v_out, kk_out, bb_out, g_out, sh_out, ext) = refs
    t = pl.program_id(1)
    tm, D = x_ref.shape
    hp = ext.shape[0] - tm
    h = _rmsnorm(x_ref[...], nw_ref[...])

    @pl.when(t == 0)
    def _():
        ext[hp - s:hp, :] = sh0_ref[...]

    ext[hp:, :] = h
    d = _shifted(ext, hp, s, tm, 1) - h
    ext[hp - s:hp, :] = h[tm - s:, :]

    @pl.when(t == pl.num_programs(1) - 1)
    def _():
        sh_out[...] = h[tm - s:, :]

    mixed = lambda i: h + d * mix_ref[i:i + 1, :]
    r = _mm(mixed(0), wrkv_ref[0])
    z = w0_ref[...] + _mm(jnp.tanh(_mm(mixed(1), w1_ref[...])), w2_ref[...])
    lw_out[...] = -math.exp(-0.5) * _sigmoid(z)
    k = _mm(mixed(2), wrkv_ref[1])
    xv = mixed(3)
    v = _mm(xv, wrkv_ref[2])
    a = _sigmoid(a0_ref[...] + _mm(_mm(mixed(4), a1_ref[...]), a2_ref[...]))
    if has_vres:
        gate_v = _sigmoid(v0_ref[...] + _mm(_mm(xv, v1_ref[...]), v2_ref[...]))
        v = v + (vf_ref[...] - v) * gate_v
    g_out[...] = _mm(_sigmoid(_mm(mixed(5), g1_ref[...])), g2_ref[...]).astype(g_out.dtype)
    r_out[...] = r.astype(r_out.dtype)
    v_out[...] = v
    kkraw = k * kkw_ref[...]
    e = e_ref[...]
    for c in range(D // LANES):
        sl = slice(c * LANES, (c + 1) * LANES)
        kc = kkraw[:, sl]
        kkn = kc * lax.rsqrt(_head_sum(kc * kc, e) + L2_EPS)
        kk_out[:, sl] = kkn.astype(kk_out.dtype)
        bb_out[:, sl] = (kkn * a[:, sl]).astype(bb_out.dtype)
    k_out[...] = (k * (1.0 + (a - 1.0) * ka_ref[...])).astype(k_out.dtype)


def _gdn_pre_kernel(s, nheads, x_ref, c0_ref, nw_ref, wqkv_ref, wgate_ref, wbg_ref, cw_ref, alog_ref, dtb_ref,
                    q_out, k_out, v_out, gate_out, bg_out, cnew_out, ext):
    t = pl.program_id(1)
    tm, D = x_ref.shape
    hp = ext.shape[0] - tm
    nh = cw_ref.shape[0] - 1
    h = _rmsnorm(x_ref[...], nw_ref[...])

    @pl.when(t == 0)
    def _():
        ext[hp - nh * s:hp, :] = c0_ref[...]

    ext[hp:, :] = _mm(h, wqkv_ref[...])
    y = _shifted(ext, hp, s, tm, 0) * cw_ref[nh:nh + 1, :]
    for j in range(nh):
        y = y + _shifted(ext, hp, s, tm, nh - j) * cw_ref[j:j + 1, :]
    hist = ext[hp + tm - nh * s:hp + tm, :]

    @pl.when(t == pl.num_programs(1) - 1)
    def _():
        cnew_out[...] = hist

    ext[hp - nh * s:hp, :] = hist
    y = _silu(y)
    qk_w = D
    for c in range(qk_w // LANES):
        sl = slice(c * LANES, (c + 1) * LANES)
        qc = y[:, sl]
        q_out[:, sl] = (qc * (lax.rsqrt(jnp.sum(qc * qc, axis=-1, keepdims=True) + L2_EPS) * LANES ** -0.5)
                        ).astype(q_out.dtype)
        sl2 = slice(qk_w + c * LANES, qk_w + (c + 1) * LANES)
        kc = y[:, sl2]
        k_out[:, sl] = (kc * lax.rsqrt(jnp.sum(kc * kc, axis=-1, keepdims=True) + L2_EPS)).astype(k_out.dtype)
    v_out[...] = y[:, 2 * qk_w:].astype(v_out.dtype)
    gate_out[...] = _mm(h, wgate_ref[...]).astype(gate_out.dtype)
    rest = _mm(h, wbg_ref[...])
    lane = lax.broadcasted_iota(jnp.int32, rest.shape, 1)
    bg_out[...] = jnp.where(lane < nheads, _sigmoid(rest),
                            -jnp.exp(alog_ref[...]) * _softplus(rest + dtb_ref[...]))


def _hgrn_pre_kernel(layer, x_ref, nw_ref, win_ref, lb_ref, q_out, k_out, lf_out, v_out, gate_out):
    D = x_ref.shape[1]
    h = _rmsnorm(x_ref[...], nw_ref[...])
    lbp = lb_ref[...]
    ex = jnp.exp(lbp - jnp.max(lbp, axis=0, keepdims=True))
    soft = ex / jnp.sum(ex, axis=0, keepdims=True)
    row = lax.broadcasted_iota(jnp.int32, soft.shape, 0)
    lb = jnp.sum(jnp.where((row >= 1) & (row <= layer), soft, 0.0), axis=0, keepdims=True)
    p = _mm(h, win_ref[...])
    q_out[...] = _silu(p[:, :D])
    f = lb + (1.0 - lb) * _sigmoid(p[:, D:2 * D])
    k_out[...] = 1.0 - f
    lf_out[...] = jnp.log(f)
    v_out[...] = p[:, 2 * D:3 * D].astype(v_out.dtype)
    gate_out[...] = p[:, 3 * D:].astype(gate_out.dtype)


def _ffn_kernel(s, y_ref, x_ref, c0_ref, wo_ref, mpw_ref, nw_ref, wup_ref, cw_ref, wdn_ref, pw_ref, x_out, cnew_out,
                ext):
    t = pl.program_id(1)
    tm, D = x_ref.shape
    hp = ext.shape[0] - tm
    nh = cw_ref.shape[0] - 1
    dff = wdn_ref.shape[0]
    x = x_ref[...] + _rmsnorm(_mm(y_ref[...], wo_ref[...]), mpw_ref[...])
    h = _rmsnorm(x, nw_ref[...])

    @pl.when(t == 0)
    def _():
        ext[hp - nh * s:hp, :] = c0_ref[...]

    ext[hp:, :] = _mm(h, wup_ref[...])
    y = _shifted(ext, hp, s, tm, 0) * cw_ref[nh:nh + 1, :]
    for j in range(nh):
        y = y + _shifted(ext, hp, s, tm, nh - j) * cw_ref[j:j + 1, :]
    hist = ext[hp + tm - nh * s:hp + tm, :]

    @pl.when(t == pl.num_programs(1) - 1)
    def _():
        cnew_out[...] = hist

    ext[hp - nh * s:hp, :] = hist
    act = _silu(y[:, dff:]) * y[:, :dff]
    x_out[...] = x + _rmsnorm(_mm(act, wdn_ref[...]), pw_ref[...])


def _const_spec(a):
    if isinstance(a, tuple):
        a, i = a
        nd = a.ndim - 1
        return pl.BlockSpec((None,) + a.shape[1:], lambda g, t: (i,) + (0,) * nd, pipeline_mode=pl.Buffered(1))
    nd = a.ndim
    return pl.BlockSpec(a.shape, lambda g, t: (0,) * nd, pipeline_mode=pl.Buffered(1))


def _row_call(body, name, tm, tiled_ins, group_ins, const_ins, tiled_out_widths, group_out_shapes, scratch):
    G, R, _ = tiled_ins[0].shape
    assert R % tm == 0
    tile_spec = lambda c: pl.BlockSpec((None, tm, c), lambda g, t: (g, t, 0))
    group_spec = lambda n, c: pl.BlockSpec((None, n, c), lambda g, t: (g, 0, 0))
    group_in_spec = lambda n, c: pl.BlockSpec((None, n, c), lambda g, t: (g, 0, 0), pipeline_mode=pl.Buffered(1))
    in_specs = ([tile_spec(a.shape[2]) for a in tiled_ins] + [group_in_spec(*a.shape[1:]) for a in group_ins]
                + [_const_spec(a) for a in const_ins])
    tiled_outs = [c if isinstance(c, tuple) else (c, F32) for c in tiled_out_widths]
    out_specs = [tile_spec(c) for c, _ in tiled_outs] + [group_spec(n, c) for n, c in group_out_shapes]
    out_shape = ([jax.ShapeDtypeStruct((G, R, c), dt) for c, dt in tiled_outs]
                 + [jax.ShapeDtypeStruct((G, n, c), F32) for n, c in group_out_shapes])
    return pl.pallas_call(
        body, name=name, grid=(G, R // tm), in_specs=in_specs, out_specs=out_specs, out_shape=out_shape,
        scratch_shapes=scratch,
        compiler_params=pltpu.CompilerParams(dimension_semantics=("parallel", "arbitrary"),
                                             vmem_limit_bytes=VMEM_LIMIT),
    )(*tiled_ins, *group_ins, *[a[0] if isinstance(a, tuple) else a for a in const_ins])


def _hist_pad(n):
    return -(-n // SUBLANES) * SUBLANES


def _tile(ref, p):
    h, j = p
    return ref[LANES * j:LANES * (j + 1), LANES * h:LANES * (h + 1)].astype(F32)


def _scan_states(chained, C, probs, tiles, scr, load, store, block_fn):
    nb = LANES // C
    tt = pl.program_id(2)
    heads = sorted({h for h, _ in probs})
    nt = len(probs) // len(heads)
    outs = {}
    if chained:
        @pl.when(tt == 0)
        def _():
            for h in heads:
                scr[h] = load((h,))

        Ss = [scr[h] for h in heads]
        for j in range(nt):
            sel = [tiles[probs.index((h, j))] for h in heads]
            for i in range(nb):
                os, Ss = block_fn(i, sel, Ss)
                for h, o in zip(heads, os):
                    outs[(h, j, i)] = o
        for h, S in zip(heads, Ss):
            scr[h] = S

        @pl.when(tt == pl.num_programs(2) - 1)
        def _():
            for h, S in zip(heads, Ss):
                store((h,), S)
    else:
        for i in range(nb):
            Ss = [load((j * nb + i, h)) for h, j in probs]
            os, Ss = block_fn(i, tiles, Ss)
            for (h, j), o, S in zip(probs, os, Ss):
                outs[(h, j, i)] = o
                store((j * nb + i, h), S)
    return [jnp.concatenate([outs[(h, j, i)] for i in range(nb)], axis=0) if nb > 1 else outs[(h, j, 0)]
            for h, j in probs]


def _scan_probs(ref):
    return [(h, j) for h in range(ref.shape[1] // LANES) for j in range(ref.shape[0] // LANES)]


def _rwkv_scan_kernel(C, chained, r_ref, lw_ref, k_ref, v_ref, kk_ref, bb_ref, g_ref, rk_ref, lnw_ref, lnb_ref,
                      e_ref, s0_ref, y_out, s_out, scr, bd):
    N = RWKV_N
    e = e_ref[...]
    probs = _scan_probs(r_ref)
    nt = r_ref.shape[0] // LANES
    nb = LANES // C
    get = lambda ref: [_tile(ref, p) for p in probs]
    r, k, v = get(r_ref), get(k_ref), get(v_ref)
    tiles = _rwkv_tile(r, get(lw_ref), k, v, get(kk_ref), get(bb_ref), C)

    bd[...] = jnp.zeros(bd.shape, F32)
    slot = lambda idx: idx[-1] * nt + (idx[0] // nb if len(idx) > 1 else 0)
    heads = lambda idx: (idx[:-1] + (2 * idx[-1],), idx[:-1] + (2 * idx[-1] + 1,))

    def load(idx):
        p = slot(idx)
        h0, h1 = heads(idx)
        bd[p, :N, :N] = s0_ref[h0]
        bd[p, N:, N:] = s0_ref[h1]
        return bd[p]

    def store(idx, S):
        p = slot(idx)
        h0, h1 = heads(idx)
        bd[p] = S
        s_out[h0] = bd[p, :N, :N]
        s_out[h1] = bd[p, N:, N:]

    O = _scan_states(chained, C, probs, tiles, scr, load, store, lambda i, ts, Ss: _rwkv_block(i, C, ts, Ss))
    np_ = len(probs)
    lanes = [slice(LANES * h, LANES * (h + 1)) for h, _ in probs]
    s1 = _head_sum(jnp.concatenate(O + [r[n] * k[n] * rk_ref[:, lanes[n]] for n in range(np_)], axis=0), e)
    d = [O[n] - s1[LANES * n:LANES * (n + 1)] * (1.0 / N) for n in range(np_)]
    s2 = _head_sum(jnp.concatenate([x * x for x in d], axis=0), e)
    for n, (h, j) in enumerate(probs):
        var = s2[LANES * n:LANES * (n + 1)] * (1.0 / N)
        on = d[n] * lax.rsqrt(var + RWKV_LNX_EPS) * lnw_ref[:, lanes[n]] + lnb_ref[:, lanes[n]]
        bonus = s1[LANES * (np_ + n):LANES * (np_ + n + 1)] * v[n]
        y_out[LANES * j:LANES * (j + 1), lanes[n]] = ((on + bonus) * _tile(g_ref, (h, j))).astype(y_out.dtype)


def _gdn_scan_kernel(C, chained, q_ref, k_ref, v_ref, gate_ref, g_ref, beta_ref, nw_ref, s0_ref, y_out, s_out,
                     scr):
    probs = _scan_probs(q_ref)
    get = lambda ref: [_tile(ref, p) for p in probs]
    tiles = _gdn_tile(get(q_ref), get(k_ref), get(v_ref), [g_ref[h, j:j + 1, :] for h, j in probs],
                      [beta_ref[h, j:j + 1, :] for h, j in probs], C)

    def store(idx, S):
        s_out[idx] = S

    O = _scan_states(chained, C, probs, tiles, scr, lambda idx: s0_ref[idx], store,
                     lambda i, ts, Ss: _gdn_block(i, C, ts, Ss))
    for n, (h, j) in enumerate(probs):
        y_out[LANES * j:LANES * (j + 1), LANES * h:LANES * (h + 1)] = (
            _rmsnorm(O[n], nw_ref[...]) * _silu(_tile(gate_ref, (h, j)))).astype(y_out.dtype)


def _gla_scan_kernel(C, chained, q_ref, k_ref, lf_ref, v_ref, gate_ref, nw_ref, s0_ref, y_out, s_out, scr, gc_scr,
                     prod_scr):
    probs = _scan_probs(q_ref)
    tiles = _gla_tile(q_ref, k_ref, lf_ref, v_ref, probs, C, gc_scr, prod_scr)

    def store(idx, ST):
        s_out[idx] = ST.T

    O = _scan_states(chained, C, probs, tiles, scr, lambda idx: s0_ref[idx].T, store,
                     lambda i, ts, Ss: _gla_block(i, C, ts, Ss))
    for n, (h, j) in enumerate(probs):
        y_out[LANES * j:LANES * (j + 1), LANES * h:LANES * (h + 1)] = (
            _rmsnorm(O[n], nw_ref[...]) * _silu(_tile(gate_ref, (h, j)))).astype(y_out.dtype)


def _scan_rows(chained, RB):
    return min(SCAN_ROWS_CHAINED if chained else SCAN_ROWS, RB)


def _scan_call(body, name, C, chained, tiles, extra_ins, extra_specs, s0, layer, nlayers, s_prev, nheads,
               extra_scratch=()):
    NB, RB, D = tiles[0].shape
    rows = _scan_rows(chained, RB)
    hg = SCAN_HEADS
    assert RB % rows == 0 and rows % LANES == 0 and nheads % hg == 0
    tile_spec = pl.BlockSpec((None, rows, hg * LANES), lambda b, h, t: (b, t, h))
    s_shape = s0.shape[-4:]
    hs = hg * (s_shape[1] // nheads)
    nt = RB // rows
    lead = (None,) if chained else (rows // C,)
    first = (lambda b, t: b) if chained else (lambda b, t: b * nt + t)

    def s_spec(l):
        if l is None:
            return pl.BlockSpec(lead + (hs,) + s_shape[2:], lambda b, h, t: (first(b, t), h, 0, 0))
        return pl.BlockSpec((None,) + lead + (hs,) + s_shape[2:], lambda b, h, t: (l, first(b, t), h, 0, 0))

    s0_layer = None if s0.ndim == 4 else layer
    n_in = len(tiles) + len(extra_ins) + 1
    kern = functools.partial(body, C, chained)
    ins = [*tiles, *extra_ins, s0]
    in_specs = [tile_spec] * len(tiles) + list(extra_specs) + [s_spec(s0_layer)]
    aliases = {}
    if s_prev is not None:
        ins.append(s_prev)
        in_specs.append(pl.BlockSpec(memory_space=pl.ANY))
        aliases = {n_in: 1}
        kern = lambda *refs: body(C, chained, *refs[:n_in], *refs[n_in + 1:])
    return pl.pallas_call(
        kern, name=name, grid=(NB, nheads // hg, RB // rows),
        in_specs=in_specs, out_specs=[tile_spec, s_spec(layer)],
        out_shape=[jax.ShapeDtypeStruct((NB, RB, D), BF16),
                   jax.ShapeDtypeStruct((nlayers,) + s_shape, F32)],
        scratch_shapes=[pltpu.VMEM((hg, LANES, LANES), F32)] + list(extra_scratch),
        input_output_aliases=aliases,
        compiler_params=pltpu.CompilerParams(dimension_semantics=("parallel", "parallel", "arbitrary"),
                                             vmem_limit_bytes=VMEM_LIMIT),
    )(*ins)


class _Group:
    def __init__(self, B, T, time_major):
        self.B, self.T, self.tm_major = B, T, time_major
        if time_major:
            self.G, self.R, self.s = 1, B * T, B
            self.C = -(-T // SUBLANES) * SUBLANES
            assert LANES % self.C == 0 and (B * self.C) % LANES == 0
        else:
            self.G, self.R, self.s = B, T, 1
            self.C = SCAN_CHUNK
            assert T % LANES == 0
        self.chained = not time_major
        self.tile = min(ROW_TILE, self.R)
        self.wide_tile = min(WIDE_TILE, self.R)
        self.ffn_tile = self.s if time_major else min(FFN_TILE, self.R)

    def to_rows(self, x):
        if self.tm_major:
            return jnp.swapaxes(x, 0, 1).reshape(1, self.R, x.shape[-1])
        return x

    def hist_to_rows(self, h):
        if self.tm_major:
            return jnp.swapaxes(h, 0, 1).reshape(1, -1, h.shape[-1])
        return h

    def hist_from_rows(self, h, n):
        if self.tm_major:
            return jnp.swapaxes(h.reshape(n, self.B, h.shape[-1]), 0, 1)
        return h

    def to_scan(self, a):
        if not self.tm_major:
            return a
        a = jnp.swapaxes(a.reshape(self.T, self.B, a.shape[-1]), 0, 1)
        a = jnp.pad(a, ((0, 0), (0, self.C - self.T), (0, 0)))
        return a.reshape(1, self.B * self.C, a.shape[-1])

    def from_scan(self, a):
        if not self.tm_major:
            return a
        return self.to_rows(a.reshape(self.B, self.C, a.shape[-1])[:, :self.T])


def _rwkv_layer(grp, x, shift0, S0, S_prev, v_first, P, j):
    D = x.shape[-1]
    s, tm = grp.s, grp.wide_tile
    N = RWKV_N
    has_vres = v_first is not None
    r2, c2 = _iota2(LANES, LANES)
    e = ((r2 // N) == (c2 // N)).astype(BF16)
    row = lambda a: a.reshape(1, -1)
    tiled = [x] + ([v_first] if has_vres else [])
    consts = [row(P['norm_mix_pre_i']), P['rwkv_mix'][j], (P['rwkv_w_rkv_bf'], j),
              row(P['rwkv_w0'][j]), P['rwkv_w1'][j].astype(BF16), P['rwkv_w2'][j].astype(BF16),
              row(P['rwkv_a0'][j]), P['rwkv_a1'][j].astype(BF16), P['rwkv_a2'][j].astype(BF16),
              P['rwkv_g1'][j].astype(BF16), P['rwkv_g2'][j].astype(BF16),
              row(P['rwkv_k_k'][j]), row(P['rwkv_k_a'][j]), e]
    if has_vres:
        consts += [row(P['rwkv_v0'][j - 1]), P['rwkv_v1'][j - 1].astype(BF16), P['rwkv_v2'][j - 1].astype(BF16)]
    r, lw, k, v, kk, bb, g, shift = _row_call(
        functools.partial(_rwkv_pre_kernel, has_vres, s), f"rwkv_pre_{j}", tm, tiled, [shift0], consts,
        [(D, BF16), D, (D, BF16), D, (D, BF16), (D, BF16), (D, BF16)], [(s, D)],
        [pltpu.VMEM((_hist_pad(s) + tm, D), F32)])
    if not has_vres:
        v_first = v
    tiles = [grp.to_scan(a) for a in (r, lw, k, v, kk, bb, g)]
    nprob = SCAN_HEADS * (_scan_rows(grp.chained, tiles[0].shape[1]) // LANES)
    vec_spec = pl.BlockSpec((1, SCAN_HEADS * LANES), lambda b, h, t: (0, h))
    e_spec = pl.BlockSpec((LANES, LANES), lambda b, h, t: (0, 0))
    y, S = _scan_call(_rwkv_scan_kernel, f"rwkv_scan_{j}", grp.C, grp.chained, tiles,
                      [row(P['rwkv_r_k'][j]), row(P['rwkv_lnx_w'][j]), row(P['rwkv_lnx_b'][j]), e],
                      [vec_spec] * 3 + [e_spec], S0[j], j, S0.shape[0], S_prev, D // LANES,
                      extra_scratch=[pltpu.VMEM((nprob, LANES, LANES), F32)])
    return grp.from_scan(y), (P['rwkv_w_o_bf'], j), shift, S, v_first


def _gdn_layer(grp, x, conv0, S0, S_prev, P, j):
    D = x.shape[-1]
    s, tm = grp.s, grp.wide_tile
    w_in = P['gdn_w_in'][j]
    cw = P['gdn_conv_w'][j]
    nh, cdim = cw.shape[0] - 1, cw.shape[1]
    H = S0.shape[2]
    assert grp.T >= nh
    row = lambda a: a.reshape(1, -1)
    lane_pad = lambda a: jnp.pad(a, ((0, 0), (0, LANES - a.shape[1])))
    zeros = jnp.zeros((1, H), F32)
    consts = [row(P['norm_mix_pre_i']), w_in[:, :cdim].astype(BF16), w_in[:, cdim:cdim + D].astype(BF16),
              lane_pad(w_in[:, cdim + D:]).astype(BF16), cw,
              lane_pad(jnp.concatenate([zeros, row(P['gdn_a_log'][j])], axis=1)),
              lane_pad(jnp.concatenate([zeros, row(P['gdn_dt_bias'][j])], axis=1))]
    q, k, v, gate, bg, conv_new = _row_call(
        functools.partial(_gdn_pre_kernel, s, H), f"gdn_pre_{j}", tm, [x], [conv0], consts,
        [(D, BF16)] * 4 + [LANES], [(nh * s, cdim)], [pltpu.VMEM((_hist_pad(nh * s) + tm, cdim), F32)])
    tiles = [grp.to_scan(a) for a in (q, k, v, gate)]
    NB, RB, _ = tiles[0].shape
    rows = _scan_rows(grp.chained, RB)
    nt = RB // rows
    bg = grp.to_scan(bg[:, :, :2 * H]).reshape(NB * RB, 2 * H).T.reshape(2 * H, NB * nt, rows // LANES, LANES)
    hg = SCAN_HEADS
    g_spec = pl.BlockSpec((hg, None, rows // LANES, LANES), lambda b, h, t: (H // hg + h, b * nt + t, 0, 0))
    beta_spec = pl.BlockSpec((hg, None, rows // LANES, LANES), lambda b, h, t: (h, b * nt + t, 0, 0))
    vec_spec = pl.BlockSpec((1, LANES), lambda b, h, t: (0, 0))
    y, S = _scan_call(_gdn_scan_kernel, f"gdn_scan_{j}", grp.C, grp.chained, tiles,
                      [bg, bg, row(P['gdn_norm_w'][j])], [g_spec, beta_spec, vec_spec], S0, j, S0.shape[0], S_prev, H)
    return grp.from_scan(y), (P['gdn_w_o_bf'], j), conv_new, S


def _hgrn_layer(grp, x, S0, S_prev, P, i, j):
    D = x.shape[-1]
    tm = grp.wide_tile
    H = S0.shape[2]
    row = lambda a: a.reshape(1, -1)
    consts = [row(P['norm_mix_pre_i']), (P['hgrn_w_in_bf'], j), P['hgrn_lb']]
    q, k, lf, v, gate = _row_call(functools.partial(_hgrn_pre_kernel, i), f"hgrn_pre_{j}", tm, [x], [], consts,
                                  [D, D, D, (D, BF16), (D, BF16)], [], [])
    tiles = [grp.to_scan(a) for a in (q, k, lf, v, gate)]
    nprob = SCAN_HEADS * (_scan_rows(grp.chained, tiles[0].shape[1]) // LANES)
    vec_spec = pl.BlockSpec((1, LANES), lambda b, h, t: (0, 0))
    y, S = _scan_call(_gla_scan_kernel, f"hgrn_scan_{j}", grp.C, grp.chained, tiles, [row(P['hgrn_norm_w'][j])],
                      [vec_spec], S0, j, S0.shape[0], S_prev, H,
                      extra_scratch=[pltpu.VMEM((nprob, LANES, LANES), F32),
                                     pltpu.VMEM((nprob, LANES, GLA_SUB * LANES), F32)])
    return grp.from_scan(y), (P['hgrn_w_o_bf'], j), S


def _trunk(grp, x, shift0, wkv0, gconv0, gS0, hS0, fconv0, P):
    D = x.shape[-1]
    depth = P['norm_mix_pre'].shape[0]
    row = lambda a: a.reshape(1, -1)
    x = grp.to_rows(x)
    v_first = None
    shift, gconv, fconv = [], [], []
    wkv = gS = hS = None
    for i in range(depth):
        kind, j = i % 3, i // 3
        P = dict(P, norm_mix_pre_i=P['norm_mix_pre'][i])
        if kind == 0:
            y, w_o, s_shift, wkv, v_first = _rwkv_layer(grp, x, grp.hist_to_rows(shift0[j][:, None]), wkv0, wkv,
                                                       v_first, P, j)
            shift.append(grp.hist_from_rows(s_shift, 1)[:, 0])
        elif kind == 1:
            y, w_o, c_new, gS = _gdn_layer(grp, x, grp.hist_to_rows(gconv0[j]), gS0, gS, P, j)
            gconv.append(grp.hist_from_rows(c_new, gconv0.shape[2]))
        else:
            y, w_o, hS = _hgrn_layer(grp, x, hS0, hS, P, i, j)
        nh = P['ffn_conv_w'].shape[1] - 1
        dff2 = P['ffn_w_up'].shape[2]
        x, c_new = _row_call(
            functools.partial(_ffn_kernel, grp.s), f"ffn_{i}", grp.ffn_tile, [y, x], [grp.hist_to_rows(fconv0[i])],
            [w_o, row(P['norm_mix_post'][i]),
             row(P['norm_ffn_pre'][i]), (P['ffn_w_up_bf'], i), P['ffn_conv_w'][i],
             (P['ffn_w_down_bf'], i), row(P['norm_ffn_post'][i])],
            [D], [(nh * grp.s, dff2)], [pltpu.VMEM((_hist_pad(nh * grp.s) + grp.ffn_tile, dff2), F32)])
        fconv.append(grp.hist_from_rows(c_new, nh))
    y = x.reshape(grp.T, grp.B, D).swapaxes(0, 1) if grp.tm_major else x
    return y, (jnp.stack(shift), wkv, jnp.stack(gconv), gS, hS, jnp.stack(fconv))


def kernel(x_prompt, x_sample, state_rwkv_shift, state_rwkv_wkv, state_gdn_conv, state_gdn_S, state_hgrn_S, state_ffn_conv, norm_mix_pre, norm_mix_post, norm_ffn_pre, norm_ffn_post, rwkv_mix, rwkv_w_rkv, rwkv_w0, rwkv_w1, rwkv_w2, rwkv_a0, rwkv_a1, rwkv_a2, rwkv_v0, rwkv_v1, rwkv_v2, rwkv_g1, rwkv_g2, rwkv_k_k, rwkv_k_a, rwkv_r_k, rwkv_lnx_w, rwkv_lnx_b, rwkv_w_o, gdn_w_in, gdn_conv_w, gdn_a_log, gdn_dt_bias, gdn_norm_w, gdn_w_o, hgrn_w_in, hgrn_lb, hgrn_norm_w, hgrn_w_o, ffn_w_up, ffn_conv_w, ffn_w_down):
    P = dict(norm_mix_pre=norm_mix_pre, norm_mix_post=norm_mix_post, norm_ffn_pre=norm_ffn_pre,
             norm_ffn_post=norm_ffn_post, rwkv_mix=rwkv_mix, rwkv_w_rkv=rwkv_w_rkv, rwkv_w0=rwkv_w0,
             rwkv_w1=rwkv_w1, rwkv_w2=rwkv_w2, rwkv_a0=rwkv_a0, rwkv_a1=rwkv_a1, rwkv_a2=rwkv_a2,
             rwkv_v0=rwkv_v0, rwkv_v1=rwkv_v1, rwkv_v2=rwkv_v2, rwkv_g1=rwkv_g1, rwkv_g2=rwkv_g2,
             rwkv_k_k=rwkv_k_k, rwkv_k_a=rwkv_k_a, rwkv_r_k=rwkv_r_k, rwkv_lnx_w=rwkv_lnx_w,
             rwkv_lnx_b=rwkv_lnx_b, rwkv_w_o=rwkv_w_o, gdn_w_in=gdn_w_in, gdn_conv_w=gdn_conv_w,
             gdn_a_log=gdn_a_log, gdn_dt_bias=gdn_dt_bias, gdn_norm_w=gdn_norm_w, gdn_w_o=gdn_w_o,
             hgrn_w_in=hgrn_w_in, hgrn_lb=hgrn_lb, hgrn_norm_w=hgrn_norm_w, hgrn_w_o=hgrn_w_o,
             ffn_w_up=ffn_w_up, ffn_conv_w=ffn_conv_w, ffn_w_down=ffn_w_down)
    for name in ('rwkv_w_rkv', 'rwkv_w_o', 'gdn_w_o', 'hgrn_w_in', 'hgrn_w_o', 'ffn_w_up', 'ffn_w_down'):
        P[name + '_bf'] = P[name].astype(BF16)
    Bp, Tp, _ = x_prompt.shape
    Bs, Ts, _ = x_sample.shape
    zero_like = lambda st: jnp.zeros((st.shape[0], Bp) + st.shape[2:], st.dtype)
    y_p, (p_shift, p_wkv, p_gconv, p_gS, p_hS, p_fconv) = _trunk(
        _Group(Bp, Tp, False), x_prompt, zero_like(state_rwkv_shift), zero_like(state_rwkv_wkv),
        zero_like(state_gdn_conv), zero_like(state_gdn_S), zero_like(state_hgrn_S), zero_like(state_ffn_conv), P)
    y_s, (s_shift, s_wkv, s_gconv, s_gS, s_hS, s_fconv) = _trunk(
        _Group(Bs, Ts, True), x_sample, state_rwkv_shift, state_rwkv_wkv, state_gdn_conv, state_gdn_S,
        state_hgrn_S, state_ffn_conv, P)
    return (y_p, y_s, p_shift, s_shift, p_wkv, s_wkv, p_gconv, s_gconv,
            p_gS, s_gS, p_hS, s_hS, p_fconv, s_fconv)
```
